```python
import jax
import jax.numpy as jnp
from jax import lax
import numpy as np


D_MODEL = 2048
BATCH = 1
SEQ = 8192
DEPTH = 2

EPS = 1e-6
D_FF = 5632
FFN_RES_SCALE = 0.5

DN_HEADS = 8
DN_DK = 128
DN_DV = 128
DN_CONV = 4
DN_CHUNK = 64

GLA_HEADS = 4
GLA_DK = 128
GLA_DV = 256
GLA_RANK = 16
GLA_TAU = 16.0
GLA_CHUNK = 64

SWA_HQ = 16
SWA_HKV = 4
SWA_HD = 64
SWA_WINDOW = 128
SWA_BLOCK = 128

N_BRANCH = 3
DN_QK = DN_HEADS * DN_DK
DN_WIDTH = DN_HEADS * DN_DV
GLA_QK = GLA_HEADS * GLA_DK
GLA_WIDTH = GLA_HEADS * GLA_DV
SWA_WIDTH = SWA_HQ * SWA_HD
SWA_KV = SWA_HKV * SWA_HD

IN_SPLITS = (DN_QK, DN_QK, DN_WIDTH, DN_WIDTH, DN_HEADS, DN_HEADS,
             GLA_QK, GLA_QK, GLA_WIDTH, GLA_WIDTH, GLA_RANK,
             SWA_WIDTH, SWA_KV, SWA_KV, N_BRANCH * D_MODEL)
D_IN = sum(IN_SPLITS)

kernel_name = 'hybrid_deltanet_gla_swa_macaron'


def rms_norm(x, g):
    xf = x.astype(jnp.float32)
    y = xf * lax.rsqrt(jnp.mean(xf * xf, axis=-1, keepdims=True) + EPS)
    return (y * g.astype(jnp.float32)).astype(x.dtype)


def l2norm(x):
    return x * lax.rsqrt(jnp.sum(x * x, axis=-1, keepdims=True) + EPS)


def swiglu(h, w1, w3, w2):
    return (jax.nn.silu(h @ w1) * (h @ w3)) @ w2


def causal_depthwise_conv(x, w):
    return lax.conv_general_dilated(
        x, w[:, None, :].astype(x.dtype), window_strides=(1,),
        padding=[(w.shape[0] - 1, 0)], dimension_numbers=('NWC', 'WIO', 'NWC'),
        feature_group_count=x.shape[-1])


def split_heads(t, n_heads):
    return t.reshape(t.shape[:-1] + (n_heads, t.shape[-1] // n_heads))


def split_cols(u):
    parts = []
    off = 0
    for n in IN_SPLITS:
        parts.append(u[..., off:off + n])
        off += n
    return parts


def to_chunks(t, c):
    b, tl, h = t.shape[:3]
    t = t.reshape((b, tl // c, c, h) + t.shape[3:])
    return jnp.moveaxis(t, 3, 1)


def from_chunks(o):
    n, b, h, c, d = o.shape
    return o.transpose(1, 0, 3, 2, 4).reshape(b, n * c, h, d)


def gated_delta_rule(q, k, v, a, b, a_log, dt_bias):
    f32 = jnp.float32
    out_dtype = v.dtype
    dk = q.shape[-1]
    dv = v.shape[-1]
    c = DN_CHUNK
    q = l2norm(q.astype(f32)) * (dk ** -0.5)
    k = l2norm(k.astype(f32))
    v = v.astype(f32)
    beta = jax.nn.sigmoid(b.astype(f32))
    g = -jnp.exp(a_log.astype(f32)) * jax.nn.softplus(a.astype(f32) + dt_bias.astype(f32))
    q, k, v, beta, g = (to_chunks(t, c) for t in (q, k, v, beta, g))
    G = jnp.cumsum(g, axis=-1)
    idx = jnp.arange(c)
    causal = idx[:, None] >= idx[None, :]
    strict = idx[:, None] > idx[None, :]
    decay = jnp.exp(jnp.where(causal, G[..., :, None] - G[..., None, :], -jnp.inf))
    k_beta = k * beta[..., None]
    A = jnp.where(strict, jnp.einsum('bhnid,bhnjd->bhnij', k_beta, k) * decay, 0.0)
    rhs = jnp.concatenate([v * beta[..., None], k_beta * jnp.exp(G)[..., None]], axis=-1)
    sol = lax.linalg.triangular_solve(A, rhs, left_side=True, lower=True, unit_diagonal=True)
    value, k_cum = sol[..., :dv], sol[..., dv:]
    attn_intra = jnp.einsum('bhnid,bhnjd->bhnij', q, k) * decay
    q_dec = q * jnp.exp(G)[..., None]
    k_dec = k * jnp.exp(G[..., -1:] - G)[..., None]
    g_last = jnp.exp(G[..., -1])
    xs = tuple(jnp.moveaxis(t, 2, 0) for t in (q_dec, attn_intra, value, k_cum, k_dec, g_last))

    def step(S, inp):
        qd, at, val, kc, kd, gl = inp
        v_new = val - jnp.einsum('bhcd,bhde->bhce', kc, S)
        o = jnp.einsum('bhcd,bhde->bhce', qd, S) + jnp.einsum('bhij,bhje->bhie', at, v_new)
        S = S * gl[..., None, None] + jnp.einsum('bhcd,bhce->bhde', kd, v_new)
        return S, o

    S0 = jnp.zeros((q.shape[0], q.shape[1], dk, dv), f32)
    _, o = lax.scan(step, S0, xs)
    return from_chunks(o).astype(out_dtype)


def gla_chunked(q, k, v, log_alpha):
    f32 = jnp.float32
    out_dtype = v.dtype
    dk = q.shape[-1]
    dv = v.shape[-1]
    c = GLA_CHUNK
    q = q.astype(f32) * (dk ** -0.5)
    q, k, v, la = (to_chunks(t.astype(f32), c) for t in (q, k, v, log_alpha))
    G = jnp.cumsum(la, axis=-2)
    qg = q * jnp.exp(G)
    kg = k * jnp.exp(-G)
    idx = jnp.arange(c)
    causal = idx[:, None] >= idx[None, :]
    A = jnp.where(causal, jnp.einsum('bhnid,bhnjd->bhnij', qg, kg), 0.0)
    o_intra = jnp.einsum('bhnij,bhnje->bhnie', A, v)
    k_dec = k * jnp.exp(G[..., -1:, :] - G)
    g_last = jnp.exp(G[..., -1, :])
    xs = tuple(jnp.moveaxis(t, 2, 0) for t in (qg, k_dec, v, g_last))

    def step(S, inp):
        qq, kd, vv, gl = inp
        o = jnp.einsum('bhcd,bhde->bhce', qq, S)
        S = S * gl[..., :, None] + jnp.einsum('bhcd,bhce->bhde', kd, vv)
        return S, o

    S0 = jnp.zeros((q.shape[0], q.shape[1], dk, dv), f32)
    _, o_inter = lax.scan(step, S0, xs)
    o = from_chunks(o_inter) + from_chunks(jnp.moveaxis(o_intra, 2, 0))
    return o.astype(out_dtype)


def swa_with_sinks(q, k, v, sinks):
    b, tl, hq, hd = q.shape
    hkv = k.shape[2]
    grp = hq // hkv
    w = SWA_BLOCK
    n = tl // w
    qb = q.reshape(b, n, w, hkv, grp, hd)
    kb = k.reshape(b, n, w, hkv, hd)
    vb = v.reshape(b, n, w, hkv, hd)
    pad = ((0, 0), (1, 0), (0, 0), (0, 0), (0, 0))
    kk = jnp.concatenate([jnp.pad(kb, pad)[:, :-1], kb], axis=2)
    vv = jnp.concatenate([jnp.pad(vb, pad)[:, :-1], vb], axis=2)
    s = jnp.einsum('bnqhgd,bnkhd->bnhgqk', qb, kk).astype(jnp.float32) * (hd ** -0.5)
    qi = jnp.arange(w)[:, None]
    ki = jnp.arange(2 * w)[None, :]
    band = (ki <= qi + w) & (ki > qi + w - SWA_WINDOW)
    nblk = jnp.arange(n)[:, None, None]
    mask = band[None] & ((nblk > 0) | (ki[None] >= w))
    s = jnp.where(mask[None, :, None, None], s, -jnp.inf)
    sink = sinks.astype(jnp.float32).reshape(hkv, grp)[None, None, :, :, None, None]
    m = jnp.maximum(jnp.max(s, axis=-1, keepdims=True), sink)
    p = jnp.exp(s - m)
    p = p / (jnp.sum(p, axis=-1, keepdims=True) + jnp.exp(sink - m))
    o = jnp.einsum('bnhgqk,bnkhd->bnqhgd', p.astype(v.dtype), vv)
    return o.reshape(b, tl, hq * hd)


def hybrid_mixer(h, w_in, dn_conv, dn_a_log, dn_dt_bias, dn_out_norm, gla_gate_up, gla_gate_bias,
                 gla_out_norm, swa_q_norm, swa_k_norm, swa_sinks, w_branch_dn, w_branch_gla,
                 w_branch_swa, w_out):
    b, tl, _ = h.shape
    u = h @ w_in
    (dn_q, dn_k, dn_v, dn_z, dn_a, dn_b, gla_q, gla_k, gla_v, gla_r, gla_lr,
     swa_q, swa_k, swa_v, gate_logits) = split_cols(u)
    qkv = jax.nn.silu(causal_depthwise_conv(jnp.concatenate([dn_q, dn_k, dn_v], axis=-1), dn_conv))
    cq, ck, cv = qkv[..., :DN_QK], qkv[..., DN_QK:2 * DN_QK], qkv[..., 2 * DN_QK:]
    o_dn = gated_delta_rule(split_heads(cq, DN_HEADS), split_heads(ck, DN_HEADS),
                            split_heads(cv, DN_HEADS), dn_a, dn_b, dn_a_log, dn_dt_bias)
    o_dn = (rms_norm(o_dn, dn_out_norm) * jax.nn.silu(split_heads(dn_z, DN_HEADS))).reshape(b, tl, DN_WIDTH)
    log_alpha = jax.nn.log_sigmoid((gla_lr @ gla_gate_up + gla_gate_bias).astype(jnp.float32)) / GLA_TAU
    o_gla = gla_chunked(split_heads(gla_q, GLA_HEADS), split_heads(gla_k, GLA_HEADS),
                        split_heads(gla_v, GLA_HEADS), split_heads(log_alpha, GLA_HEADS))
    o_gla = (rms_norm(o_gla, gla_out_norm) * jax.nn.silu(split_heads(gla_r, GLA_HEADS))).reshape(b, tl, GLA_WIDTH)
    sq = rms_norm(split_heads(swa_q, SWA_HQ), swa_q_norm)
    sk = rms_norm(split_heads(swa_k, SWA_HKV), swa_k_norm)
    o_swa = swa_with_sinks(sq, sk, split_heads(swa_v, SWA_HKV), swa_sinks)
    gates = jax.nn.sigmoid(gate_logits).reshape(b, tl, N_BRANCH, D_MODEL)
    y = (gates[..., 0, :] * (o_dn @ w_branch_dn)
         + gates[..., 1, :] * (o_gla @ w_branch_gla)
         + gates[..., 2, :] * (o_swa @ w_branch_swa))
    return y @ w_out


def setup_inputs(seed: int = 0) -> dict:
    key = jax.random.key(seed)
    ks = jax.random.split(key, 32)
    f32 = jnp.float32
    L = DEPTH

    def nrm(k, shape, fan_in):
        return jax.random.normal(k, shape, f32) * (fan_in ** -0.5)

    def gain(k, shape):
        return 1.0 + 0.01 * jax.random.normal(k, shape, f32)

    dt = jnp.exp(jax.random.uniform(ks[9], (L, DN_HEADS), f32) * (np.log(0.1) - np.log(0.001)) + np.log(0.001))
    return {
        'x': jax.random.normal(ks[0], (BATCH, SEQ, D_MODEL), f32),
        'ffn1_norm': gain(ks[1], (L, D_MODEL)),
        'ffn1_w1': nrm(ks[2], (L, D_MODEL, D_FF), D_MODEL),
        'ffn1_w3': nrm(ks[3], (L, D_MODEL, D_FF), D_MODEL),
        'ffn1_w2': nrm(ks[4], (L, D_FF, D_MODEL), D_FF),
        'mix_norm': gain(ks[5], (L, D_MODEL)),
        'w_in': nrm(ks[6], (L, D_MODEL, D_IN), D_MODEL),
        'dn_conv': nrm(ks[7], (L, DN_CONV, 2 * DN_QK + DN_WIDTH), DN_CONV),
        'dn_a_log': jnp.log(jax.random.uniform(ks[8], (L, DN_HEADS), f32, 1.0, 16.0)),
        'dn_dt_bias': dt + jnp.log(-jnp.expm1(-dt)),
        'dn_out_norm': gain(ks[10], (L, DN_DV)),
        'gla_gate_up': nrm(ks[11], (L, GLA_RANK, GLA_QK), GLA_RANK),
        'gla_gate_bias': 0.1 * jax.random.normal(ks[12], (L, GLA_QK), f32),
        'gla_out_norm': gain(ks[13], (L, GLA_DV)),
        'swa_q_norm': gain(ks[14], (L, SWA_HD)),
        'swa_k_norm': gain(ks[15], (L, SWA_HD)),
        'swa_sinks': jax.random.normal(ks[16], (L, SWA_HQ), f32),
        'w_branch_dn': nrm(ks[17], (L, DN_WIDTH, D_MODEL), DN_WIDTH),
        'w_branch_gla': nrm(ks[18], (L, GLA_WIDTH, D_MODEL), GLA_WIDTH),
        'w_branch_swa': nrm(ks[19], (L, SWA_WIDTH, D_MODEL), SWA_WIDTH),
        'w_out': nrm(ks[20], (L, D_MODEL, D_MODEL), D_MODEL),
        'ffn2_norm': gain(ks[21], (L, D_MODEL)),
        'ffn2_w1': nrm(ks[22], (L, D_MODEL, D_FF), D_MODEL),
        'ffn2_w3': nrm(ks[23], (L, D_MODEL, D_FF), D_MODEL),
        'ffn2_w2': nrm(ks[24], (L, D_FF, D_MODEL), D_FF),
    }


def reference(x, ffn1_norm, ffn1_w1, ffn1_w3, ffn1_w2, mix_norm, w_in, dn_conv, dn_a_log, dn_dt_bias,
              dn_out_norm, gla_gate_up, gla_gate_bias, gla_out_norm, swa_q_norm, swa_k_norm, swa_sinks,
              w_branch_dn, w_branch_gla, w_branch_swa, w_out, ffn2_norm, ffn2_w1, ffn2_w3, ffn2_w2):
    for l in range(DEPTH):
        x = x + FFN_RES_SCALE * swiglu(rms_norm(x, ffn1_norm[l]), ffn1_w1[l], ffn1_w3[l], ffn1_w2[l])
        x = x + hybrid_mixer(rms_norm(x, mix_norm[l]), w_in[l], dn_conv[l], dn_a_log[l], dn_dt_bias[l],
                             dn_out_norm[l], gla_gate_up[l], gla_gate_bias[l], gla_out_norm[l],
                             swa_q_norm[l], swa_k_norm[l], swa_sinks[l], w_branch_dn[l],
                             w_branch_gla[l], w_branch_swa[l], w_out[l])
        x = x + FFN_RES_SCALE * swiglu(rms_norm(x, ffn2_norm[l]), ffn2_w1[l], ffn2_w3[l], ffn2_w2[l])
    return x
```

```python
import functools

import jax
import jax.numpy as jnp
from jax import lax
from jax.experimental import pallas as pl
from jax.experimental.pallas import tpu as pltpu

F32 = jnp.float32
BF16 = jnp.bfloat16

D_MODEL = 2048
DEPTH = 2
EPS = 1e-6
D_FF = 5632
FFN_RES_SCALE = 0.5
DN_HEADS, DN_DK, DN_DV, DN_CONV = 8, 128, 128, 4
GLA_HEADS, GLA_DK, GLA_DV, GLA_RANK, GLA_TAU = 4, 128, 256, 16, 16.0
SWA_HQ, SWA_HKV, SWA_HD, SWA_WINDOW = 16, 4, 64, 128
N_BRANCH = 3
CHUNK = 64
DN_QK = DN_HEADS * DN_DK
DN_WIDTH = DN_HEADS * DN_DV
GLA_QK = GLA_HEADS * GLA_DK
GLA_WIDTH = GLA_HEADS * GLA_DV
SWA_WIDTH = SWA_HQ * SWA_HD
SWA_KV = SWA_HKV * SWA_HD

_SPLITS = (DN_QK, DN_QK, DN_WIDTH, DN_WIDTH, DN_HEADS, DN_HEADS,
           GLA_QK, GLA_QK, GLA_WIDTH, GLA_WIDTH, GLA_RANK,
           SWA_WIDTH, SWA_KV, SWA_KV, N_BRANCH * D_MODEL)
_OFFS = tuple(sum(_SPLITS[:i]) for i in range(len(_SPLITS) + 1))
(W_DNQ, W_DNK, W_DNV, W_DNZ, W_DNA, W_DNB, W_GLQ, W_GLK, W_GLV, W_GLR, W_GLLR,
 W_SWQ, W_SWK, W_SWV, W_GATE, W_END) = _OFFS

LANES = 128
SUBLANES = 8
PAIR = 2 * CHUNK

U_DN = 0
U_GLQ = U_DN + 4 * DN_QK
U_GLK = U_GLQ + GLA_QK
U_GLV = U_GLK + GLA_QK
U_GLR = U_GLV + GLA_WIDTH
U_SWQ = U_GLR + GLA_WIDTH
U_SWK = U_SWQ + SWA_WIDTH
U_SWV = U_SWK + SWA_HKV * LANES
U_GATE = U_SWV + SWA_HKV * LANES
U_SMALL = U_GATE + N_BRANCH * D_MODEL
U_COLS = 16384
SM_A, SM_B, SM_LR = 0, DN_HEADS, 2 * DN_HEADS

VMEM_LIMIT = 56 * 1024 * 1024


def _cparams(n_axes):
    return pltpu.CompilerParams(dimension_semantics=("arbitrary",) * n_axes,
                                vmem_limit_bytes=VMEM_LIMIT)


def _bf(t):
    return t.astype(BF16)


def _mm(a, b):
    return jnp.dot(_bf(a), _bf(b), preferred_element_type=F32)


def _mm_nt(a, b):
    return lax.dot_general(_bf(a), _bf(b), (((1,), (1,)), ((), ())), preferred_element_type=F32)


def _mm_tn(a, b):
    return lax.dot_general(_bf(a), _bf(b), (((0,), (0,)), ((), ())), preferred_element_type=F32)


def _silu(t):
    return t * jax.nn.sigmoid(t)


def _rms_rows(x, gain):
    ms = jnp.mean(x * x, axis=-1, keepdims=True)
    return x * lax.rsqrt(ms + EPS) * gain


def _chunk_cumsum(tri, t):
    hi = _bf(t)
    r1 = t - hi.astype(F32)
    mid = _bf(r1)
    lo = _bf(r1 - mid.astype(F32))
    dot = functools.partial(jnp.dot, preferred_element_type=F32)
    return dot(tri, hi) + dot(tri, mid) + dot(tri, lo)


def _pair_masks():
    sub = lax.broadcasted_iota(jnp.int32, (PAIR, PAIR), 0)
    lane = lax.broadcasted_iota(jnp.int32, (PAIR, PAIR), 1)
    same = (sub >= CHUNK) == (lane >= CHUNK)
    return sub, lane, same


def _ffn_kernel(x_ref, g_ref, w1_ref, w3_ref, w2_ref, o_ref, h_ref):
    @pl.when(pl.program_id(1) == 0)
    def _():
        x = x_ref[...]
        h_ref[...] = _bf(_rms_rows(x, g_ref[...]))
        o_ref[...] = x

    h = h_ref[...]
    a = jnp.dot(h, w1_ref[...], preferred_element_type=F32)
    b = jnp.dot(h, w3_ref[...], preferred_element_type=F32)
    act = _silu(a) * (b * FFN_RES_SCALE)
    o_ref[...] += jnp.dot(_bf(act), w2_ref[...], preferred_element_type=F32)


def _ffn(x, gain, w1, w3, w2, *, tm=1024, tf=512):
    t, d = x.shape
    f = w1.shape[1]
    return pl.pallas_call(
        _ffn_kernel,
        grid=(t // tm, f // tf),
        in_specs=[
            pl.BlockSpec((tm, d), lambda i, j: (i, 0), pipeline_mode=pl.Buffered(1)),
            pl.BlockSpec((1, d), lambda i, j: (0, 0)),
            pl.BlockSpec((d, tf), lambda i, j: (0, j)),
            pl.BlockSpec((d, tf), lambda i, j: (0, j)),
            pl.BlockSpec((tf, d), lambda i, j: (j, 0)),
        ],
        out_specs=pl.BlockSpec((tm, d), lambda i, j: (i, 0)),
        out_shape=jax.ShapeDtypeStruct((t, d), F32),
        scratch_shapes=[pltpu.VMEM((tm, d), BF16)],
        compiler_params=_cparams(2),
        name="ffn",
    )(x, gain, w1, w3, w2)


def _inproj_kernel(x_ref, g_ref, w_ref, u_ref, h_ref):
    @pl.when(pl.program_id(1) == 0)
    def _():
        h_ref[...] = _bf(_rms_rows(x_ref[...], g_ref[...]))

    u_ref[...] = jnp.dot(h_ref[...], w_ref[...], preferred_element_type=F32)


def _inproj(x, gain, w, *, tm=1024, tn=2048):
    t, d = x.shape
    n = w.shape[1]
    return pl.pallas_call(
        _inproj_kernel,
        grid=(t // tm, n // tn),
        in_specs=[
            pl.BlockSpec((tm, d), lambda i, j: (i, 0), pipeline_mode=pl.Buffered(1)),
            pl.BlockSpec((1, d), lambda i, j: (0, 0)),
            pl.BlockSpec((d, tn), lambda i, j: (0, j)),
        ],
        out_specs=pl.BlockSpec((tm, tn), lambda i, j: (i, j)),
        out_shape=jax.ShapeDtypeStruct((t, n), F32),
        scratch_shapes=[pltpu.VMEM((tm, d), BF16)],
        compiler_params=_cparams(2),
        name="inproj",
    )(x, gain, w)


def _build_w_in(w):
    d = w.shape[0]

    def dup(seg):
        s = seg.reshape(d, SWA_HKV, SWA_HD)
        return jnp.concatenate([s, s], axis=-1).reshape(d, SWA_HKV * LANES)

    small = jnp.concatenate(
        [w[:, W_DNA:W_GLQ], w[:, W_GLLR:W_SWQ],
         jnp.zeros((d, LANES - 2 * DN_HEADS - GLA_RANK), w.dtype)], axis=1)
    parts = [w[:, W_DNQ:W_DNA], w[:, W_GLQ:W_GLLR], w[:, W_SWQ:W_SWK],
             dup(w[:, W_SWK:W_SWV]), dup(w[:, W_SWV:W_GATE]), w[:, W_GATE:W_END], small]
    used = U_SMALL + LANES
    parts.append(jnp.zeros((d, U_COLS - used), w.dtype))
    return _bf(jnp.concatenate(parts, axis=1))


def _tri_inv(a, sub, lane, eye):
    bd16 = (sub >> 4) == (lane >> 4)
    bd32 = (sub >> 5) == (lane >> 5)
    a16 = jnp.where(bd16, a, 0.0)
    x = eye - a16
    p = _mm(a16, a16)
    x = x + _mm(x, p)
    p = _mm(p, p)
    x = x + _mm(x, p)
    p = _mm(p, p)
    x = x + _mm(x, p)
    a32 = jnp.where(bd32 & jnp.logical_not(bd16), a, 0.0)
    x = x - _mm(x, _mm(a32, x))
    a64 = jnp.where(bd32, 0.0, a)
    x = x - _mm(x, _mm(a64, x))
    return x


def _dn_kernel(q_ref, k_ref, v_ref, z_ref, pq_ref, pk_ref, pv_ref, sm_ref,
               cq_ref, ck_ref, cv_ref, alog_ref, dtb_ref, on_ref, o_ref,
               xs_ref, q_s, k_s, v_s, g_s, b_s, o_s, s_ref, *, tb_rows):
    head = pl.program_id(0)
    tb = pl.program_id(1)

    @pl.when(tb == 0)
    def _():
        s_ref[...] = jnp.zeros_like(s_ref)

    keep = (tb > 0).astype(F32)

    def conv_silu(x_ref, p_ref, w_ref):
        xs_ref[0:SUBLANES, :] = p_ref[...] * keep
        xs_ref[SUBLANES:, :] = x_ref[...]
        w = w_ref[...]
        y = w[DN_CONV - 1:DN_CONV, :] * xs_ref[pl.ds(SUBLANES, tb_rows), :]
        for kk in range(DN_CONV - 1):
            off = SUBLANES - (DN_CONV - 1) + kk
            y = y + w[kk:kk + 1, :] * xs_ref[pl.ds(off, tb_rows), :]
        return _silu(y)

    qc = conv_silu(q_ref, pq_ref, cq_ref)
    q_s[...] = qc * lax.rsqrt(jnp.sum(qc * qc, axis=-1, keepdims=True) + EPS) * (DN_DK ** -0.5)
    kc = conv_silu(k_ref, pk_ref, ck_ref)
    k_s[...] = kc * lax.rsqrt(jnp.sum(kc * kc, axis=-1, keepdims=True) + EPS)
    v_s[...] = conv_silu(v_ref, pv_ref, cv_ref)
    sm = sm_ref[...]
    g_s[...] = -jnp.exp(alog_ref[...]) * jax.nn.softplus(sm + dtb_ref[...])
    b_s[...] = jax.nn.sigmoid(sm)

    sub, lane, same = _pair_masks()
    causal = same & (sub >= lane)
    strict = same & (sub > lane)
    eye = jnp.where(sub == lane, 1.0, 0.0).astype(F32)
    tri = jnp.where(causal, 1.0, 0.0).astype(BF16)
    first = sub[:, :1] < CHUNK

    def body(p, carry):
        r0 = pl.multiple_of(p * PAIR, PAIR)
        rows = pl.ds(r0, PAIR)
        q2 = q_s[rows, :]
        k2 = k_s[rows, :]
        v2 = v_s[rows, :]
        gall = _chunk_cumsum(tri, g_s[rows, :])
        g = jnp.sum(jnp.where(lane == head + SM_A, gall, 0.0), axis=1, keepdims=True)
        beta = jnp.sum(jnp.where(lane == head + SM_B, b_s[rows, :], 0.0), axis=1, keepdims=True)
        grow = jnp.sum(jnp.where(sub == head + SM_A, gall.T, 0.0), axis=0, keepdims=True)
        decay = jnp.where(causal, jnp.exp(jnp.where(causal, g - grow, 0.0)), 0.0)
        kb = k2 * beta
        a = jnp.where(strict, _mm_nt(kb, k2) * decay, 0.0)
        tm = _tri_inv(a, sub, lane, eye)
        eg = jnp.exp(g)
        sol = _mm(tm, jnp.concatenate([v2 * beta, kb * eg], axis=1))
        value = sol[:, :DN_DV]
        kcum = sol[:, DN_DV:]
        at = jnp.where(causal, _mm_nt(q2, k2) * decay, 0.0)
        g_end0 = g[CHUNK - 1:CHUNK, :]
        g_end1 = g[PAIR - 1:PAIR, :]
        kd = k2 * jnp.exp(jnp.where(first, g_end0, g_end1) - g)
        qe = q2 * eg - _mm(at, kcum)
        oloc = _mm(at, value)
        s = s_ref[...]
        for c, g_end in ((0, g_end0), (1, g_end1)):
            sl = slice(c * CHUNK, (c + 1) * CHUNK)
            kd_c = kd[sl]
            pm = -_mm_tn(kd_c, kcum[sl])
            qc_ = _mm_tn(kd_c, value[sl])
            o_s[pl.ds(r0 + c * CHUNK, CHUNK), :] = _mm(qe[sl], s) + oloc[sl]
            s = jnp.exp(g_end) * s + _mm(pm, s) + qc_
        s_ref[...] = s
        return carry

    lax.fori_loop(0, tb_rows // PAIR, body, 0)
    o = o_s[...]
    o_ref[...] = _bf(_rms_rows(o, on_ref[...]) * _silu(z_ref[...]))


def _deltanet(u, conv_w, alog_row, dtb_row, out_norm, *, tb_rows=1024):
    t = u.shape[0]
    nb = DN_QK // LANES
    prev_blk = tb_rows // SUBLANES

    def col(seg):
        return lambda h, i: (i, U_DN // LANES + seg * nb + h)

    def prev(seg):
        return lambda h, i: (jnp.maximum(i * prev_blk - 1, 0), U_DN // LANES + seg * nb + h)

    def cw(seg):
        return lambda h, i: (0, seg * nb + h)

    row = pl.BlockSpec((1, LANES), lambda h, i: (0, 0))
    blk = lambda seg: pl.BlockSpec((tb_rows, LANES), col(seg))
    pblk = lambda seg: pl.BlockSpec((SUBLANES, LANES), prev(seg))
    cblk = lambda seg: pl.BlockSpec((DN_CONV, LANES), cw(seg))
    return pl.pallas_call(
        functools.partial(_dn_kernel, tb_rows=tb_rows),
        grid=(DN_HEADS, t // tb_rows),
        in_specs=[blk(0), blk(1), blk(2), blk(3), pblk(0), pblk(1), pblk(2),
                  pl.BlockSpec((tb_rows, LANES), lambda h, i: (i, U_SMALL // LANES)),
                  cblk(0), cblk(1), cblk(2), row, row, row],
        out_specs=pl.BlockSpec((tb_rows, LANES), lambda h, i: (i, h)),
        out_shape=jax.ShapeDtypeStruct((t, DN_WIDTH), BF16),
        scratch_shapes=[pltpu.VMEM((tb_rows + SUBLANES, LANES), F32)]
        + [pltpu.VMEM((tb_rows, LANES), F32)] * 6
        + [pltpu.VMEM((DN_DK, DN_DV), F32)],
        compiler_params=_cparams(2),
        name="deltanet",
    )(u, u, u, u, u, u, u, u, conv_w, conv_w, conv_w, alog_row, dtb_row, out_norm)


def _gla_kernel(q_ref, k_ref, v_ref, r_ref, sm_ref, gu_ref, gb_ref, on_ref, o_ref,
                la_s, st_ref, *, tb_rows):
    @pl.when(pl.program_id(1) == 0)
    def _():
        st_ref[...] = jnp.zeros_like(st_ref)

    logits = jnp.dot(_bf(sm_ref[...]), gu_ref[...], preferred_element_type=F32) + gb_ref[...]
    la_s[...] = jax.nn.log_sigmoid(logits) * (1.0 / GLA_TAU)

    sub, lane, same = _pair_masks()
    causal = same & (sub >= lane)
    tri = jnp.where(causal, 1.0, 0.0).astype(BF16)
    first = sub < CHUNK
    gain = on_ref[...]

    def body(p, carry):
        r0 = pl.multiple_of(p * PAIR, PAIR)
        rows = pl.ds(r0, PAIR)
        g = _chunk_cumsum(tri, la_s[rows, :])
        q2 = q_ref[rows, :] * (GLA_DK ** -0.5)
        k2 = k_ref[rows, :]
        v2 = _bf(v_ref[rows, :])
        qg = _bf(q2 * jnp.exp(g))
        kg = _bf(k2 * jnp.exp(-g))
        a = jnp.where(causal, _mm_nt(qg, kg), 0.0)
        o_intra = _mm(a, v2)
        g_end0 = g[CHUNK - 1:CHUNK, :]
        g_end1 = g[PAIR - 1:PAIR, :]
        kd = _bf(k2 * jnp.exp(jnp.where(first, g_end0, g_end1) - g))
        st = st_ref[...]
        for c, g_end in ((0, g_end0), (1, g_end1)):
            sl = slice(c * CHUNK, (c + 1) * CHUNK)
            o_c = _mm_nt(qg[sl], st) + o_intra[sl]
            st = st * jnp.exp(g_end) + _mm_tn(v2[sl], kd[sl])
            orow = pl.ds(r0 + c * CHUNK, CHUNK)
            o_ref[orow, :] = _bf(_rms_rows(o_c, gain) * _silu(r_ref[orow, :]))
        st_ref[...] = st
        return carry

    lax.fori_loop(0, tb_rows // PAIR, body, 0)


def _gla(u, gate_up_pad, gate_bias, out_norm, *, tb_rows=1024):
    t = u.shape[0]
    return pl.pallas_call(
        functools.partial(_gla_kernel, tb_rows=tb_rows),
        grid=(GLA_HEADS, t // tb_rows),
        in_specs=[
            pl.BlockSpec((tb_rows, GLA_DK), lambda h, i: (i, U_GLQ // GLA_DK + h)),
            pl.BlockSpec((tb_rows, GLA_DK), lambda h, i: (i, U_GLK // GLA_DK + h)),
            pl.BlockSpec((tb_rows, GLA_DV), lambda h, i: (i, U_GLV // GLA_DV + h)),
            pl.BlockSpec((tb_rows, GLA_DV), lambda h, i: (i, U_GLR // GLA_DV + h)),
            pl.BlockSpec((tb_rows, LANES), lambda h, i: (i, U_SMALL // LANES)),
            pl.BlockSpec((LANES, GLA_DK), lambda h, i: (0, h)),
            pl.BlockSpec((1, GLA_DK), lambda h, i: (0, h)),
            pl.BlockSpec((1, GLA_DV), lambda h, i: (0, 0)),
        ],
        out_specs=pl.BlockSpec((tb_rows, GLA_DV), lambda h, i: (i, h)),
        out_shape=jax.ShapeDtypeStruct((t, GLA_WIDTH), BF16),
        scratch_shapes=[pltpu.VMEM((tb_rows, GLA_DK), F32), pltpu.VMEM((GLA_DV, GLA_DK), F32)],
        compiler_params=_cparams(2),
        name="gla",
    )(u, u, u, u, u, gate_up_pad, gate_bias, out_norm)


def _swa_kernel(q_ref, kc_ref, kp_ref, vc_ref, vp_ref, qg_ref, kg_ref, sink_ref, o_ref):
    n = pl.program_id(0)
    w = SWA_WINDOW
    lane = lax.broadcasted_iota(jnp.int32, (w, LANES), 1)
    lo = lane < SWA_HD
    lane2 = lax.broadcasted_iota(jnp.int32, (2 * w, LANES), 1)
    lo2 = lane2 < SWA_HD
    qi = lax.broadcasted_iota(jnp.int32, (w, 2 * w), 0)
    ki = lax.broadcasted_iota(jnp.int32, (w, 2 * w), 1)
    mask = (ki <= qi + w) & (ki > qi + w - SWA_WINDOW) & ((n > 0) | (ki >= w))
    kk = jnp.concatenate([kp_ref[...], kc_ref[...]], axis=0)
    vv = jnp.concatenate([vp_ref[...], vc_ref[...]], axis=0)
    qgain = qg_ref[...]
    kgain = kg_ref[...]
    for grp in range(SWA_HKV):
        cols = slice(grp * LANES, (grp + 1) * LANES)
        kg = kk[:, cols]
        kn = _bf(kg * lax.rsqrt(jnp.sum(kg * kg, axis=-1, keepdims=True) * (1.0 / LANES) + EPS) * kgain)
        vg = vv[:, cols]
        v_half = (_bf(jnp.where(lo2, vg, 0.0)), _bf(jnp.where(lo2, 0.0, vg)))
        for pr in range(SWA_HQ // SWA_HKV // 2):
            pair = grp * (SWA_HQ // SWA_HKV // 2) + pr
            pcols = slice(pair * LANES, (pair + 1) * LANES)
            qp = q_ref[:, pcols]
            sq = qp * qp
            ms_lo = jnp.sum(jnp.where(lo, sq, 0.0), axis=-1, keepdims=True) * (1.0 / SWA_HD)
            ms_hi = jnp.sum(jnp.where(lo, 0.0, sq), axis=-1, keepdims=True) * (1.0 / SWA_HD)
            qn = qp * jnp.where(lo, lax.rsqrt(ms_lo + EPS), lax.rsqrt(ms_hi + EPS)) * qgain
            acc = None
            for e in range(2):
                sink = sink_ref[pair * 2 + e]
                qm = jnp.where(lo, qn, 0.0) if e == 0 else jnp.where(lo, 0.0, qn)
                s = _mm_nt(qm, kn) * (SWA_HD ** -0.5)
                s = jnp.where(mask, s, -jnp.inf)
                m = jnp.maximum(jnp.max(s, axis=-1, keepdims=True), sink)
                p = jnp.exp(s - m)
                p = p / (jnp.sum(p, axis=-1, keepdims=True) + jnp.exp(sink - m))
                part = jnp.dot(_bf(p), v_half[e], preferred_element_type=F32)
                acc = part if acc is None else acc + part
            o_ref[:, pcols] = _bf(acc)


def _swa(u, qgain, kgain, sinks):
    t = u.shape[0]
    w = SWA_WINDOW
    kvw = SWA_HKV * LANES
    cur = lambda c: (lambda n: (n, c))
    prv = lambda c: (lambda n: (jnp.maximum(n - 1, 0), c))
    row = pl.BlockSpec((1, LANES), lambda n: (0, 0))
    return pl.pallas_call(
        _swa_kernel,
        grid=(t // w,),
        in_specs=[
            pl.BlockSpec((w, SWA_WIDTH), cur(U_SWQ // SWA_WIDTH)),
            pl.BlockSpec((w, kvw), cur(U_SWK // kvw)),
            pl.BlockSpec((w, kvw), prv(U_SWK // kvw)),
            pl.BlockSpec((w, kvw), cur(U_SWV // kvw)),
            pl.BlockSpec((w, kvw), prv(U_SWV // kvw)),
            row, row,
            pl.BlockSpec(memory_space=pltpu.SMEM),
        ],
        out_specs=pl.BlockSpec((w, SWA_WIDTH), lambda n: (n, 0)),
        out_shape=jax.ShapeDtypeStruct((t, SWA_WIDTH), BF16),
        compiler_params=_cparams(1),
        name="swa",
    )(u, u, u, u, u, qgain, kgain, sinks)


def _merge_kernel(x_ref, od_ref, og_ref, os_ref, g0_ref, g1_ref, g2_ref,
                  wd_ref, wg_ref, ws_ref, wo_ref, o_ref):
    @pl.when(pl.program_id(1) == 0)
    def _():
        o_ref[...] = x_ref[...]

    dot = functools.partial(jnp.dot, preferred_element_type=F32)
    y = (jax.nn.sigmoid(g0_ref[...]) * dot(od_ref[...], wd_ref[...])
         + jax.nn.sigmoid(g1_ref[...]) * dot(og_ref[...], wg_ref[...])
         + jax.nn.sigmoid(g2_ref[...]) * dot(os_ref[...], ws_ref[...]))
    o_ref[...] += dot(_bf(y), wo_ref[...])


def _merge(x, o_dn, o_gla, o_swa, u, w_dn, w_gla, w_swa, w_o, *, tm=512, tj=512):
    t, d = x.shape
    nj = d // tj
    gate = lambda b: (lambda i, j: (i, U_GATE // tj + b * nj + j))
    act = lambda width: pl.BlockSpec((tm, width), lambda i, j: (i, 0))
    wcol = lambda width: pl.BlockSpec((width, tj), lambda i, j: (0, j))
    return pl.pallas_call(
        _merge_kernel,
        grid=(t // tm, nj),
        in_specs=[
            pl.BlockSpec((tm, d), lambda i, j: (i, 0), pipeline_mode=pl.Buffered(1)),
            act(DN_WIDTH), act(GLA_WIDTH), act(SWA_WIDTH),
            pl.BlockSpec((tm, tj), gate(0)), pl.BlockSpec((tm, tj), gate(1)),
            pl.BlockSpec((tm, tj), gate(2)),
            wcol(DN_WIDTH), wcol(GLA_WIDTH), wcol(SWA_WIDTH),
            pl.BlockSpec((tj, d), lambda i, j: (j, 0)),
        ],
        out_specs=pl.BlockSpec((tm, d), lambda i, j: (i, 0)),
        out_shape=jax.ShapeDtypeStruct((t, d), F32),
        compiler_params=_cparams(2),
        name="merge",
    )(x, o_dn, o_gla, o_swa, u, u, u, w_dn, w_gla, w_swa, w_o)


def _lane_row(vec):
    return jnp.pad(vec.astype(F32), (0, LANES - vec.shape[0]))[None, :]


def kernel(x, ffn1_norm, ffn1_w1, ffn1_w3, ffn1_w2, mix_norm, w_in, dn_conv, dn_a_log, dn_dt_bias, dn_out_norm, gla_gate_up, gla_gate_bias, gla_out_norm, swa_q_norm, swa_k_norm, swa_sinks, w_branch_dn, w_branch_gla, w_branch_swa, w_out, ffn2_norm, ffn2_w1, ffn2_w3, ffn2_w2):
    assert x.shape[0] == 1 and x.shape[2] == D_MODEL
    xs = x[0]
    for l in range(DEPTH):
        xs = _ffn(xs, ffn1_norm[l][None], _bf(ffn1_w1[l]), _bf(ffn1_w3[l]), _bf(ffn1_w2[l]))
        u = _inproj(xs, mix_norm[l][None], _build_w_in(w_in[l]))
        o_dn = _deltanet(u, dn_conv[l], _lane_row(dn_a_log[l]), _lane_row(dn_dt_bias[l]),
                         dn_out_norm[l][None])
        gate_up_pad = jnp.zeros((LANES, GLA_QK), BF16).at[SM_LR:SM_LR + GLA_RANK].set(
            _bf(gla_gate_up[l]))
        o_gla = _gla(u, gate_up_pad, gla_gate_bias[l][None], gla_out_norm[l][None])
        o_swa = _swa(u, jnp.tile(swa_q_norm[l], 2)[None], jnp.tile(swa_k_norm[l], 2)[None],
                     swa_sinks[l])
        xs = _merge(xs, o_dn, o_gla, o_swa, u, _bf(w_branch_dn[l]), _bf(w_branch_gla[l]),
                    _bf(w_branch_swa[l]), _bf(w_out[l]))
        xs = _ffn(xs, ffn2_norm[l][None], _bf(ffn2_w1[l]), _bf(ffn2_w3[l]), _bf(ffn2_w2[l]))
    return xs[None]
```

```python
import functools

import jax
import jax.numpy as jnp
from jax import lax
from jax.experimental import pallas as pl
from jax.experimental.pallas import tpu as pltpu

F32 = jnp.float32
BF16 = jnp.bfloat16

D_MODEL = 2048
DEPTH = 2
EPS = 1e-6
D_FF = 5632
FFN_RES_SCALE = 0.5
DN_HEADS, DN_DK, DN_DV, DN_CONV = 8, 128, 128, 4
GLA_HEADS, GLA_DK, GLA_DV, GLA_RANK, GLA_TAU = 4, 128, 256, 16, 16.0
SWA_HQ, SWA_HKV, SWA_HD, SWA_WINDOW = 16, 4, 64, 128
N_BRANCH = 3
CHUNK = 64
DN_QK = DN_HEADS * DN_DK
DN_WIDTH = DN_HEADS * DN_DV
GLA_QK = GLA_HEADS * GLA_DK
GLA_WIDTH = GLA_HEADS * GLA_DV
SWA_WIDTH = SWA_HQ * SWA_HD
SWA_KV = SWA_HKV * SWA_HD

_SPLITS = (DN_QK, DN_QK, DN_WIDTH, DN_WIDTH, DN_HEADS, DN_HEADS,
           GLA_QK, GLA_QK, GLA_WIDTH, GLA_WIDTH, GLA_RANK,
           SWA_WIDTH, SWA_KV, SWA_KV, N_BRANCH * D_MODEL)
_OFFS = tuple(sum(_SPLITS[:i]) for i in range(len(_SPLITS) + 1))
(W_DNQ, W_DNK, W_DNV, W_DNZ, W_DNA, W_DNB, W_GLQ, W_GLK, W_GLV, W_GLR, W_GLLR,
 W_SWQ, W_SWK, W_SWV, W_GATE, W_END) = _OFFS

LANES = 128
SUBLANES = 8
PAIR = 2 * CHUNK

U_DN = 0
U_GLQ = U_DN + 4 * DN_QK
U_GLK = U_GLQ + GLA_QK
U_GLV = U_GLK + GLA_QK
U_GLR = U_GLV + GLA_WIDTH
U_SWQ = U_GLR + GLA_WIDTH
U_SWK = U_SWQ + SWA_WIDTH
U_SWV = U_SWK + SWA_HKV * LANES
U_GATE = U_SWV + SWA_HKV * LANES
U_SMALL = U_GATE + N_BRANCH * D_MODEL
U_COLS = 16384
SM_A, SM_B, SM_LR = 0, DN_HEADS, 2 * DN_HEADS

VMEM_LIMIT = 60 * 1024 * 1024


def _cparams(n_axes):
    return pltpu.CompilerParams(dimension_semantics=("arbitrary",) * n_axes,
                                vmem_limit_bytes=VMEM_LIMIT)


def _bf(t):
    return t.astype(BF16)


def _mm(a, b):
    return jnp.dot(_bf(a), _bf(b), preferred_element_type=F32)


def _mm_nt(a, b):
    return lax.dot_general(_bf(a), _bf(b), (((1,), (1,)), ((), ())), preferred_element_type=F32)


def _mm_tn(a, b):
    return lax.dot_general(_bf(a), _bf(b), (((0,), (0,)), ((), ())), preferred_element_type=F32)


def _silu(t):
    return t * jax.nn.sigmoid(t)


def _rms_rows(x, gain):
    ms = jnp.mean(x * x, axis=-1, keepdims=True)
    return x * lax.rsqrt(ms + EPS) * gain


def _each(fn, *lists):
    return [fn(*args) for args in zip(*lists)]


def _chunk_cumsum(tri, tiles):
    hi = _each(_bf, tiles)
    r1 = _each(lambda t, h: t - h.astype(F32), tiles, hi)
    mid = _each(_bf, r1)
    lo = _each(lambda r, m: _bf(r - m.astype(F32)), r1, mid)
    dot = functools.partial(jnp.dot, tri, preferred_element_type=F32)
    return _each(lambda h, m, l: dot(h) + dot(m) + dot(l), hi, mid, lo)


def _pair_masks():
    sub = lax.broadcasted_iota(jnp.int32, (PAIR, PAIR), 0)
    lane = lax.broadcasted_iota(jnp.int32, (PAIR, PAIR), 1)
    same = (sub >= CHUNK) == (lane >= CHUNK)
    return sub, lane, same


def _ffn_kernel(x_ref, g_ref, w1_ref, w3_ref, w2_ref, o_ref, h_ref):
    @pl.when(pl.program_id(1) == 0)
    def _():
        x = x_ref[...]
        h_ref[...] = _bf(_rms_rows(x, g_ref[...]))
        o_ref[...] = x

    h = h_ref[...]
    a = jnp.dot(h, _bf(w1_ref[...]), preferred_element_type=F32)
    b = jnp.dot(h, _bf(w3_ref[...]), preferred_element_type=F32)
    act = _silu(a) * (b * FFN_RES_SCALE)
    o_ref[...] += jnp.dot(_bf(act), _bf(w2_ref[...]), preferred_element_type=F32)


def _ffn(x, gain, w1, w3, w2, *, tm=1024, tf=512):
    t, d = x.shape
    f = w1.shape[1]
    return pl.pallas_call(
        _ffn_kernel,
        grid=(t // tm, f // tf),
        in_specs=[
            pl.BlockSpec((tm, d), lambda i, j: (i, 0), pipeline_mode=pl.Buffered(1)),
            pl.BlockSpec((1, d), lambda i, j: (0, 0)),
            pl.BlockSpec((d, tf), lambda i, j: (0, j)),
            pl.BlockSpec((d, tf), lambda i, j: (0, j)),
            pl.BlockSpec((tf, d), lambda i, j: (j, 0)),
        ],
        out_specs=pl.BlockSpec((tm, d), lambda i, j: (i, 0), pipeline_mode=pl.Buffered(1)),
        out_shape=jax.ShapeDtypeStruct((t, d), F32),
        scratch_shapes=[pltpu.VMEM((tm, d), BF16)],
        compiler_params=_cparams(2),
        name="ffn",
    )(x, gain, w1, w3, w2)


def _inproj_kernel(x_ref, g_ref, w_ref, u_ref, h_ref):
    @pl.when(pl.program_id(1) == 0)
    def _():
        h_ref[...] = _bf(_rms_rows(x_ref[...], g_ref[...]))

    u_ref[...] = jnp.dot(h_ref[...], w_ref[...], preferred_element_type=F32)


def _inproj(x, gain, w, *, tm=1024, tn=2048):
    t, d = x.shape
    n = w.shape[1]
    return pl.pallas_call(
        _inproj_kernel,
        grid=(t // tm, n // tn),
        in_specs=[
            pl.BlockSpec((tm, d), lambda i, j: (i, 0), pipeline_mode=pl.Buffered(1)),
            pl.BlockSpec((1, d), lambda i, j: (0, 0)),
            pl.BlockSpec((d, tn), lambda i, j: (0, j)),
        ],
        out_specs=pl.BlockSpec((tm, tn), lambda i, j: (i, j)),
        out_shape=jax.ShapeDtypeStruct((t, n), F32),
        scratch_shapes=[pltpu.VMEM((tm, d), BF16)],
        compiler_params=_cparams(2),
        name="inproj",
    )(x, gain, w)


def _build_w_in(w):
    d = w.shape[0]

    def dup(seg):
        s = seg.reshape(d, SWA_HKV, SWA_HD)
        return jnp.concatenate([s, s], axis=-1).reshape(d, SWA_HKV * LANES)

    w = _bf(w)
    small = jnp.concatenate(
        [w[:, W_DNA:W_GLQ], w[:, W_GLLR:W_SWQ],
         jnp.zeros((d, LANES - 2 * DN_HEADS - GLA_RANK), w.dtype)], axis=1)
    parts = [w[:, W_DNQ:W_DNA], w[:, W_GLQ:W_GLLR], w[:, W_SWQ:W_SWK],
             dup(w[:, W_SWK:W_SWV]), dup(w[:, W_SWV:W_GATE]), w[:, W_GATE:W_END], small]
    used = U_SMALL + LANES
    parts.append(jnp.zeros((d, U_COLS - used), w.dtype))
    return jnp.concatenate(parts, axis=1)


def _tri_inv(a, sub, lane, eye):
    bd16 = (sub >> 4) == (lane >> 4)
    bd32 = (sub >> 5) == (lane >> 5)
    off32 = bd32 & jnp.logical_not(bd16)
    a16 = _each(lambda t: _bf(jnp.where(bd16, t, 0.0)), a)
    x = _each(lambda t: eye - t.astype(F32), a16)
    p = a16
    for _ in range(3):
        p = _each(lambda t: _bf(_mm(t, t)), p)
        x = _each(lambda xt, pt: xt + _mm(xt, pt), x, p)
    for blk in (lambda t: jnp.where(off32, t, 0.0), lambda t: jnp.where(bd32, 0.0, t)):
        xb = _each(_bf, x)
        y = _each(lambda at, xt: _mm(blk(at), xt), a, xb)
        x = _each(lambda xt, xbt, yt: xt - _mm(xbt, yt), x, xb, y)
    return x


def _dn_kernel(q_ref, k_ref, v_ref, z_ref, pq_ref, pk_ref, pv_ref, sm_ref,
               cq_ref, ck_ref, cv_ref, alog_ref, dtb_ref, on_ref, o_ref,
               xs_ref, gc_s, gt_s, b_s, qe_s, ol_s, pm_s, qc_s, gl_s, o_s, s_ref, *, tb_rows):
    tb = pl.program_id(0)
    head = pl.program_id(1)
    n_pairs = tb_rows // PAIR

    @pl.when(tb == 0)
    def _():
        s_ref[head] = jnp.zeros((DN_DK, DN_DV), F32)

    sub, lane, same = _pair_masks()
    causal = same & (sub >= lane)
    strict = same & (sub > lane)
    eye = jnp.where(sub == lane, 1.0, 0.0).astype(F32)
    tri = jnp.where(causal, 1.0, 0.0).astype(BF16)
    first = sub[:, :1] < CHUNK
    rows = [slice(p * PAIR, (p + 1) * PAIR) for p in range(n_pairs)]
    halves = [slice(c * CHUNK, (c + 1) * CHUNK) for c in range(2)]

    @pl.when(head == 0)
    def _():
        sm = sm_ref[...]
        b_s[...] = jax.nn.sigmoid(sm)
        gs = -jnp.exp(alog_ref[...]) * jax.nn.softplus(sm + dtb_ref[...])
        gall = _chunk_cumsum(tri, [gs[r] for r in rows])
        for r, gt in zip(rows, gall):
            gc_s[r, :] = gt
            gt_s[r, :] = gt.T

    keep = (tb > 0).astype(F32)

    def conv_silu(x_ref, p_ref, w_ref):
        xs_ref[0:SUBLANES, :] = p_ref[...] * keep
        xs_ref[SUBLANES:, :] = x_ref[...]
        w = w_ref[...]
        y = w[DN_CONV - 1:DN_CONV, :] * xs_ref[pl.ds(SUBLANES, tb_rows), :]
        for kk in range(DN_CONV - 1):
            off = SUBLANES - (DN_CONV - 1) + kk
            y = y + w[kk:kk + 1, :] * xs_ref[pl.ds(off, tb_rows), :]
        return _silu(y)

    qc = conv_silu(q_ref, pq_ref, cq_ref)
    qn = qc * lax.rsqrt(jnp.sum(qc * qc, axis=-1, keepdims=True) + EPS) * (DN_DK ** -0.5)
    kc = conv_silu(k_ref, pk_ref, ck_ref)
    kn = kc * lax.rsqrt(jnp.sum(kc * kc, axis=-1, keepdims=True) + EPS)
    vc = conv_silu(v_ref, pv_ref, cv_ref)

    q2 = [qn[r] for r in rows]
    k2 = [kn[r] for r in rows]
    g = [jnp.sum(jnp.where(lane == head + SM_A, gc_s[r, :], 0.0), axis=1, keepdims=True)
         for r in rows]
    beta = [jnp.sum(jnp.where(lane == head + SM_B, b_s[r, :], 0.0), axis=1, keepdims=True)
            for r in rows]
    grow = [jnp.sum(jnp.where(sub == head + SM_A, gt_s[r, :], 0.0), axis=0, keepdims=True)
            for r in rows]
    decay = _each(lambda gc, gr: jnp.where(causal, jnp.exp(jnp.where(causal, gc - gr, 0.0)), 0.0),
                  g, grow)
    eg = _each(jnp.exp, g)
    kb = _each(lambda kt, bt: kt * bt, k2, beta)
    k2b = _each(_bf, k2)
    a = _each(lambda kbt, kt, dt: jnp.where(strict, _mm_nt(kbt, kt) * dt, 0.0), kb, k2b, decay)
    at = _each(lambda qt, kt, dt: _bf(jnp.where(causal, _mm_nt(qt, kt) * dt, 0.0)), q2, k2b, decay)
    tm = _tri_inv(a, sub, lane, eye)
    rhs = [jnp.concatenate([vc[r] * bt, kbt * egt], axis=1)
           for r, bt, kbt, egt in zip(rows, beta, kb, eg)]
    sol = _each(_mm, tm, rhs)
    value = [_bf(t[:, :DN_DV]) for t in sol]
    kcum = [_bf(t[:, DN_DV:]) for t in sol]
    g_end = [[gt[CHUNK - 1:CHUNK, :], gt[PAIR - 1:PAIR, :]] for gt in g]
    kd = _each(lambda kt, gt, ge: _bf(kt * jnp.exp(jnp.where(first, ge[0], ge[1]) - gt)),
               k2, g, g_end)
    qe = _each(lambda qt, egt, att, kct: _bf(qt * egt - _mm(att, kct)), q2, eg, at, kcum)
    ol = _each(_mm, at, value)
    for p, r in enumerate(rows):
        qe_s[r, :] = qe[p]
        ol_s[r, :] = ol[p]
    for p in range(n_pairs):
        for c, sl in enumerate(halves):
            pm_s[2 * p + c] = _bf(-_mm_tn(kd[p][sl], kcum[p][sl]))
    for p in range(n_pairs):
        for c, sl in enumerate(halves):
            ci = 2 * p + c
            qc_s[ci] = _mm_tn(kd[p][sl], value[p][sl])
            gl_s[ci:ci + 1, :] = jnp.broadcast_to(jnp.exp(g_end[p][c]), (1, LANES))

    def step(c, s):
        rows = pl.ds(pl.multiple_of(c * CHUNK, CHUNK), CHUNK)
        sb = _bf(s)
        o_s[rows, :] = jnp.dot(qe_s[rows, :], sb, preferred_element_type=F32) + ol_s[rows, :]
        return (gl_s[pl.ds(c, 1), :] * s
                + jnp.dot(pm_s[c], sb, preferred_element_type=F32) + qc_s[c])

    s_ref[head] = lax.fori_loop(0, 2 * n_pairs, step, s_ref[head], unroll=2)
    o_ref[...] = _bf(_rms_rows(o_s[...], on_ref[...]) * _silu(z_ref[...]))


def _deltanet(u, conv_w, alog_row, dtb_row, out_norm, *, tb_rows=1024):
    t = u.shape[0]
    nb = DN_QK // LANES
    prev_blk = tb_rows // SUBLANES

    def col(seg):
        return lambda i, h: (i, U_DN // LANES + seg * nb + h)

    def prev(seg):
        return lambda i, h: (jnp.maximum(i * prev_blk - 1, 0), U_DN // LANES + seg * nb + h)

    def cw(seg):
        return lambda i, h: (0, seg * nb + h)

    row = pl.BlockSpec((1, LANES), lambda i, h: (0, 0))
    blk = lambda seg: pl.BlockSpec((tb_rows, LANES), col(seg))
    pblk = lambda seg: pl.BlockSpec((SUBLANES, LANES), prev(seg))
    cblk = lambda seg: pl.BlockSpec((DN_CONV, LANES), cw(seg))
    n_chunks = tb_rows // CHUNK
    rows_f32 = pltpu.VMEM((tb_rows, LANES), F32)
    return pl.pallas_call(
        functools.partial(_dn_kernel, tb_rows=tb_rows),
        grid=(t // tb_rows, DN_HEADS),
        in_specs=[blk(0), blk(1), blk(2), blk(3), pblk(0), pblk(1), pblk(2),
                  pl.BlockSpec((tb_rows, LANES), lambda i, h: (i, U_SMALL // LANES)),
                  cblk(0), cblk(1), cblk(2), row, row, row],
        out_specs=pl.BlockSpec((tb_rows, LANES), lambda i, h: (i, h)),
        out_shape=jax.ShapeDtypeStruct((t, DN_WIDTH), BF16),
        scratch_shapes=[
            pltpu.VMEM((tb_rows + SUBLANES, LANES), F32),
            rows_f32, rows_f32, rows_f32,
            pltpu.VMEM((tb_rows, DN_DK), BF16),
            rows_f32,
            pltpu.VMEM((n_chunks, DN_DK, DN_DK), BF16),
            pltpu.VMEM((n_chunks, DN_DK, DN_DV), F32),
            pltpu.VMEM((n_chunks, LANES), F32),
            rows_f32,
            pltpu.VMEM((DN_HEADS, DN_DK, DN_DV), F32),
        ],
        compiler_params=_cparams(2),
        name="deltanet",
    )(u, u, u, u, u, u, u, u, conv_w, conv_w, conv_w, alog_row, dtb_row, out_norm)


def _gla_kernel(q_ref, k_ref, v_ref, r_ref, sm_ref, gu_ref, gb_ref, on_ref, o_ref,
                qg_s, oi_s, kv_s, gl_s, st_ref, *, tb_rows):
    n_pairs = tb_rows // PAIR

    @pl.when(pl.program_id(1) == 0)
    def _():
        st_ref[...] = jnp.zeros_like(st_ref)

    logits = jnp.dot(_bf(sm_ref[...]), gu_ref[...], preferred_element_type=F32) + gb_ref[...]
    la = jax.nn.log_sigmoid(logits) * (1.0 / GLA_TAU)

    sub, lane, same = _pair_masks()
    causal = same & (sub >= lane)
    tri = jnp.where(causal, 1.0, 0.0).astype(BF16)
    first = sub < CHUNK
    gain = on_ref[...]

    rows = [slice(p * PAIR, (p + 1) * PAIR) for p in range(n_pairs)]
    halves = [slice(c * CHUNK, (c + 1) * CHUNK) for c in range(2)]
    g = _chunk_cumsum(tri, [la[r] for r in rows])
    k2 = [k_ref[r, :] for r in rows]
    v2 = [_bf(v_ref[r, :]) for r in rows]
    qg = [_bf(q_ref[r, :] * (GLA_DK ** -0.5) * jnp.exp(gt)) for r, gt in zip(rows, g)]
    kg = _each(lambda kt, gt: _bf(kt * jnp.exp(-gt)), k2, g)
    a = _each(lambda qt, kt: _bf(jnp.where(causal, _mm_nt(qt, kt), 0.0)), qg, kg)
    oi = _each(_mm, a, v2)
    g_end = [[gt[CHUNK - 1:CHUNK, :], gt[PAIR - 1:PAIR, :]] for gt in g]
    kd = _each(lambda kt, gt, ge: _bf(kt * jnp.exp(jnp.where(first, ge[0], ge[1]) - gt)),
               k2, g, g_end)
    for p, r in enumerate(rows):
        qg_s[r, :] = qg[p]
        oi_s[r, :] = oi[p]
        for c, sl in enumerate(halves):
            ci = 2 * p + c
            kv_s[ci] = _mm_tn(v2[p][sl], kd[p][sl])
            gl_s[ci:ci + 1, :] = jnp.exp(g_end[p][c])

    def step(c, st):
        rws = pl.ds(pl.multiple_of(c * CHUNK, CHUNK), CHUNK)
        o_c = _mm_nt(qg_s[rws, :], st) + oi_s[rws, :]
        o_ref[rws, :] = _bf(_rms_rows(o_c, gain) * _silu(r_ref[rws, :]))
        return st * gl_s[pl.ds(c, 1), :] + kv_s[c]

    st_ref[...] = lax.fori_loop(0, 2 * n_pairs, step, st_ref[...], unroll=4)


def _gla(u, gate_up_pad, gate_bias, out_norm, *, tb_rows=1024):
    t = u.shape[0]
    return pl.pallas_call(
        functools.partial(_gla_kernel, tb_rows=tb_rows),
        grid=(GLA_HEADS, t // tb_rows),
        in_specs=[
            pl.BlockSpec((tb_rows, GLA_DK), lambda h, i: (i, U_GLQ // GLA_DK + h)),
            pl.BlockSpec((tb_rows, GLA_DK), lambda h, i: (i, U_GLK // GLA_DK + h)),
            pl.BlockSpec((tb_rows, GLA_DV), lambda h, i: (i, U_GLV // GLA_DV + h)),
            pl.BlockSpec((tb_rows, GLA_DV), lambda h, i: (i, U_GLR // GLA_DV + h)),
            pl.BlockSpec((tb_rows, LANES), lambda h, i: (i, U_SMALL // LANES)),
            pl.BlockSpec((LANES, GLA_DK), lambda h, i: (0, h)),
            pl.BlockSpec((1, GLA_DK), lambda h, i: (0, h)),
            pl.BlockSpec((1, GLA_DV), lambda h, i: (0, 0)),
        ],
        out_specs=pl.BlockSpec((tb_rows, GLA_DV), lambda h, i: (i, h)),
        out_shape=jax.ShapeDtypeStruct((t, GLA_WIDTH), BF16),
        scratch_shapes=[
            pltpu.VMEM((tb_rows, GLA_DK), BF16),
            pltpu.VMEM((tb_rows, GLA_DV), F32),
            pltpu.VMEM((tb_rows // CHUNK, GLA_DV, GLA_DK), F32),
            pltpu.VMEM((tb_rows // CHUNK, GLA_DK), F32),
            pltpu.VMEM((GLA_DV, GLA_DK), F32),
        ],
        compiler_params=_cparams(2),
        name="gla",
    )(u, u, u, u, u, gate_up_pad, gate_bias, out_norm)


def _swa_kernel(q_ref, kc_ref, kp_ref, vc_ref, vp_ref, qg_ref, kg_ref, sink_ref, o_ref):
    n = pl.program_id(0)
    w = SWA_WINDOW
    lane = lax.broadcasted_iota(jnp.int32, (w, LANES), 1)
    lo = lane < SWA_HD
    lane2 = lax.broadcasted_iota(jnp.int32, (2 * w, LANES), 1)
    lo2 = lane2 < SWA_HD
    qi = lax.broadcasted_iota(jnp.int32, (w, 2 * w), 0)
    ki = lax.broadcasted_iota(jnp.int32, (w, 2 * w), 1)
    mask = (ki <= qi + w) & (ki > qi + w - SWA_WINDOW) & ((n > 0) | (ki >= w))
    kk = jnp.concatenate([kp_ref[...], kc_ref[...]], axis=0)
    vv = jnp.concatenate([vp_ref[...], vc_ref[...]], axis=0)
    qgain = qg_ref[...]
    kgain = kg_ref[...]
    heads_per_grp = SWA_HQ // SWA_HKV
    gcols = [slice(g * LANES, (g + 1) * LANES) for g in range(SWA_HKV)]
    pcols = [slice(p * LANES, (p + 1) * LANES) for p in range(SWA_HQ // 2)]

    def norm_k(kg):
        ms = jnp.sum(kg * kg, axis=-1, keepdims=True) * (1.0 / LANES)
        return _bf(kg * lax.rsqrt(ms + EPS) * kgain)

    def norm_q(qp):
        sq = qp * qp
        ms_lo = jnp.sum(jnp.where(lo, sq, 0.0), axis=-1, keepdims=True) * (1.0 / SWA_HD)
        ms_hi = jnp.sum(jnp.where(lo, 0.0, sq), axis=-1, keepdims=True) * (1.0 / SWA_HD)
        return qp * jnp.where(lo, lax.rsqrt(ms_lo + EPS), lax.rsqrt(ms_hi + EPS)) * qgain

    def probs(s, sink):
        s = jnp.where(mask, s * (SWA_HD ** -0.5), -jnp.inf)
        m = jnp.maximum(jnp.max(s, axis=-1, keepdims=True), sink)
        p = jnp.exp(s - m)
        return _bf(p / (jnp.sum(p, axis=-1, keepdims=True) + jnp.exp(sink - m)))

    kn = [norm_k(kk[:, c]) for c in gcols]
    v_half = [(_bf(jnp.where(lo2, vv[:, c], 0.0)), _bf(jnp.where(lo2, 0.0, vv[:, c])))
              for c in gcols]
    qn = [norm_q(q_ref[:, c]) for c in pcols]
    heads = range(SWA_HQ)
    qm = [_bf(jnp.where(lo, qn[h // 2], 0.0) if h % 2 == 0 else jnp.where(lo, 0.0, qn[h // 2]))
          for h in heads]
    s = [_mm_nt(qm[h], kn[h // heads_per_grp]) for h in heads]
    p = [probs(s[h], sink_ref[h]) for h in heads]
    part = [jnp.dot(p[h], v_half[h // heads_per_grp][h % 2], preferred_element_type=F32)
            for h in heads]
    for pr, c in enumerate(pcols):
        o_ref[:, c] = _bf(part[2 * pr] + part[2 * pr + 1])


def _swa(u, qgain, kgain, sinks):
    t = u.shape[0]
    w = SWA_WINDOW
    kvw = SWA_HKV * LANES
    cur = lambda c: (lambda n: (n, c))
    prv = lambda c: (lambda n: (jnp.maximum(n - 1, 0), c))
    row = pl.BlockSpec((1, LANES), lambda n: (0, 0))
    return pl.pallas_call(
        _swa_kernel,
        grid=(t // w,),
        in_specs=[
            pl.BlockSpec((w, SWA_WIDTH), cur(U_SWQ // SWA_WIDTH)),
            pl.BlockSpec((w, kvw), cur(U_SWK // kvw)),
            pl.BlockSpec((w, kvw), prv(U_SWK // kvw)),
            pl.BlockSpec((w, kvw), cur(U_SWV // kvw)),
            pl.BlockSpec((w, kvw), prv(U_SWV // kvw)),
            row, row,
            pl.BlockSpec(memory_space=pltpu.SMEM),
        ],
        out_specs=pl.BlockSpec((w, SWA_WIDTH), lambda n: (n, 0)),
        out_shape=jax.ShapeDtypeStruct((t, SWA_WIDTH), BF16),
        compiler_params=_cparams(1),
        name="swa",
    )(u, u, u, u, u, qgain, kgain, sinks)


def _merge_kernel(x_ref, od_ref, og_ref, os_ref, g0_ref, g1_ref, g2_ref,
                  wd_ref, wg_ref, ws_ref, wo_ref, o_ref):
    @pl.when(pl.program_id(1) == 0)
    def _():
        o_ref[...] = x_ref[...]

    dot = functools.partial(jnp.dot, preferred_element_type=F32)
    y = (jax.nn.sigmoid(g0_ref[...]) * dot(od_ref[...], wd_ref[...])
         + jax.nn.sigmoid(g1_ref[...]) * dot(og_ref[...], wg_ref[...])
         + jax.nn.sigmoid(g2_ref[...]) * dot(os_ref[...], ws_ref[...]))
    o_ref[...] += dot(_bf(y), wo_ref[...])


def _merge(x, o_dn, o_gla, o_swa, u, w_dn, w_gla, w_swa, w_o, *, tm=512, tj=512):
    t, d = x.shape
    nj = d // tj
    gate = lambda b: (lambda i, j: (i, U_GATE // tj + b * nj + j))
    act = lambda width: pl.BlockSpec((tm, width), lambda i, j: (i, 0))
    wcol = lambda width: pl.BlockSpec((width, tj), lambda i, j: (0, j))
    return pl.pallas_call(
        _merge_kernel,
        grid=(t // tm, nj),
        in_specs=[
            pl.BlockSpec((tm, d), lambda i, j: (i, 0), pipeline_mode=pl.Buffered(1)),
            act(DN_WIDTH), act(GLA_WIDTH), act(SWA_WIDTH),
            pl.BlockSpec((tm, tj), gate(0)), pl.BlockSpec((tm, tj), gate(1)),
            pl.BlockSpec((tm, tj), gate(2)),
            wcol(DN_WIDTH), wcol(GLA_WIDTH), wcol(SWA_WIDTH),
            pl.BlockSpec((tj, d), lambda i, j: (j, 0)),
        ],
        out_specs=pl.BlockSpec((tm, d), lambda i, j: (i, 0)),
        out_shape=jax.ShapeDtypeStruct((t, d), F32),
        compiler_params=_cparams(2),
        name="merge",
    )(x, o_dn, o_gla, o_swa, u, u, u, w_dn, w_gla, w_swa, w_o)


def _lane_row(vec):
    return jnp.pad(vec.astype(F32), (0, LANES - vec.shape[0]))[None, :]


def kernel(x, ffn1_norm, ffn1_w1, ffn1_w3, ffn1_w2, mix_norm, w_in, dn_conv, dn_a_log, dn_dt_bias, dn_out_norm, gla_gate_up, gla_gate_bias, gla_out_norm, swa_q_norm, swa_k_norm, swa_sinks, w_branch_dn, w_branch_gla, w_branch_swa, w_out, ffn2_norm, ffn2_w1, ffn2_w3, ffn2_w2):
    assert x.shape[0] == 1 and x.shape[2] == D_MODEL
    xs = x[0]
    for l in range(DEPTH):
        xs = _ffn(xs, ffn1_norm[l][None], ffn1_w1[l], ffn1_w3[l], ffn1_w2[l])
        u = _inproj(xs, mix_norm[l][None], _build_w_in(w_in[l]))
        o_dn = _deltanet(u, dn_conv[l], _lane_row(dn_a_log[l]), _lane_row(dn_dt_bias[l]),
                         dn_out_norm[l][None])
        gate_up_pad = jnp.zeros((LANES, GLA_QK), BF16).at[SM_LR:SM_LR + GLA_RANK].set(
            _bf(gla_gate_up[l]))
        o_gla = _gla(u, gate_up_pad, gla_gate_bias[l][None], gla_out_norm[l][None])
        o_swa = _swa(u, jnp.tile(swa_q_norm[l], 2)[None], jnp.tile(swa_k_norm[l], 2)[None],
                     swa_sinks[l])
        xs = _merge(xs, o_dn, o_gla, o_swa, u, _bf(w_branch_dn[l]), _bf(w_branch_gla[l]),
                    _bf(w_branch_swa[l]), _bf(w_out[l]))
        xs = _ffn(xs, ffn2_norm[l][None], ffn2_w1[l], ffn2_w3[l], ffn2_w2[l])
    return xs[None]
```

```python
import functools

import jax
import jax.numpy as jnp
from jax import lax
from jax.experimental import pallas as pl
from jax.experimental.pallas import tpu as pltpu

F32 = jnp.float32
BF16 = jnp.bfloat16

D_MODEL = 2048
DEPTH = 2
EPS = 1e-6
D_FF = 5632
FFN_RES_SCALE = 0.5
DN_HEADS, DN_DK, DN_DV, DN_CONV = 8, 128, 128, 4
GLA_HEADS, GLA_DK, GLA_DV, GLA_RANK, GLA_TAU = 4, 128, 256, 16, 16.0
SWA_HQ, SWA_HKV, SWA_HD, SWA_WINDOW = 16, 4, 64, 128
N_BRANCH = 3
CHUNK = 64
DN_QK = DN_HEADS * DN_DK
DN_WIDTH = DN_HEADS * DN_DV
GLA_QK = GLA_HEADS * GLA_DK
GLA_WIDTH = GLA_HEADS * GLA_DV
SWA_WIDTH = SWA_HQ * SWA_HD
SWA_KV = SWA_HKV * SWA_HD

_SPLITS = (DN_QK, DN_QK, DN_WIDTH, DN_WIDTH, DN_HEADS, DN_HEADS,
           GLA_QK, GLA_QK, GLA_WIDTH, GLA_WIDTH, GLA_RANK,
           SWA_WIDTH, SWA_KV, SWA_KV, N_BRANCH * D_MODEL)
_OFFS = tuple(sum(_SPLITS[:i]) for i in range(len(_SPLITS) + 1))
(W_DNQ, W_DNK, W_DNV, W_DNZ, W_DNA, W_DNB, W_GLQ, W_GLK, W_GLV, W_GLR, W_GLLR,
 W_SWQ, W_SWK, W_SWV, W_GATE, W_END) = _OFFS

LANES = 128
SUBLANES = 8
PAIR = 2 * CHUNK

U_DN = 0
U_GLQ = U_DN + 4 * DN_QK
U_GLK = U_GLQ + GLA_QK
U_GLV = U_GLK + GLA_QK
U_GLR = U_GLV + GLA_WIDTH
U_SWQ = U_GLR + GLA_WIDTH
U_SWK = U_SWQ + SWA_WIDTH
U_SWV = U_SWK + SWA_HKV * LANES
U_COLS = U_SWV + SWA_HKV * LANES
SM_A, SM_B, SM_LR = 0, DN_HEADS, 2 * DN_HEADS

VMEM_LIMIT = 60 * 1024 * 1024


def _cparams(n_axes):
    return pltpu.CompilerParams(dimension_semantics=("arbitrary",) * n_axes,
                                vmem_limit_bytes=VMEM_LIMIT)


def _bf(t):
    return t.astype(BF16)


def _mm(a, b):
    return jnp.dot(_bf(a), _bf(b), preferred_element_type=F32)


def _mm_nt(a, b):
    return lax.dot_general(_bf(a), _bf(b), (((1,), (1,)), ((), ())), preferred_element_type=F32)


def _mm_tn(a, b):
    return lax.dot_general(_bf(a), _bf(b), (((0,), (0,)), ((), ())), preferred_element_type=F32)


def _silu(t):
    return t * jax.nn.sigmoid(t)


def _rms_rows(x, gain):
    ms = jnp.mean(x * x, axis=-1, keepdims=True)
    return x * lax.rsqrt(ms + EPS) * gain


def _each(fn, *lists):
    return [fn(*args) for args in zip(*lists)]


def _chunk_cumsum(tri, tiles):
    hi = _each(_bf, tiles)
    r1 = _each(lambda t, h: t - h.astype(F32), tiles, hi)
    mid = _each(_bf, r1)
    lo = _each(lambda r, m: _bf(r - m.astype(F32)), r1, mid)
    dot = functools.partial(jnp.dot, tri, preferred_element_type=F32)
    return _each(lambda h, m, l: dot(h) + dot(m) + dot(l), hi, mid, lo)


def _pair_masks():
    sub = lax.broadcasted_iota(jnp.int32, (PAIR, PAIR), 0)
    lane = lax.broadcasted_iota(jnp.int32, (PAIR, PAIR), 1)
    same = (sub >= CHUNK) == (lane >= CHUNK)
    return sub, lane, same


def _ffn_kernel(x_ref, g_ref, w1_ref, w3_ref, w2_ref, o_ref, h_ref):
    @pl.when(pl.program_id(1) == 0)
    def _():
        x = x_ref[...]
        h_ref[...] = _bf(_rms_rows(x, g_ref[...]))
        o_ref[...] = x

    h = h_ref[...]
    a = jnp.dot(h, _bf(w1_ref[...]), preferred_element_type=F32)
    b = jnp.dot(h, _bf(w3_ref[...]), preferred_element_type=F32)
    act = _silu(a) * (b * FFN_RES_SCALE)
    o_ref[...] += jnp.dot(_bf(act), _bf(w2_ref[...]), preferred_element_type=F32)


def _ffn(x, gain, w1, w3, w2, layer, *, tm=1024, tf=512):
    t, d = x.shape
    f = w1.shape[2]
    return pl.pallas_call(
        _ffn_kernel,
        grid=(t // tm, f // tf),
        in_specs=[
            pl.BlockSpec((tm, d), lambda i, j: (i, 0), pipeline_mode=pl.Buffered(1)),
            pl.BlockSpec((1, d), lambda i, j: (0, 0)),
            pl.BlockSpec((None, d, tf), lambda i, j: (layer, 0, j)),
            pl.BlockSpec((None, d, tf), lambda i, j: (layer, 0, j)),
            pl.BlockSpec((None, tf, d), lambda i, j: (layer, j, 0)),
        ],
        out_specs=pl.BlockSpec((tm, d), lambda i, j: (i, 0)),
        out_shape=jax.ShapeDtypeStruct((t, d), F32),
        scratch_shapes=[pltpu.VMEM((tm, d), BF16)],
        compiler_params=_cparams(2),
        name="ffn",
    )(x, gain, w1, w3, w2)


def _inproj_kernel(x_ref, g_ref, w_ref, ws_ref, u_ref, sm_ref, h_ref):
    @pl.when(pl.program_id(1) == 0)
    def _():
        h = _bf(_rms_rows(x_ref[...], g_ref[...]))
        h_ref[...] = h
        sm_ref[...] = jnp.dot(h, ws_ref[...], preferred_element_type=F32)

    u_ref[...] = jnp.dot(h_ref[...], w_ref[...], preferred_element_type=F32)


def _inproj(x, gain, w, w_small, *, tm=1024, tn=2304):
    t, d = x.shape
    n = w.shape[1]
    return pl.pallas_call(
        _inproj_kernel,
        grid=(t // tm, n // tn),
        in_specs=[
            pl.BlockSpec((tm, d), lambda i, j: (i, 0), pipeline_mode=pl.Buffered(1)),
            pl.BlockSpec((1, d), lambda i, j: (0, 0)),
            pl.BlockSpec((d, tn), lambda i, j: (0, j)),
            pl.BlockSpec((d, LANES), lambda i, j: (0, 0)),
        ],
        out_specs=[pl.BlockSpec((tm, tn), lambda i, j: (i, j)),
                   pl.BlockSpec((tm, LANES), lambda i, j: (i, 0))],
        out_shape=[jax.ShapeDtypeStruct((t, n), F32), jax.ShapeDtypeStruct((t, LANES), F32)],
        scratch_shapes=[pltpu.VMEM((tm, d), BF16)],
        compiler_params=_cparams(2),
        name="inproj",
    )(x, gain, w, w_small)


def _gateproj_kernel(x_ref, g_ref, w_ref, o_ref, h_ref):
    @pl.when(pl.program_id(1) == 0)
    def _():
        h_ref[...] = _bf(_rms_rows(x_ref[...], g_ref[...]))

    o_ref[...] = _bf(jax.nn.sigmoid(
        jnp.dot(h_ref[...], w_ref[...], preferred_element_type=F32)))


def _gateproj(x, gain, w, *, tm=1024, tn=2048):
    t, d = x.shape
    n = w.shape[1]
    return pl.pallas_call(
        _gateproj_kernel,
        grid=(t // tm, n // tn),
        in_specs=[
            pl.BlockSpec((tm, d), lambda i, j: (i, 0), pipeline_mode=pl.Buffered(1)),
            pl.BlockSpec((1, d), lambda i, j: (0, 0)),
            pl.BlockSpec((d, tn), lambda i, j: (0, j)),
        ],
        out_specs=pl.BlockSpec((tm, tn), lambda i, j: (i, j)),
        out_shape=jax.ShapeDtypeStruct((t, n), BF16),
        scratch_shapes=[pltpu.VMEM((tm, d), BF16)],
        compiler_params=_cparams(2),
        name="gateproj",
    )(x, gain, w)


def _build_w_in(w):
    d = w.shape[0]

    def dup(seg):
        s = seg.reshape(d, SWA_HKV, SWA_HD)
        return jnp.concatenate([s, s], axis=-1).reshape(d, SWA_HKV * LANES)

    w = _bf(w)
    small = jnp.concatenate(
        [w[:, W_DNA:W_GLQ], w[:, W_GLLR:W_SWQ],
         jnp.zeros((d, LANES - 2 * DN_HEADS - GLA_RANK), w.dtype)], axis=1)
    main = jnp.concatenate(
        [w[:, W_DNQ:W_DNA], w[:, W_GLQ:W_GLLR], w[:, W_SWQ:W_SWK],
         dup(w[:, W_SWK:W_SWV]), dup(w[:, W_SWV:W_GATE])], axis=1)
    return main, small, w[:, W_GATE:W_END]


def _tri_inv(a, sub, lane, eye):
    bd16 = (sub >> 4) == (lane >> 4)
    bd32 = (sub >> 5) == (lane >> 5)
    off32 = bd32 & jnp.logical_not(bd16)
    a16 = _each(lambda t: _bf(jnp.where(bd16, t, 0.0)), a)
    x = _each(lambda t: eye - t.astype(F32), a16)
    p = a16
    for _ in range(3):
        p = _each(lambda t: _bf(_mm(t, t)), p)
        x = _each(lambda xt, pt: xt + _mm(xt, pt), x, p)
    for blk in (lambda t: jnp.where(off32, t, 0.0), lambda t: jnp.where(bd32, 0.0, t)):
        xb = _each(_bf, x)
        y = _each(lambda at, xt: _mm(blk(at), xt), a, xb)
        x = _each(lambda xt, xbt, yt: xt - _mm(xbt, yt), x, xb, y)
    return x


def _dn_kernel(q_ref, k_ref, v_ref, z_ref, pq_ref, pk_ref, pv_ref, sm_ref,
               cq_ref, ck_ref, cv_ref, alog_ref, dtb_ref, on_ref, o_ref,
               xs_ref, gc_s, gt_s, b_s, qe_s, ol_s, pm_s, qc_s, gl_s, o_s, s_ref, *, tb_rows):
    tb = pl.program_id(0)
    head = pl.program_id(1)
    n_pairs = tb_rows // PAIR

    @pl.when(tb == 0)
    def _():
        s_ref[head] = jnp.zeros((DN_DK, DN_DV), F32)

    sub, lane, same = _pair_masks()
    causal = same & (sub >= lane)
    strict = same & (sub > lane)
    eye = jnp.where(sub == lane, 1.0, 0.0).astype(F32)
    tri = jnp.where(causal, 1.0, 0.0).astype(BF16)
    first = sub[:, :1] < CHUNK
    rows = [slice(p * PAIR, (p + 1) * PAIR) for p in range(n_pairs)]
    halves = [slice(c * CHUNK, (c + 1) * CHUNK) for c in range(2)]

    @pl.when(head == 0)
    def _():
        sm = sm_ref[...]
        b_s[...] = jax.nn.sigmoid(sm)
        gs = -jnp.exp(alog_ref[...]) * jax.nn.softplus(sm + dtb_ref[...])
        gall = _chunk_cumsum(tri, [gs[r] for r in rows])
        for r, gt in zip(rows, gall):
            gc_s[r, :] = gt
            gt_s[r, :] = gt.T

    keep = (tb > 0).astype(F32)

    def conv_silu(x_ref, p_ref, w_ref):
        xs_ref[0:SUBLANES, :] = p_ref[...] * keep
        xs_ref[SUBLANES:, :] = x_ref[...]
        w = w_ref[...]
        y = w[DN_CONV - 1:DN_CONV, :] * xs_ref[pl.ds(SUBLANES, tb_rows), :]
        for kk in range(DN_CONV - 1):
            off = SUBLANES - (DN_CONV - 1) + kk
            y = y + w[kk:kk + 1, :] * xs_ref[pl.ds(off, tb_rows), :]
        return _silu(y)

    qc = conv_silu(q_ref, pq_ref, cq_ref)
    qn = qc * lax.rsqrt(jnp.sum(qc * qc, axis=-1, keepdims=True) + EPS) * (DN_DK ** -0.5)
    kc = conv_silu(k_ref, pk_ref, ck_ref)
    kn = kc * lax.rsqrt(jnp.sum(kc * kc, axis=-1, keepdims=True) + EPS)
    vc = conv_silu(v_ref, pv_ref, cv_ref)

    q2 = [qn[r] for r in rows]
    k2 = [kn[r] for r in rows]
    g = [jnp.sum(jnp.where(lane == head + SM_A, gc_s[r, :], 0.0), axis=1, keepdims=True)
         for r in rows]
    beta = [jnp.sum(jnp.where(lane == head + SM_B, b_s[r, :], 0.0), axis=1, keepdims=True)
            for r in rows]
    grow = [jnp.sum(jnp.where(sub == head + SM_A, gt_s[r, :], 0.0), axis=0, keepdims=True)
            for r in rows]
    decay = _each(lambda gc, gr: jnp.where(causal, jnp.exp(jnp.where(causal, gc - gr, 0.0)), 0.0),
                  g, grow)
    eg = _each(jnp.exp, g)
    kb = _each(lambda kt, bt: kt * bt, k2, beta)
    k2b = _each(_bf, k2)
    a = _each(lambda kbt, kt, dt: jnp.where(strict, _mm_nt(kbt, kt) * dt, 0.0), kb, k2b, decay)
    at = _each(lambda qt, kt, dt: _bf(jnp.where(causal, _mm_nt(qt, kt) * dt, 0.0)), q2, k2b, decay)
    tm = _tri_inv(a, sub, lane, eye)
    rhs = [jnp.concatenate([vc[r] * bt, kbt * egt], axis=1)
           for r, bt, kbt, egt in zip(rows, beta, kb, eg)]
    sol = _each(_mm, tm, rhs)
    value = [_bf(t[:, :DN_DV]) for t in sol]
    kcum = [_bf(t[:, DN_DV:]) for t in sol]
    g_end = [[gt[CHUNK - 1:CHUNK, :], gt[PAIR - 1:PAIR, :]] for gt in g]
    kd = _each(lambda kt, gt, ge: _bf(kt * jnp.exp(jnp.where(first, ge[0], ge[1]) - gt)),
               k2, g, g_end)
    qe = _each(lambda qt, egt, att, kct: _bf(qt * egt - _mm(att, kct)), q2, eg, at, kcum)
    ol = _each(_mm, at, value)
    for p, r in enumerate(rows):
        qe_s[r, :] = qe[p]
        ol_s[r, :] = ol[p]
    for p in range(n_pairs):
        for c, sl in enumerate(halves):
            pm_s[2 * p + c] = _bf(-_mm_tn(kd[p][sl], kcum[p][sl]))
    for p in range(n_pairs):
        for c, sl in enumerate(halves):
            ci = 2 * p + c
            qc_s[ci] = _mm_tn(kd[p][sl], value[p][sl])
            gl_s[ci:ci + 1, :] = jnp.broadcast_to(jnp.exp(g_end[p][c]), (1, LANES))

    def step(c, s):
        rows = pl.ds(pl.multiple_of(c * CHUNK, CHUNK), CHUNK)
        sb = _bf(s)
        o_s[rows, :] = jnp.dot(qe_s[rows, :], sb, preferred_element_type=F32) + ol_s[rows, :]
        return (gl_s[pl.ds(c, 1), :] * s
                + jnp.dot(pm_s[c], sb, preferred_element_type=F32) + qc_s[c])

    s_ref[head] = lax.fori_loop(0, 2 * n_pairs, step, s_ref[head], unroll=2)
    o_ref[...] = _bf(_rms_rows(o_s[...], on_ref[...]) * _silu(z_ref[...]))


def _deltanet(u, small, conv_w, alog_row, dtb_row, out_norm, *, tb_rows=1024):
    t = u.shape[0]
    nb = DN_QK // LANES
    prev_blk = tb_rows // SUBLANES

    def col(seg):
        return lambda i, h: (i, U_DN // LANES + seg * nb + h)

    def prev(seg):
        return lambda i, h: (jnp.maximum(i * prev_blk - 1, 0), U_DN // LANES + seg * nb + h)

    def cw(seg):
        return lambda i, h: (0, seg * nb + h)

    row = pl.BlockSpec((1, LANES), lambda i, h: (0, 0))
    blk = lambda seg: pl.BlockSpec((tb_rows, LANES), col(seg))
    pblk = lambda seg: pl.BlockSpec((SUBLANES, LANES), prev(seg))
    cblk = lambda seg: pl.BlockSpec((DN_CONV, LANES), cw(seg))
    n_chunks = tb_rows // CHUNK
    rows_f32 = pltpu.VMEM((tb_rows, LANES), F32)
    return pl.pallas_call(
        functools.partial(_dn_kernel, tb_rows=tb_rows),
        grid=(t // tb_rows, DN_HEADS),
        in_specs=[blk(0), blk(1), blk(2), blk(3), pblk(0), pblk(1), pblk(2),
                  pl.BlockSpec((tb_rows, LANES), lambda i, h: (i, 0)),
                  cblk(0), cblk(1), cblk(2), row, row, row],
        out_specs=pl.BlockSpec((tb_rows, LANES), lambda i, h: (i, h)),
        out_shape=jax.ShapeDtypeStruct((t, DN_WIDTH), BF16),
        scratch_shapes=[
            pltpu.VMEM((tb_rows + SUBLANES, LANES), F32),
            rows_f32, rows_f32, rows_f32,
            pltpu.VMEM((tb_rows, DN_DK), BF16),
            rows_f32,
            pltpu.VMEM((n_chunks, DN_DK, DN_DK), BF16),
            pltpu.VMEM((n_chunks, DN_DK, DN_DV), F32),
            pltpu.VMEM((n_chunks, LANES), F32),
            rows_f32,
            pltpu.VMEM((DN_HEADS, DN_DK, DN_DV), F32),
        ],
        compiler_params=_cparams(2),
        name="deltanet",
    )(u, u, u, u, u, u, u, small, conv_w, conv_w, conv_w, alog_row, dtb_row, out_norm)


def _gla_kernel(q_ref, k_ref, v_ref, r_ref, sm_ref, gu_ref, gb_ref, on_ref, o_ref,
                qg_s, oi_s, kv_s, gl_s, st_ref, *, tb_rows):
    n_pairs = tb_rows // PAIR

    @pl.when(pl.program_id(1) == 0)
    def _():
        st_ref[...] = jnp.zeros_like(st_ref)

    logits = jnp.dot(_bf(sm_ref[...]), gu_ref[...], preferred_element_type=F32) + gb_ref[...]
    la = jax.nn.log_sigmoid(logits) * (1.0 / GLA_TAU)

    sub, lane, same = _pair_masks()
    causal = same & (sub >= lane)
    tri = jnp.where(causal, 1.0, 0.0).astype(BF16)
    first = sub < CHUNK
    gain = on_ref[...]

    rows = [slice(p * PAIR, (p + 1) * PAIR) for p in range(n_pairs)]
    halves = [slice(c * CHUNK, (c + 1) * CHUNK) for c in range(2)]
    g = _chunk_cumsum(tri, [la[r] for r in rows])
    k2 = [k_ref[r, :] for r in rows]
    v2 = [_bf(v_ref[r, :]) for r in rows]
    qg = [_bf(q_ref[r, :] * (GLA_DK ** -0.5) * jnp.exp(gt)) for r, gt in zip(rows, g)]
    kg = _each(lambda kt, gt: _bf(kt * jnp.exp(-gt)), k2, g)
    a = _each(lambda qt, kt: _bf(jnp.where(causal, _mm_nt(qt, kt), 0.0)), qg, kg)
    oi = _each(_mm, a, v2)
    g_end = [[gt[CHUNK - 1:CHUNK, :], gt[PAIR - 1:PAIR, :]] for gt in g]
    kd = _each(lambda kt, gt, ge: _bf(kt * jnp.exp(jnp.where(first, ge[0], ge[1]) - gt)),
               k2, g, g_end)
    for p, r in enumerate(rows):
        qg_s[r, :] = qg[p]
        oi_s[r, :] = oi[p]
        for c, sl in enumerate(halves):
            ci = 2 * p + c
            kv_s[ci] = _mm_tn(v2[p][sl], kd[p][sl])
            gl_s[ci:ci + 1, :] = jnp.exp(g_end[p][c])

    def step(c, st):
        rws = pl.ds(pl.multiple_of(c * CHUNK, CHUNK), CHUNK)
        o_c = _mm_nt(qg_s[rws, :], st) + oi_s[rws, :]
        o_ref[rws, :] = _bf(_rms_rows(o_c, gain) * _silu(r_ref[rws, :]))
        return st * gl_s[pl.ds(c, 1), :] + kv_s[c]

    st_ref[...] = lax.fori_loop(0, 2 * n_pairs, step, st_ref[...], unroll=4)


def _gla(u, small, gate_up_pad, gate_bias, out_norm, *, tb_rows=1024):
    t = u.shape[0]
    return pl.pallas_call(
        functools.partial(_gla_kernel, tb_rows=tb_rows),
        grid=(GLA_HEADS, t // tb_rows),
        in_specs=[
            pl.BlockSpec((tb_rows, GLA_DK), lambda h, i: (i, U_GLQ // GLA_DK + h)),
            pl.BlockSpec((tb_rows, GLA_DK), lambda h, i: (i, U_GLK // GLA_DK + h)),
            pl.BlockSpec((tb_rows, GLA_DV), lambda h, i: (i, U_GLV // GLA_DV + h)),
            pl.BlockSpec((tb_rows, GLA_DV), lambda h, i: (i, U_GLR // GLA_DV + h)),
            pl.BlockSpec((tb_rows, LANES), lambda h, i: (i, 0)),
            pl.BlockSpec((LANES, GLA_DK), lambda h, i: (0, h)),
            pl.BlockSpec((1, GLA_DK), lambda h, i: (0, h)),
            pl.BlockSpec((1, GLA_DV), lambda h, i: (0, 0)),
        ],
        out_specs=pl.BlockSpec((tb_rows, GLA_DV), lambda h, i: (i, h)),
        out_shape=jax.ShapeDtypeStruct((t, GLA_WIDTH), BF16),
        scratch_shapes=[
            pltpu.VMEM((tb_rows, GLA_DK), BF16),
            pltpu.VMEM((tb_rows, GLA_DV), F32),
            pltpu.VMEM((tb_rows // CHUNK, GLA_DV, GLA_DK), F32),
            pltpu.VMEM((tb_rows // CHUNK, GLA_DK), F32),
            pltpu.VMEM((GLA_DV, GLA_DK), F32),
        ],
        compiler_params=_cparams(2),
        name="gla",
    )(u, u, u, u, small, gate_up_pad, gate_bias, out_norm)


def _swa_kernel(q_ref, kc_ref, kp_ref, vc_ref, vp_ref, qg_ref, kg_ref, sink_ref, o_ref):
    n = pl.program_id(0)
    w = SWA_WINDOW
    lane = lax.broadcasted_iota(jnp.int32, (w, LANES), 1)
    lo = lane < SWA_HD
    lane2 = lax.broadcasted_iota(jnp.int32, (2 * w, LANES), 1)
    lo2 = lane2 < SWA_HD
    qi = lax.broadcasted_iota(jnp.int32, (w, 2 * w), 0)
    ki = lax.broadcasted_iota(jnp.int32, (w, 2 * w), 1)
    mask = (ki <= qi + w) & (ki > qi + w - SWA_WINDOW) & ((n > 0) | (ki >= w))
    kk = jnp.concatenate([kp_ref[...], kc_ref[...]], axis=0)
    vv = jnp.concatenate([vp_ref[...], vc_ref[...]], axis=0)
    qgain = qg_ref[...]
    kgain = kg_ref[...]
    heads_per_grp = SWA_HQ // SWA_HKV
    gcols = [slice(g * LANES, (g + 1) * LANES) for g in range(SWA_HKV)]
    pcols = [slice(p * LANES, (p + 1) * LANES) for p in range(SWA_HQ // 2)]

    def norm_k(kg):
        ms = jnp.sum(kg * kg, axis=-1, keepdims=True) * (1.0 / LANES)
        return _bf(kg * lax.rsqrt(ms + EPS) * kgain)

    def norm_q(qp):
        sq = qp * qp
        ms_lo = jnp.sum(jnp.where(lo, sq, 0.0), axis=-1, keepdims=True) * (1.0 / SWA_HD)
        ms_hi = jnp.sum(jnp.where(lo, 0.0, sq), axis=-1, keepdims=True) * (1.0 / SWA_HD)
        return qp * jnp.where(lo, lax.rsqrt(ms_lo + EPS), lax.rsqrt(ms_hi + EPS)) * qgain

    def probs(s, sink):
        s = jnp.where(mask, s * (SWA_HD ** -0.5), -jnp.inf)
        m = jnp.maximum(jnp.max(s, axis=-1, keepdims=True), sink)
        p = jnp.exp(s - m)
        return _bf(p / (jnp.sum(p, axis=-1, keepdims=True) + jnp.exp(sink - m)))

    kn = [norm_k(kk[:, c]) for c in gcols]
    v_half = [(_bf(jnp.where(lo2, vv[:, c], 0.0)), _bf(jnp.where(lo2, 0.0, vv[:, c])))
              for c in gcols]
    qn = [norm_q(q_ref[:, c]) for c in pcols]
    heads = range(SWA_HQ)
    qm = [_bf(jnp.where(lo, qn[h // 2], 0.0) if h % 2 == 0 else jnp.where(lo, 0.0, qn[h // 2]))
          for h in heads]
    s = [_mm_nt(qm[h], kn[h // heads_per_grp]) for h in heads]
    p = [probs(s[h], sink_ref[h]) for h in heads]
    part = [jnp.dot(p[h], v_half[h // heads_per_grp][h % 2], preferred_element_type=F32)
            for h in heads]
    for pr, c in enumerate(pcols):
        o_ref[:, c] = _bf(part[2 * pr] + part[2 * pr + 1])


def _swa(u, qgain, kgain, sinks):
    t = u.shape[0]
    w = SWA_WINDOW
    kvw = SWA_HKV * LANES
    cur = lambda c: (lambda n: (n, c))
    prv = lambda c: (lambda n: (jnp.maximum(n - 1, 0), c))
    row = pl.BlockSpec((1, LANES), lambda n: (0, 0))
    return pl.pallas_call(
        _swa_kernel,
        grid=(t // w,),
        in_specs=[
            pl.BlockSpec((w, SWA_WIDTH), cur(U_SWQ // SWA_WIDTH)),
            pl.BlockSpec((w, kvw), cur(U_SWK // kvw)),
            pl.BlockSpec((w, kvw), prv(U_SWK // kvw)),
            pl.BlockSpec((w, kvw), cur(U_SWV // kvw)),
            pl.BlockSpec((w, kvw), prv(U_SWV // kvw)),
            row, row,
            pl.BlockSpec(memory_space=pltpu.SMEM),
        ],
        out_specs=pl.BlockSpec((w, SWA_WIDTH), lambda n: (n, 0)),
        out_shape=jax.ShapeDtypeStruct((t, SWA_WIDTH), BF16),
        compiler_params=_cparams(1),
        name="swa",
    )(u, u, u, u, u, qgain, kgain, sinks)


def _merge_kernel(x_ref, od_ref, og_ref, os_ref, g0_ref, g1_ref, g2_ref,
                  wd_ref, wg_ref, ws_ref, wo_ref, o_ref):
    @pl.when(pl.program_id(1) == 0)
    def _():
        o_ref[...] = x_ref[...]

    dot = functools.partial(jnp.dot, preferred_element_type=F32)
    y = (g0_ref[...].astype(F32) * dot(od_ref[...], wd_ref[...])
         + g1_ref[...].astype(F32) * dot(og_ref[...], wg_ref[...])
         + g2_ref[...].astype(F32) * dot(os_ref[...], ws_ref[...]))
    o_ref[...] += dot(_bf(y), wo_ref[...])


def _merge(x, o_dn, o_gla, o_swa, gates, w_dn, w_gla, w_swa, w_o, *, tm=1024, tj=512):
    t, d = x.shape
    nj = d // tj
    gate = lambda b: (lambda i, j: (i, b * nj + j))
    act = lambda width: pl.BlockSpec((tm, width), lambda i, j: (i, 0))
    wcol = lambda width: pl.BlockSpec((width, tj), lambda i, j: (0, j))
    return pl.pallas_call(
        _merge_kernel,
        grid=(t // tm, nj),
        in_specs=[
            pl.BlockSpec((tm, d), lambda i, j: (i, 0), pipeline_mode=pl.Buffered(1)),
            act(DN_WIDTH), act(GLA_WIDTH), act(SWA_WIDTH),
            pl.BlockSpec((tm, tj), gate(0)), pl.BlockSpec((tm, tj), gate(1)),
            pl.BlockSpec((tm, tj), gate(2)),
            wcol(DN_WIDTH), wcol(GLA_WIDTH), wcol(SWA_WIDTH),
            pl.BlockSpec((tj, d), lambda i, j: (j, 0)),
        ],
        out_specs=pl.BlockSpec((tm, d), lambda i, j: (i, 0)),
        out_shape=jax.ShapeDtypeStruct((t, d), F32),
        compiler_params=_cparams(2),
        name="merge",
    )(x, o_dn, o_gla, o_swa, gates, gates, gates, w_dn, w_gla, w_swa, w_o)


def _lane_row(vec):
    return jnp.pad(vec.astype(F32), (0, LANES - vec.shape[0]))[None, :]


def kernel(x, ffn1_norm, ffn1_w1, ffn1_w3, ffn1_w2, mix_norm, w_in, dn_conv, dn_a_log, dn_dt_bias, dn_out_norm, gla_gate_up, gla_gate_bias, gla_out_norm, swa_q_norm, swa_k_norm, swa_sinks, w_branch_dn, w_branch_gla, w_branch_swa, w_out, ffn2_norm, ffn2_w1, ffn2_w3, ffn2_w2):
    assert x.shape[0] == 1 and x.shape[2] == D_MODEL
    xs = x[0]
    for l in range(DEPTH):
        xs = _ffn(xs, ffn1_norm[l][None], ffn1_w1, ffn1_w3, ffn1_w2, l)
        w_main, w_small, w_gate = _build_w_in(w_in[l])
        u, small = _inproj(xs, mix_norm[l][None], w_main, w_small)
        gates = _gateproj(xs, mix_norm[l][None], w_gate)
        o_dn = _deltanet(u, small, dn_conv[l], _lane_row(dn_a_log[l]), _lane_row(dn_dt_bias[l]),
                         dn_out_norm[l][None])
        gate_up_pad = jnp.zeros((LANES, GLA_QK), BF16).at[SM_LR:SM_LR + GLA_RANK].set(
            _bf(gla_gate_up[l]))
        o_gla = _gla(u, small, gate_up_pad, gla_gate_bias[l][None], gla_out_norm[l][None])
        o_swa = _swa(u, jnp.tile(swa_q_norm[l], 2)[None], jnp.tile(swa_k_norm[l], 2)[None],
                     swa_sinks[l])
        xs = _merge(xs, o_dn, o_gla, o_swa, gates, _bf(w_branch_dn[l]), _bf(w_branch_gla[l]),
                    _bf(w_branch_swa[l]), _bf(w_out[l]))
        xs = _ffn(xs, ffn2_norm[l][None], ffn2_w1, ffn2_w3, ffn2_w2, l)
    return xs[None]
```

```python
import functools

import numpy as np
import jax
import jax.numpy as jnp
from jax import lax
from jax.experimental import pallas as pl
from jax.experimental.pallas import tpu as pltpu

F32 = jnp.float32
BF16 = jnp.bfloat16

D_MODEL = 2048
DEPTH = 2
EPS = 1e-6
D_FF = 5632
FFN_RES_SCALE = 0.5
DN_HEADS, DN_DK, DN_DV, DN_CONV = 8, 128, 128, 4
GLA_HEADS, GLA_DK, GLA_DV, GLA_RANK, GLA_TAU = 4, 128, 256, 16, 16.0
SWA_HQ, SWA_HKV, SWA_HD, SWA_WINDOW = 16, 4, 64, 128
N_BRANCH = 3
CHUNK = 64
DN_QK = DN_HEADS * DN_DK
DN_WIDTH = DN_HEADS * DN_DV
GLA_QK = GLA_HEADS * GLA_DK
GLA_WIDTH = GLA_HEADS * GLA_DV
SWA_WIDTH = SWA_HQ * SWA_HD
SWA_KV = SWA_HKV * SWA_HD

_SPLITS = (DN_QK, DN_QK, DN_WIDTH, DN_WIDTH, DN_HEADS, DN_HEADS,
           GLA_QK, GLA_QK, GLA_WIDTH, GLA_WIDTH, GLA_RANK,
           SWA_WIDTH, SWA_KV, SWA_KV, N_BRANCH * D_MODEL)
_OFFS = tuple(sum(_SPLITS[:i]) for i in range(len(_SPLITS) + 1))
(W_DNQ, W_DNK, W_DNV, W_DNZ, W_DNA, W_DNB, W_GLQ, W_GLK, W_GLV, W_GLR, W_GLLR,
 W_SWQ, W_SWK, W_SWV, W_GATE, W_END) = _OFFS

LANES = 128
SUBLANES = 8
PAIR = 2 * CHUNK

U_DN = 0
U_GLQ = U_DN + 4 * DN_QK
U_GLK = U_GLQ + GLA_QK
U_GLV = U_GLK + GLA_QK
U_GLR = U_GLV + GLA_WIDTH
U_SWQ = U_GLR + GLA_WIDTH
U_SWK = U_SWQ + SWA_WIDTH
U_SWV = U_SWK + SWA_KV
U_SMALL = U_SWV + SWA_KV
SM_A, SM_B, SM_LR = 0, DN_HEADS, 2 * DN_HEADS

RELAY_BW = 512
_RELAY_REGIONS = ((W_GATE, N_BRANCH * D_MODEL // RELAY_BW), (W_DNQ, 4 * DN_QK // RELAY_BW),
                  (W_GLQ, (2 * GLA_QK + 2 * GLA_WIDTH) // RELAY_BW),
                  (W_SWQ, (SWA_WIDTH + 2 * SWA_KV) // RELAY_BW))
RELAY_NBLK = sum(n for _, n in _RELAY_REGIONS) + 1
WP_GATE_COLS = N_BRANCH * D_MODEL
U_COLS = (RELAY_NBLK * RELAY_BW) - WP_GATE_COLS
SMALL_MARK = -1

VMEM_LIMIT = 60 * 1024 * 1024


def _cparams(n_axes):
    return pltpu.CompilerParams(dimension_semantics=("arbitrary",) * n_axes,
                                vmem_limit_bytes=VMEM_LIMIT)


def _bf(t):
    return t.astype(BF16)


def _mm(a, b):
    return jnp.dot(_bf(a), _bf(b), preferred_element_type=F32)


def _mm_nt(a, b):
    return lax.dot_general(_bf(a), _bf(b), (((1,), (1,)), ((), ())), preferred_element_type=F32)


def _mm_tn(a, b):
    return lax.dot_general(_bf(a), _bf(b), (((0,), (0,)), ((), ())), preferred_element_type=F32)


def _silu(t):
    return t * jax.nn.sigmoid(t)


def _rms_rows(x, gain):
    ms = jnp.mean(x * x, axis=-1, keepdims=True)
    return x * lax.rsqrt(ms + EPS) * gain


def _each(fn, *lists):
    return [fn(*args) for args in zip(*lists)]


def _chunk_cumsum(tri, tiles):
    hi = _each(_bf, tiles)
    r1 = _each(lambda t, h: t - h.astype(F32), tiles, hi)
    mid = _each(_bf, r1)
    lo = _each(lambda r, m: _bf(r - m.astype(F32)), r1, mid)
    dot = functools.partial(jnp.dot, tri, preferred_element_type=F32)
    return _each(lambda h, m, l: dot(h) + dot(m) + dot(l), hi, mid, lo)


def _pair_masks():
    sub = lax.broadcasted_iota(jnp.int32, (PAIR, PAIR), 0)
    lane = lax.broadcasted_iota(jnp.int32, (PAIR, PAIR), 1)
    same = (sub >= CHUNK) == (lane >= CHUNK)
    return sub, lane, same


def _ffn_kernel(x_ref, g_ref, w1_ref, w3_ref, w2_ref, o_ref, h_ref):
    @pl.when(pl.program_id(1) == 0)
    def _():
        x = x_ref[...]
        h_ref[...] = _bf(_rms_rows(x, g_ref[...]))
        o_ref[...] = x

    h = h_ref[...]
    a = jnp.dot(h, _bf(w1_ref[...]), preferred_element_type=F32)
    b = jnp.dot(h, _bf(w3_ref[...]), preferred_element_type=F32)
    act = _silu(a) * (b * FFN_RES_SCALE)
    o_ref[...] += jnp.dot(_bf(act), _bf(w2_ref[...]), preferred_element_type=F32)


def _ffn(x, gain, w1, w3, w2, layer, *, tm=1024, tf=512):
    t, d = x.shape
    f = w1.shape[2]
    return pl.pallas_call(
        _ffn_kernel,
        grid=(t // tm, f // tf),
        in_specs=[
            pl.BlockSpec((tm, d), lambda i, j: (i, 0), pipeline_mode=pl.Buffered(1)),
            pl.BlockSpec((1, d), lambda i, j: (0, 0)),
            pl.BlockSpec((None, d, tf), lambda i, j: (layer, 0, j)),
            pl.BlockSpec((None, d, tf), lambda i, j: (layer, 0, j)),
            pl.BlockSpec((None, tf, d), lambda i, j: (layer, j, 0)),
        ],
        out_specs=pl.BlockSpec((tm, d), lambda i, j: (i, 0)),
        out_shape=jax.ShapeDtypeStruct((t, d), F32),
        scratch_shapes=[pltpu.VMEM((tm, d), BF16)],
        compiler_params=_cparams(2),
        name="ffn",
    )(x, gain, w1, w3, w2)


def _inproj_kernel(x_ref, g_ref, w_ref, u_ref, h_ref):
    @pl.when(pl.program_id(1) == 0)
    def _():
        h_ref[...] = _bf(_rms_rows(x_ref[...], g_ref[...]))

    u_ref[...] = jnp.dot(h_ref[...], w_ref[...], preferred_element_type=F32)


def _gateproj_kernel(x_ref, g_ref, w_ref, o_ref, h_ref):
    @pl.when(pl.program_id(1) == 0)
    def _():
        h_ref[...] = _bf(_rms_rows(x_ref[...], g_ref[...]))

    o_ref[...] = _bf(jax.nn.sigmoid(
        jnp.dot(h_ref[...], w_ref[...], preferred_element_type=F32)))


def _project(body, name, out_dtype, x, gain, wp, layer, col0, n, *, tm=1024, tn):
    t, d = x.shape
    assert col0 % tn == 0 and n % tn == 0
    return pl.pallas_call(
        body,
        grid=(t // tm, n // tn),
        in_specs=[
            pl.BlockSpec((tm, d), lambda i, j: (i, 0), pipeline_mode=pl.Buffered(1)),
            pl.BlockSpec((1, d), lambda i, j: (0, 0)),
            pl.BlockSpec((None, d, tn), lambda i, j: (layer, 0, col0 // tn + j)),
        ],
        out_specs=pl.BlockSpec((tm, tn), lambda i, j: (i, j)),
        out_shape=jax.ShapeDtypeStruct((t, n), out_dtype),
        scratch_shapes=[pltpu.VMEM((tm, d), BF16)],
        compiler_params=_cparams(2),
        name=name,
    )(x, gain, wp)


def _inproj(x, gain, wp, layer):
    return _project(_inproj_kernel, "inproj", F32, x, gain, wp, layer,
                    WP_GATE_COLS, U_COLS, tn=1536)


def _gateproj(x, gain, wp, layer):
    return _project(_gateproj_kernel, "gateproj", BF16, x, gain, wp, layer,
                    0, WP_GATE_COLS, tn=2048)


def _relay_table():
    per_bw = RELAY_BW // LANES
    rows = []
    for src, nblk in _RELAY_REGIONS:
        base, shift = divmod(src, LANES)
        assert (base * LANES) % RELAY_BW == 0
        for k in range(nblk):
            blk = base // per_bw + k
            rows.append((blk, (blk + 1) * per_bw, shift))
    assert W_DNA % RELAY_BW == 0 and W_DNA % LANES == SM_A
    assert W_GLLR % LANES == SM_LR
    rows.append((W_DNA // RELAY_BW, W_GLLR // LANES, SMALL_MARK))
    assert len(rows) == RELAY_NBLK
    return np.asarray(rows, np.int32).T.copy()


_RELAY_SHIFTS = tuple(sorted({src % LANES for src, _ in _RELAY_REGIONS}))


def _relayout_kernel(tab_ref, a_ref, b_ref, o_ref):
    code = tab_ref[2, pl.program_id(1)]
    a = a_ref[...]
    b = b_ref[...]
    for shift in _RELAY_SHIFTS:
        @pl.when(code == shift)
        def _(shift=shift):
            if shift == 0:
                o_ref[...] = _bf(a)
            else:
                x = jnp.concatenate([a, b], axis=1)
                o_ref[...] = _bf(pltpu.roll(x, x.shape[1] - shift, axis=1)[:, :RELAY_BW])

    @pl.when(code == SMALL_MARK)
    def _():
        lane = lax.broadcasted_iota(jnp.int32, b.shape, 1)
        small = jnp.where(lane < SM_LR, a[:, :LANES], jnp.where(lane < SM_LR + GLA_RANK, b, 0.0))
        o_ref[...] = jnp.zeros(o_ref.shape, o_ref.dtype)
        o_ref[:, :LANES] = _bf(small)


def _relayout_w_in(w_in):
    n_layers, d, _ = w_in.shape
    return pl.pallas_call(
        _relayout_kernel,
        grid_spec=pltpu.PrefetchScalarGridSpec(
            num_scalar_prefetch=1,
            grid=(n_layers, RELAY_NBLK),
            in_specs=[
                pl.BlockSpec((None, d, RELAY_BW), lambda l, j, tab: (l, 0, tab[0, j])),
                pl.BlockSpec((None, d, LANES), lambda l, j, tab: (l, 0, tab[1, j])),
            ],
            out_specs=pl.BlockSpec((None, d, RELAY_BW), lambda l, j, tab: (l, 0, j)),
        ),
        out_shape=jax.ShapeDtypeStruct((n_layers, d, RELAY_NBLK * RELAY_BW), BF16),
        compiler_params=_cparams(2),
        name="relayout_w_in",
    )(jnp.asarray(_relay_table()), w_in, w_in)


def _tri_inv(a, sub, lane, eye):
    bd16 = (sub >> 4) == (lane >> 4)
    bd32 = (sub >> 5) == (lane >> 5)
    off32 = bd32 & jnp.logical_not(bd16)
    a16 = _each(lambda t: _bf(jnp.where(bd16, t, 0.0)), a)
    x = _each(lambda t: eye - t.astype(F32), a16)
    p = a16
    for _ in range(3):
        p = _each(lambda t: _bf(_mm(t, t)), p)
        x = _each(lambda xt, pt: xt + _mm(xt, pt), x, p)
    for blk in (lambda t: jnp.where(off32, t, 0.0), lambda t: jnp.where(bd32, 0.0, t)):
        xb = _each(_bf, x)
        y = _each(lambda at, xt: _mm(blk(at), xt), a, xb)
        x = _each(lambda xt, xbt, yt: xt - _mm(xbt, yt), x, xb, y)
    return x


def _dn_kernel(q_ref, k_ref, v_ref, z_ref, pq_ref, pk_ref, pv_ref, sm_ref,
               cq_ref, ck_ref, cv_ref, alog_ref, dtb_ref, on_ref, o_ref,
               xs_ref, gc_s, gt_s, b_s, qe_s, ol_s, pm_s, qc_s, gl_s, o_s, s_ref,
               *, tb_rows, hs):
    tb = pl.program_id(0)
    hstep = pl.program_id(1)
    n_pairs = tb_rows // PAIR
    n_chunks = 2 * n_pairs
    heads = [hstep * hs + hh for hh in range(hs)]
    hcols = [slice(hh * LANES, (hh + 1) * LANES) for hh in range(hs)]

    @pl.when(tb == 0)
    def _():
        for head in heads:
            s_ref[head] = jnp.zeros((DN_DK, DN_DV), F32)

    sub, lane, same = _pair_masks()
    causal = same & (sub >= lane)
    strict = same & (sub > lane)
    eye = jnp.where(sub == lane, 1.0, 0.0).astype(F32)
    tri = jnp.where(causal, 1.0, 0.0).astype(BF16)
    first = sub[:, :1] < CHUNK
    rows = [slice(p * PAIR, (p + 1) * PAIR) for p in range(n_pairs)]
    halves = [slice(c * CHUNK, (c + 1) * CHUNK) for c in range(2)]

    @pl.when(hstep == 0)
    def _():
        sm = sm_ref[...]
        b_s[...] = jax.nn.sigmoid(sm)
        gs = -jnp.exp(alog_ref[...]) * jax.nn.softplus(sm + dtb_ref[...])
        gall = _chunk_cumsum(tri, [gs[r] for r in rows])
        for r, gt in zip(rows, gall):
            gc_s[r, :] = gt
            gt_s[r, :] = gt.T

    keep = (tb > 0).astype(F32)

    def conv_silu(x_ref, p_ref, w_ref):
        xs_ref[0:SUBLANES, :] = p_ref[...] * keep
        xs_ref[SUBLANES:, :] = x_ref[...]
        w = w_ref[...]
        y = w[DN_CONV - 1:DN_CONV, :] * xs_ref[pl.ds(SUBLANES, tb_rows), :]
        for kk in range(DN_CONV - 1):
            off = SUBLANES - (DN_CONV - 1) + kk
            y = y + w[kk:kk + 1, :] * xs_ref[pl.ds(off, tb_rows), :]
        return _silu(y)

    def l2norm(t):
        return t * lax.rsqrt(jnp.sum(t * t, axis=-1, keepdims=True) + EPS)

    qc = conv_silu(q_ref, pq_ref, cq_ref)
    kc = conv_silu(k_ref, pk_ref, ck_ref)
    vc = conv_silu(v_ref, pv_ref, cv_ref)
    qn = [l2norm(qc[:, c]) * (DN_DK ** -0.5) for c in hcols]
    kn = [l2norm(kc[:, c]) for c in hcols]

    tiles = [(hh, p) for hh in range(hs) for p in range(n_pairs)]
    q2 = [qn[hh][rows[p]] for hh, p in tiles]
    k2 = [kn[hh][rows[p]] for hh, p in tiles]
    v2 = [vc[rows[p], hcols[hh]] for hh, p in tiles]
    g = [jnp.sum(jnp.where(lane == heads[hh] + SM_A, gc_s[rows[p], :], 0.0), axis=1, keepdims=True)
         for hh, p in tiles]
    beta = [jnp.sum(jnp.where(lane == heads[hh] + SM_B, b_s[rows[p], :], 0.0), axis=1, keepdims=True)
            for hh, p in tiles]
    grow = [jnp.sum(jnp.where(sub == heads[hh] + SM_A, gt_s[rows[p], :], 0.0), axis=0, keepdims=True)
            for hh, p in tiles]
    decay = _each(lambda gc, gr: jnp.where(causal, jnp.exp(jnp.where(causal, gc - gr, 0.0)), 0.0),
                  g, grow)
    eg = _each(jnp.exp, g)
    kb = _each(lambda kt, bt: kt * bt, k2, beta)
    k2b = _each(_bf, k2)
    a = _each(lambda kbt, kt, dt: jnp.where(strict, _mm_nt(kbt, kt) * dt, 0.0), kb, k2b, decay)
    at = _each(lambda qt, kt, dt: _bf(jnp.where(causal, _mm_nt(qt, kt) * dt, 0.0)), q2, k2b, decay)
    tm = _tri_inv(a, sub, lane, eye)
    rhs = _each(lambda vt, bt, kbt, egt: jnp.concatenate([vt * bt, kbt * egt], axis=1),
                v2, beta, kb, eg)
    sol = _each(_mm, tm, rhs)
    value = [_bf(t[:, :DN_DV]) for t in sol]
    kcum = [_bf(t[:, DN_DV:]) for t in sol]
    g_end = [[gt[CHUNK - 1:CHUNK, :], gt[PAIR - 1:PAIR, :]] for gt in g]
    kd = _each(lambda kt, gt, ge: _bf(kt * jnp.exp(jnp.where(first, ge[0], ge[1]) - gt)),
               k2, g, g_end)
    qe = _each(lambda qt, egt, att, kct: _bf(qt * egt - _mm(att, kct)), q2, eg, at, kcum)
    ol = _each(_mm, at, value)
    for t, (hh, p) in enumerate(tiles):
        qe_s[hh, rows[p], :] = qe[t]
        ol_s[hh, rows[p], :] = ol[t]
    for t, (hh, p) in enumerate(tiles):
        for c, sl in enumerate(halves):
            pm_s[hh * n_chunks + 2 * p + c] = _bf(-_mm_tn(kd[t][sl], kcum[t][sl]))
    for t, (hh, p) in enumerate(tiles):
        for c, sl in enumerate(halves):
            ci = hh * n_chunks + 2 * p + c
            qc_s[ci] = _mm_tn(kd[t][sl], value[t][sl])
            gl_s[ci:ci + 1, :] = jnp.broadcast_to(jnp.exp(g_end[t][c]), (1, LANES))

    def step(c, states):
        crow = pl.ds(pl.multiple_of(c * CHUNK, CHUNK), CHUNK)
        sb = [_bf(s) for s in states]
        for hh in range(hs):
            o_s[crow, hcols[hh]] = (jnp.dot(qe_s[hh, crow, :], sb[hh], preferred_element_type=F32)
                                    + ol_s[hh, crow, :])
        upd = [jnp.dot(pm_s[hh * n_chunks + c], sb[hh], preferred_element_type=F32)
               for hh in range(hs)]
        return tuple(gl_s[pl.ds(hh * n_chunks + c, 1), :] * states[hh] + upd[hh]
                     + qc_s[hh * n_chunks + c] for hh in range(hs))

    final = lax.fori_loop(0, n_chunks, step, tuple(s_ref[head] for head in heads), unroll=2)
    for head, s in zip(heads, final):
        s_ref[head] = s
    gain = on_ref[...]
    for c in hcols:
        o_ref[:, c] = _bf(_rms_rows(o_s[:, c], gain) * _silu(z_ref[:, c]))


def _deltanet(u, conv_w, alog_row, dtb_row, out_norm, *, tb_rows=256, hs=8):
    t = u.shape[0]
    width = hs * LANES
    nb = DN_QK // width
    prev_blk = tb_rows // SUBLANES
    assert DN_DK == LANES and DN_DV == LANES and DN_HEADS % hs == 0

    def col(seg):
        return lambda i, h: (i, U_DN // width + seg * nb + h)

    def prev(seg):
        return lambda i, h: (jnp.maximum(i * prev_blk - 1, 0), U_DN // width + seg * nb + h)

    def cw(seg):
        return lambda i, h: (0, seg * nb + h)

    row = pl.BlockSpec((1, LANES), lambda i, h: (0, 0))
    blk = lambda seg: pl.BlockSpec((tb_rows, width), col(seg))
    pblk = lambda seg: pl.BlockSpec((SUBLANES, width), prev(seg))
    cblk = lambda seg: pl.BlockSpec((DN_CONV, width), cw(seg))
    n_chunks = tb_rows // CHUNK
    rows_f32 = pltpu.VMEM((tb_rows, LANES), F32)
    return pl.pallas_call(
        functools.partial(_dn_kernel, tb_rows=tb_rows, hs=hs),
        grid=(t // tb_rows, DN_HEADS // hs),
        in_specs=[blk(0), blk(1), blk(2), blk(3), pblk(0), pblk(1), pblk(2),
                  pl.BlockSpec((tb_rows, LANES), lambda i, h: (i, U_SMALL // LANES)),
                  cblk(0), cblk(1), cblk(2), row, row, row],
        out_specs=pl.BlockSpec((tb_rows, width), lambda i, h: (i, h)),
        out_shape=jax.ShapeDtypeStruct((t, DN_WIDTH), BF16),
        scratch_shapes=[
            pltpu.VMEM((tb_rows + SUBLANES, width), F32),
            rows_f32, rows_f32, rows_f32,
            pltpu.VMEM((hs, tb_rows, DN_DK), BF16),
            pltpu.VMEM((hs, tb_rows, DN_DV), F32),
            pltpu.VMEM((hs * n_chunks, DN_DK, DN_DK), BF16),
            pltpu.VMEM((hs * n_chunks, DN_DK, DN_DV), F32),
            pltpu.VMEM((hs * n_chunks, LANES), F32),
            pltpu.VMEM((tb_rows, width), F32),
            pltpu.VMEM((DN_HEADS, DN_DK, DN_DV), F32),
        ],
        compiler_params=_cparams(2),
        name="deltanet",
    )(u, u, u, u, u, u, u, u, conv_w, conv_w, conv_w, alog_row, dtb_row, out_norm)


def _gla_kernel(q_ref, k_ref, v_ref, r_ref, sm_ref, gu_ref, gb_ref, on_ref, o_ref,
                qg_s, oi_s, kv_s, gl_s, st_ref, *, tb_rows):
    n_pairs = tb_rows // PAIR

    @pl.when(pl.program_id(1) == 0)
    def _():
        st_ref[...] = jnp.zeros_like(st_ref)

    logits = jnp.dot(_bf(sm_ref[...]), gu_ref[...], preferred_element_type=F32) + gb_ref[...]
    la = jax.nn.log_sigmoid(logits) * (1.0 / GLA_TAU)

    sub, lane, same = _pair_masks()
    causal = same & (sub >= lane)
    tri = jnp.where(causal, 1.0, 0.0).astype(BF16)
    first = sub < CHUNK
    gain = on_ref[...]

    rows = [slice(p * PAIR, (p + 1) * PAIR) for p in range(n_pairs)]
    halves = [slice(c * CHUNK, (c + 1) * CHUNK) for c in range(2)]
    g = _chunk_cumsum(tri, [la[r] for r in rows])
    k2 = [k_ref[r, :] for r in rows]
    v2 = [_bf(v_ref[r, :]) for r in rows]
    qg = [_bf(q_ref[r, :] * (GLA_DK ** -0.5) * jnp.exp(gt)) for r, gt in zip(rows, g)]
    kg = _each(lambda kt, gt: _bf(kt * jnp.exp(-gt)), k2, g)
    a = _each(lambda qt, kt: _bf(jnp.where(causal, _mm_nt(qt, kt), 0.0)), qg, kg)
    oi = _each(_mm, a, v2)
    g_end = [[gt[CHUNK - 1:CHUNK, :], gt[PAIR - 1:PAIR, :]] for gt in g]
    kd = _each(lambda kt, gt, ge: _bf(kt * jnp.exp(jnp.where(first, ge[0], ge[1]) - gt)),
               k2, g, g_end)
    for p, r in enumerate(rows):
        qg_s[r, :] = qg[p]
        oi_s[r, :] = oi[p]
        for c, sl in enumerate(halves):
            ci = 2 * p + c
            kv_s[ci] = _mm_tn(v2[p][sl], kd[p][sl])
            gl_s[ci:ci + 1, :] = jnp.exp(g_end[p][c])

    def step(c, st):
        rws = pl.ds(pl.multiple_of(c * CHUNK, CHUNK), CHUNK)
        o_c = _mm_nt(qg_s[rws, :], st) + oi_s[rws, :]
        o_ref[rws, :] = _bf(_rms_rows(o_c, gain) * _silu(r_ref[rws, :]))
        return st * gl_s[pl.ds(c, 1), :] + kv_s[c]

    st_ref[...] = lax.fori_loop(0, 2 * n_pairs, step, st_ref[...], unroll=4)


def _gla(u, gate_up_pad, gate_bias, out_norm, *, tb_rows=1024):
    t = u.shape[0]
    return pl.pallas_call(
        functools.partial(_gla_kernel, tb_rows=tb_rows),
        grid=(GLA_HEADS, t // tb_rows),
        in_specs=[
            pl.BlockSpec((tb_rows, GLA_DK), lambda h, i: (i, U_GLQ // GLA_DK + h)),
            pl.BlockSpec((tb_rows, GLA_DK), lambda h, i: (i, U_GLK // GLA_DK + h)),
            pl.BlockSpec((tb_rows, GLA_DV), lambda h, i: (i, U_GLV // GLA_DV + h)),
            pl.BlockSpec((tb_rows, GLA_DV), lambda h, i: (i, U_GLR // GLA_DV + h)),
            pl.BlockSpec((tb_rows, LANES), lambda h, i: (i, U_SMALL // LANES)),
            pl.BlockSpec((LANES, GLA_DK), lambda h, i: (0, h)),
            pl.BlockSpec((1, GLA_DK), lambda h, i: (0, h)),
            pl.BlockSpec((1, GLA_DV), lambda h, i: (0, 0)),
        ],
        out_specs=pl.BlockSpec((tb_rows, GLA_DV), lambda h, i: (i, h)),
        out_shape=jax.ShapeDtypeStruct((t, GLA_WIDTH), BF16),
        scratch_shapes=[
            pltpu.VMEM((tb_rows, GLA_DK), BF16),
            pltpu.VMEM((tb_rows, GLA_DV), F32),
            pltpu.VMEM((tb_rows // CHUNK, GLA_DV, GLA_DK), F32),
            pltpu.VMEM((tb_rows // CHUNK, GLA_DK), F32),
            pltpu.VMEM((GLA_DV, GLA_DK), F32),
        ],
        compiler_params=_cparams(2),
        name="gla",
    )(u, u, u, u, u, gate_up_pad, gate_bias, out_norm)


def _swa_kernel(q_ref, kc_ref, kp_ref, vc_ref, vp_ref, qg_ref, kg_ref, sink_ref, o_ref):
    n = pl.program_id(0)
    w = SWA_WINDOW
    lane = lax.broadcasted_iota(jnp.int32, (w, LANES), 1)
    lo = lane < SWA_HD
    lane2 = lax.broadcasted_iota(jnp.int32, (2 * w, LANES), 1)
    lo2 = lane2 < SWA_HD
    qi = lax.broadcasted_iota(jnp.int32, (w, 2 * w), 0)
    ki = lax.broadcasted_iota(jnp.int32, (w, 2 * w), 1)
    mask = (ki <= qi + w) & (ki > qi + w - SWA_WINDOW) & ((n > 0) | (ki >= w))
    kk = jnp.concatenate([kp_ref[...], kc_ref[...]], axis=0)
    vv = jnp.concatenate([vp_ref[...], vc_ref[...]], axis=0)
    qgain = qg_ref[...]
    kgain = kg_ref[...]
    heads_per_grp = SWA_HQ // SWA_HKV
    pcols = [slice(p * LANES, (p + 1) * LANES) for p in range(SWA_HQ // 2)]

    def head_halves(t, g):
        tile = t[:, (g // 2) * LANES:(g // 2 + 1) * LANES]
        swapped = pltpu.roll(tile, SWA_HD, axis=1)
        return (tile, swapped) if g % 2 == 0 else (swapped, tile)

    def norm_k(kg):
        ms = jnp.sum(kg * kg, axis=-1, keepdims=True) * (1.0 / LANES)
        return _bf(kg * lax.rsqrt(ms + EPS) * kgain)

    def norm_q(qp):
        sq = qp * qp
        ms_lo = jnp.sum(jnp.where(lo, sq, 0.0), axis=-1, keepdims=True) * (1.0 / SWA_HD)
        ms_hi = jnp.sum(jnp.where(lo, 0.0, sq), axis=-1, keepdims=True) * (1.0 / SWA_HD)
        return qp * jnp.where(lo, lax.rsqrt(ms_lo + EPS), lax.rsqrt(ms_hi + EPS)) * qgain

    def probs(s, sink):
        s = jnp.where(mask, s * (SWA_HD ** -0.5), -jnp.inf)
        m = jnp.maximum(jnp.max(s, axis=-1, keepdims=True), sink)
        p = jnp.exp(s - m)
        return _bf(p / (jnp.sum(p, axis=-1, keepdims=True) + jnp.exp(sink - m)))

    groups = range(SWA_HKV)
    k_lo_hi = [head_halves(kk, g) for g in groups]
    kn = [norm_k(jnp.where(lo2, k_lo, k_hi)) for k_lo, k_hi in k_lo_hi]
    v_lo_hi = [head_halves(vv, g) for g in groups]
    v_half = [(_bf(jnp.where(lo2, v_lo, 0.0)), _bf(jnp.where(lo2, 0.0, v_hi)))
              for v_lo, v_hi in v_lo_hi]
    qn = [norm_q(q_ref[:, c]) for c in pcols]
    heads = range(SWA_HQ)
    qm = [_bf(jnp.where(lo, qn[h // 2], 0.0) if h % 2 == 0 else jnp.where(lo, 0.0, qn[h // 2]))
          for h in heads]
    s = [_mm_nt(qm[h], kn[h // heads_per_grp]) for h in heads]
    p = [probs(s[h], sink_ref[h]) for h in heads]
    part = [jnp.dot(p[h], v_half[h // heads_per_grp][h % 2], preferred_element_type=F32)
            for h in heads]
    for pr, c in enumerate(pcols):
        o_ref[:, c] = _bf(part[2 * pr] + part[2 * pr + 1])


def _swa(u, qgain, kgain, sinks):
    t = u.shape[0]
    w = SWA_WINDOW
    kvw = SWA_KV
    assert U_SWK % kvw == 0 and U_SWV % kvw == 0 and U_SWQ % SWA_WIDTH == 0
    cur = lambda c: (lambda n: (n, c))
    prv = lambda c: (lambda n: (jnp.maximum(n - 1, 0), c))
    row = pl.BlockSpec((1, LANES), lambda n: (0, 0))
    return pl.pallas_call(
        _swa_kernel,
        grid=(t // w,),
        in_specs=[
            pl.BlockSpec((w, SWA_WIDTH), cur(U_SWQ // SWA_WIDTH)),
            pl.BlockSpec((w, kvw), cur(U_SWK // kvw)),
            pl.BlockSpec((w, kvw), prv(U_SWK // kvw)),
            pl.BlockSpec((w, kvw), cur(U_SWV // kvw)),
            pl.BlockSpec((w, kvw), prv(U_SWV // kvw)),
            row, row,
            pl.BlockSpec(memory_space=pltpu.SMEM),
        ],
        out_specs=pl.BlockSpec((w, SWA_WIDTH), lambda n: (n, 0)),
        out_shape=jax.ShapeDtypeStruct((t, SWA_WIDTH), BF16),
        compiler_params=_cparams(1),
        name="swa",
    )(u, u, u, u, u, qgain, kgain, sinks)


def _merge_kernel(x_ref, od_ref, og_ref, os_ref, g0_ref, g1_ref, g2_ref,
                  wd_ref, wg_ref, ws_ref, wo_ref, o_ref):
    @pl.when(pl.program_id(1) == 0)
    def _():
        o_ref[...] = x_ref[...]

    dot = functools.partial(jnp.dot, preferred_element_type=F32)
    y = (g0_ref[...].astype(F32) * dot(od_ref[...], wd_ref[...])
         + g1_ref[...].astype(F32) * dot(og_ref[...], wg_ref[...])
         + g2_ref[...].astype(F32) * dot(os_ref[...], ws_ref[...]))
    o_ref[...] += dot(_bf(y), wo_ref[...])


def _merge(x, o_dn, o_gla, o_swa, gates, w_dn, w_gla, w_swa, w_o, *, tm=1024, tj=512):
    t, d = x.shape
    nj = d // tj
    gate = lambda b: (lambda i, j: (i, b * nj + j))
    act = lambda width: pl.BlockSpec((tm, width), lambda i, j: (i, 0))
    wcol = lambda width: pl.BlockSpec((width, tj), lambda i, j: (0, j))
    return pl.pallas_call(
        _merge_kernel,
        grid=(t // tm, nj),
        in_specs=[
            pl.BlockSpec((tm, d), lambda i, j: (i, 0), pipeline_mode=pl.Buffered(1)),
            act(DN_WIDTH), act(GLA_WIDTH), act(SWA_WIDTH),
            pl.BlockSpec((tm, tj), gate(0)), pl.BlockSpec((tm, tj), gate(1)),
            pl.BlockSpec((tm, tj), gate(2)),
            wcol(DN_WIDTH), wcol(GLA_WIDTH), wcol(SWA_WIDTH),
            pl.BlockSpec((tj, d), lambda i, j: (j, 0)),
        ],
        out_specs=pl.BlockSpec((tm, d), lambda i, j: (i, 0)),
        out_shape=jax.ShapeDtypeStruct((t, d), F32),
        compiler_params=_cparams(2),
        name="merge",
    )(x, o_dn, o_gla, o_swa, gates, gates, gates, w_dn, w_gla, w_swa, w_o)


def _lane_row(vec):
    return jnp.pad(vec.astype(F32), (0, LANES - vec.shape[0]))[None, :]


def kernel(x, ffn1_norm, ffn1_w1, ffn1_w3, ffn1_w2, mix_norm, w_in, dn_conv, dn_a_log, dn_dt_bias, dn_out_norm, gla_gate_up, gla_gate_bias, gla_out_norm, swa_q_norm, swa_k_norm, swa_sinks, w_branch_dn, w_branch_gla, w_branch_swa, w_out, ffn2_norm, ffn2_w1, ffn2_w3, ffn2_w2):
    assert x.shape[0] == 1 and x.shape[2] == D_MODEL
    xs = x[0]
    wp = _relayout_w_in(w_in)
    for l in range(DEPTH):
        xs = _ffn(xs, ffn1_norm[l][None], ffn1_w1, ffn1_w3, ffn1_w2, l)
        u = _inproj(xs, mix_norm[l][None], wp, l)
        gates = _gateproj(xs, mix_norm[l][None], wp, l)
        o_dn = _deltanet(u, dn_conv[l], _lane_row(dn_a_log[l]), _lane_row(dn_dt_bias[l]),
                         dn_out_norm[l][None])
        gate_up_pad = jnp.zeros((LANES, GLA_QK), BF16).at[SM_LR:SM_LR + GLA_RANK].set(
            _bf(gla_gate_up[l]))
        o_gla = _gla(u, gate_up_pad, gla_gate_bias[l][None], gla_out_norm[l][None])
        o_swa = _swa(u, jnp.tile(swa_q_norm[l], 2)[None], jnp.tile(swa_k_norm[l], 2)[None],
                     swa_sinks[l])
        xs = _merge(xs, o_dn, o_gla, o_swa, gates, _bf(w_branch_dn[l]), _bf(w_branch_gla[l]),
                    _bf(w_branch_swa[l]), _bf(w_out[l]))
        xs = _ffn(xs, ffn2_norm[l][None], ffn2_w1, ffn2_w3, ffn2_w2, l)
    return xs[None]
```

```python
import functools

import numpy as np
import jax
import jax.numpy as jnp
from jax import lax
from jax.experimental import pallas as pl
from jax.experimental.pallas import tpu as pltpu

F32 = jnp.float32
BF16 = jnp.bfloat16

D_MODEL = 2048
DEPTH = 2
EPS = 1e-6
D_FF = 5632
FFN_RES_SCALE = 0.5
DN_HEADS, DN_DK, DN_DV, DN_CONV = 8, 128, 128, 4
GLA_HEADS, GLA_DK, GLA_DV, GLA_RANK, GLA_TAU = 4, 128, 256, 16, 16.0
SWA_HQ, SWA_HKV, SWA_HD, SWA_WINDOW = 16, 4, 64, 128
N_BRANCH = 3
CHUNK = 64
DN_QK = DN_HEADS * DN_DK
DN_WIDTH = DN_HEADS * DN_DV
GLA_QK = GLA_HEADS * GLA_DK
GLA_WIDTH = GLA_HEADS * GLA_DV
SWA_WIDTH = SWA_HQ * SWA_HD
SWA_KV = SWA_HKV * SWA_HD

_SPLITS = (DN_QK, DN_QK, DN_WIDTH, DN_WIDTH, DN_HEADS, DN_HEADS,
           GLA_QK, GLA_QK, GLA_WIDTH, GLA_WIDTH, GLA_RANK,
           SWA_WIDTH, SWA_KV, SWA_KV, N_BRANCH * D_MODEL)
_OFFS = tuple(sum(_SPLITS[:i]) for i in range(len(_SPLITS) + 1))
(W_DNQ, W_DNK, W_DNV, W_DNZ, W_DNA, W_DNB, W_GLQ, W_GLK, W_GLV, W_GLR, W_GLLR,
 W_SWQ, W_SWK, W_SWV, W_GATE, W_END) = _OFFS

LANES = 128
SUBLANES = 8
PAIR = 2 * CHUNK

U_DN = 0
U_GLQ = U_DN + 4 * DN_QK
U_GLK = U_GLQ + GLA_QK
U_GLV = U_GLK + GLA_QK
U_GLR = U_GLV + GLA_WIDTH
U_SWQ = U_GLR + GLA_WIDTH
U_SWK = U_SWQ + SWA_WIDTH
U_SWV = U_SWK + SWA_KV
U_SMALL = U_SWV + SWA_KV
SM_A, SM_B, SM_LR = 0, DN_HEADS, 2 * DN_HEADS

RELAY_BW = 512
_RELAY_REGIONS = ((W_GATE, N_BRANCH * D_MODEL // RELAY_BW), (W_DNQ, 4 * DN_QK // RELAY_BW),
                  (W_GLQ, (2 * GLA_QK + 2 * GLA_WIDTH) // RELAY_BW),
                  (W_SWQ, (SWA_WIDTH + 2 * SWA_KV) // RELAY_BW))
RELAY_NBLK = sum(n for _, n in _RELAY_REGIONS) + 1
WP_GATE_COLS = N_BRANCH * D_MODEL
U_COLS = (RELAY_NBLK * RELAY_BW) - WP_GATE_COLS

VMEM_LIMIT = 60 * 1024 * 1024


def _cparams(n_axes):
    return pltpu.CompilerParams(dimension_semantics=("arbitrary",) * n_axes,
                                vmem_limit_bytes=VMEM_LIMIT)


def _bf(t):
    return t.astype(BF16)


def _mm(a, b):
    return jnp.dot(_bf(a), _bf(b), preferred_element_type=F32)


def _mm_nt(a, b):
    return lax.dot_general(_bf(a), _bf(b), (((1,), (1,)), ((), ())), preferred_element_type=F32)


def _mm_tn(a, b):
    return lax.dot_general(_bf(a), _bf(b), (((0,), (0,)), ((), ())), preferred_element_type=F32)


def _silu(t):
    return t * jax.nn.sigmoid(t)


def _rms_rows(x, gain):
    ms = jnp.mean(x * x, axis=-1, keepdims=True)
    return x * lax.rsqrt(ms + EPS) * gain


def _each(fn, *lists):
    return [fn(*args) for args in zip(*lists)]


def _chunk_cumsum(tri, tiles):
    hi = _each(_bf, tiles)
    r1 = _each(lambda t, h: t - h.astype(F32), tiles, hi)
    mid = _each(_bf, r1)
    lo = _each(lambda r, m: _bf(r - m.astype(F32)), r1, mid)
    dot = functools.partial(jnp.dot, tri, preferred_element_type=F32)
    return _each(lambda h, m, l: dot(h) + dot(m) + dot(l), hi, mid, lo)


def _pair_masks():
    sub = lax.broadcasted_iota(jnp.int32, (PAIR, PAIR), 0)
    lane = lax.broadcasted_iota(jnp.int32, (PAIR, PAIR), 1)
    same = (sub >= CHUNK) == (lane >= CHUNK)
    return sub, lane, same


def _ffn_kernel(x_ref, g_ref, w1_ref, w3_ref, w2_ref, o_ref, h_ref):
    @pl.when(pl.program_id(1) == 0)
    def _():
        x = x_ref[...]
        h_ref[...] = _bf(_rms_rows(x, g_ref[...]))
        o_ref[...] = x

    h = h_ref[...]
    a = jnp.dot(h, _bf(w1_ref[...]), preferred_element_type=F32)
    b = jnp.dot(h, _bf(w3_ref[...]), preferred_element_type=F32)
    act = _silu(a) * (b * FFN_RES_SCALE)
    o_ref[...] += jnp.dot(_bf(act), _bf(w2_ref[...]), preferred_element_type=F32)


def _ffn(x, gain, w1, w3, w2, layer, *, tm=1024, tf=512):
    t, d = x.shape
    f = w1.shape[2]
    return pl.pallas_call(
        _ffn_kernel,
        grid=(t // tm, f // tf),
        in_specs=[
            pl.BlockSpec((tm, d), lambda i, j: (i, 0), pipeline_mode=pl.Buffered(1)),
            pl.BlockSpec((1, d), lambda i, j: (0, 0)),
            pl.BlockSpec((None, d, tf), lambda i, j: (layer, 0, j)),
            pl.BlockSpec((None, d, tf), lambda i, j: (layer, 0, j)),
            pl.BlockSpec((None, tf, d), lambda i, j: (layer, j, 0)),
        ],
        out_specs=pl.BlockSpec((tm, d), lambda i, j: (i, 0)),
        out_shape=jax.ShapeDtypeStruct((t, d), F32),
        scratch_shapes=[pltpu.VMEM((tm, d), BF16)],
        compiler_params=_cparams(2),
        name="ffn",
    )(x, gain, w1, w3, w2)


def _inproj_kernel(x_ref, g_ref, w_ref, u_ref, h_ref):
    @pl.when(pl.program_id(1) == 0)
    def _():
        h_ref[...] = _bf(_rms_rows(x_ref[...], g_ref[...]))

    u_ref[...] = _mm_nt(h_ref[...], w_ref[...])


def _inproj(x, gain, wp, layer, *, tm=1024, tn=1536):
    t, d = x.shape
    assert WP_GATE_COLS % tn == 0 and U_COLS % tn == 0
    return pl.pallas_call(
        _inproj_kernel,
        grid=(t // tm, U_COLS // tn),
        in_specs=[
            pl.BlockSpec((tm, d), lambda i, j: (i, 0), pipeline_mode=pl.Buffered(1)),
            pl.BlockSpec((1, d), lambda i, j: (0, 0)),
            pl.BlockSpec((None, tn, d), lambda i, j: (layer, WP_GATE_COLS // tn + j, 0)),
        ],
        out_specs=[pl.BlockSpec((tm, tn), lambda i, j: (i, j)),
                   pl.BlockSpec((tm, d), lambda i, j: (i, 0))],
        out_shape=[jax.ShapeDtypeStruct((t, U_COLS), F32), jax.ShapeDtypeStruct((t, d), BF16)],
        compiler_params=_cparams(2),
        name="inproj",
    )(x, gain, wp)


def _relay_rows():
    rows = [src + k * RELAY_BW for src, nblk in _RELAY_REGIONS for k in range(nblk)]
    assert all(r % SUBLANES == 0 for r in rows) and len(rows) == RELAY_NBLK - 1
    return np.asarray(rows + [0], np.int32) // SUBLANES


def _relayout_kernel(tab_ref, src_ref, ab_ref, lr_ref, o_ref):
    last = pl.program_id(1) == RELAY_NBLK - 1

    @pl.when(jnp.logical_not(last))
    def _():
        o_ref[...] = _bf(src_ref[0])

    @pl.when(last)
    def _():
        n_ab, n_lr = ab_ref.shape[0], lr_ref.shape[0]
        o_ref[...] = jnp.zeros(o_ref.shape, o_ref.dtype)
        o_ref[SM_A:SM_A + n_ab, :] = _bf(ab_ref[...])
        o_ref[SM_LR:SM_LR + n_lr, :] = _bf(lr_ref[...])


def _relayout_w_in(w_t):
    n_layers, _, d = w_t.shape
    n_ab, n_lr = 2 * DN_HEADS, GLA_RANK
    assert W_DNA % n_ab == 0 and W_GLLR % n_lr == 0 and SM_LR == n_ab
    return pl.pallas_call(
        _relayout_kernel,
        grid_spec=pltpu.PrefetchScalarGridSpec(
            num_scalar_prefetch=1,
            grid=(n_layers, RELAY_NBLK),
            in_specs=[
                pl.BlockSpec((pl.Element(1), pl.Element(RELAY_BW), pl.Element(d)),
                             lambda l, j, tab: (l, tab[j] * SUBLANES, 0)),
                pl.BlockSpec((None, n_ab, d), lambda l, j, tab: (l, W_DNA // n_ab, 0)),
                pl.BlockSpec((None, n_lr, d), lambda l, j, tab: (l, W_GLLR // n_lr, 0)),
            ],
            out_specs=pl.BlockSpec((None, RELAY_BW, d), lambda l, j, tab: (l, j, 0)),
        ),
        out_shape=jax.ShapeDtypeStruct((n_layers, RELAY_NBLK * RELAY_BW, d), BF16),
        compiler_params=_cparams(2),
        name="relayout_w_in",
    )(jnp.asarray(_relay_rows()), w_t, w_t, w_t)


def _tri_inv(a, sub, lane, eye):
    bd16 = (sub >> 4) == (lane >> 4)
    bd32 = (sub >> 5) == (lane >> 5)
    off32 = bd32 & jnp.logical_not(bd16)
    a16 = _each(lambda t: _bf(jnp.where(bd16, t, 0.0)), a)
    x = _each(lambda t: eye - t.astype(F32), a16)
    p = a16
    for _ in range(3):
        p = _each(lambda t: _bf(_mm(t, t)), p)
        x = _each(lambda xt, pt: xt + _mm(xt, pt), x, p)
    for blk in (lambda t: jnp.where(off32, t, 0.0), lambda t: jnp.where(bd32, 0.0, t)):
        xb = _each(_bf, x)
        y = _each(lambda at, xt: _mm(blk(at), xt), a, xb)
        x = _each(lambda xt, xbt, yt: xt - _mm(xbt, yt), x, xb, y)
    return x


def _dn_kernel(q_ref, k_ref, v_ref, z_ref, pq_ref, pk_ref, pv_ref, sm_ref,
               cq_ref, ck_ref, cv_ref, alog_ref, dtb_ref, on_ref, o_ref,
               xs_ref, gc_s, gt_s, b_s, qe_s, ol_s, pm_s, qc_s, gl_s, o_s, s_ref,
               *, tb_rows, hs):
    tb = pl.program_id(0)
    hstep = pl.program_id(1)
    n_pairs = tb_rows // PAIR
    n_chunks = 2 * n_pairs
    heads = [hstep * hs + hh for hh in range(hs)]
    hcols = [slice(hh * LANES, (hh + 1) * LANES) for hh in range(hs)]

    @pl.when(tb == 0)
    def _():
        for head in heads:
            s_ref[head] = jnp.zeros((DN_DK, DN_DV), F32)

    sub, lane, same = _pair_masks()
    causal = same & (sub >= lane)
    strict = same & (sub > lane)
    eye = jnp.where(sub == lane, 1.0, 0.0).astype(F32)
    tri = jnp.where(causal, 1.0, 0.0).astype(BF16)
    first = sub[:, :1] < CHUNK
    rows = [slice(p * PAIR, (p + 1) * PAIR) for p in range(n_pairs)]
    halves = [slice(c * CHUNK, (c + 1) * CHUNK) for c in range(2)]

    @pl.when(hstep == 0)
    def _():
        sm = sm_ref[...]
        b_s[...] = jax.nn.sigmoid(sm)
        gs = -jnp.exp(alog_ref[...]) * jax.nn.softplus(sm + dtb_ref[...])
        gall = _chunk_cumsum(tri, [gs[r] for r in rows])
        for r, gt in zip(rows, gall):
            gc_s[r, :] = gt
            gt_s[r, :] = gt.T

    keep = (tb > 0).astype(F32)

    def conv_silu(x_ref, p_ref, w_ref):
        xs_ref[0:SUBLANES, :] = p_ref[...] * keep
        xs_ref[SUBLANES:, :] = x_ref[...]
        w = w_ref[...]
        y = w[DN_CONV - 1:DN_CONV, :] * xs_ref[pl.ds(SUBLANES, tb_rows), :]
        for kk in range(DN_CONV - 1):
            off = SUBLANES - (DN_CONV - 1) + kk
            y = y + w[kk:kk + 1, :] * xs_ref[pl.ds(off, tb_rows), :]
        return _silu(y)

    def l2norm(t):
        return t * lax.rsqrt(jnp.sum(t * t, axis=-1, keepdims=True) + EPS)

    qc = conv_silu(q_ref, pq_ref, cq_ref)
    kc = conv_silu(k_ref, pk_ref, ck_ref)
    vc = conv_silu(v_ref, pv_ref, cv_ref)
    qn = [l2norm(qc[:, c]) * (DN_DK ** -0.5) for c in hcols]
    kn = [l2norm(kc[:, c]) for c in hcols]

    tiles = [(hh, p) for hh in range(hs) for p in range(n_pairs)]
    q2 = [qn[hh][rows[p]] for hh, p in tiles]
    k2 = [kn[hh][rows[p]] for hh, p in tiles]
    v2 = [vc[rows[p], hcols[hh]] for hh, p in tiles]
    g = [jnp.sum(jnp.where(lane == heads[hh] + SM_A, gc_s[rows[p], :], 0.0), axis=1, keepdims=True)
         for hh, p in tiles]
    beta = [jnp.sum(jnp.where(lane == heads[hh] + SM_B, b_s[rows[p], :], 0.0), axis=1, keepdims=True)
            for hh, p in tiles]
    grow = [jnp.sum(jnp.where(sub == heads[hh] + SM_A, gt_s[rows[p], :], 0.0), axis=0, keepdims=True)
            for hh, p in tiles]
    decay = _each(lambda gc, gr: jnp.where(causal, jnp.exp(jnp.where(causal, gc - gr, 0.0)), 0.0),
                  g, grow)
    eg = _each(jnp.exp, g)
    kb = _each(lambda kt, bt: kt * bt, k2, beta)
    k2b = _each(_bf, k2)
    a = _each(lambda kbt, kt, dt: jnp.where(strict, _mm_nt(kbt, kt) * dt, 0.0), kb, k2b, decay)
    at = _each(lambda qt, kt, dt: _bf(jnp.where(causal, _mm_nt(qt, kt) * dt, 0.0)), q2, k2b, decay)
    tm = _tri_inv(a, sub, lane, eye)
    rhs = _each(lambda vt, bt, kbt, egt: jnp.concatenate([vt * bt, kbt * egt], axis=1),
                v2, beta, kb, eg)
    sol = _each(_mm, tm, rhs)
    value = [_bf(t[:, :DN_DV]) for t in sol]
    kcum = [_bf(t[:, DN_DV:]) for t in sol]
    g_end = [[gt[CHUNK - 1:CHUNK, :], gt[PAIR - 1:PAIR, :]] for gt in g]
    kd = _each(lambda kt, gt, ge: _bf(kt * jnp.exp(jnp.where(first, ge[0], ge[1]) - gt)),
               k2, g, g_end)
    qe = _each(lambda qt, egt, att, kct: _bf(qt * egt - _mm(att, kct)), q2, eg, at, kcum)
    ol = _each(_mm, at, value)
    for t, (hh, p) in enumerate(tiles):
        qe_s[hh, rows[p], :] = qe[t]
        ol_s[hh, rows[p], :] = ol[t]
    for t, (hh, p) in enumerate(tiles):
        for c, sl in enumerate(halves):
            pm_s[hh * n_chunks + 2 * p + c] = _bf(-_mm_tn(kd[t][sl], kcum[t][sl]))
    for t, (hh, p) in enumerate(tiles):
        for c, sl in enumerate(halves):
            ci = hh * n_chunks + 2 * p + c
            qc_s[ci] = _mm_tn(kd[t][sl], value[t][sl])
            gl_s[ci:ci + 1, :] = jnp.broadcast_to(jnp.exp(g_end[t][c]), (1, LANES))

    def step(c, states):
        crow = pl.ds(pl.multiple_of(c * CHUNK, CHUNK), CHUNK)
        sb = [_bf(s) for s in states]
        for hh in range(hs):
            o_s[crow, hcols[hh]] = (jnp.dot(qe_s[hh, crow, :], sb[hh], preferred_element_type=F32)
                                    + ol_s[hh, crow, :])
        upd = [jnp.dot(pm_s[hh * n_chunks + c], sb[hh], preferred_element_type=F32)
               for hh in range(hs)]
        return tuple(gl_s[pl.ds(hh * n_chunks + c, 1), :] * states[hh] + upd[hh]
                     + qc_s[hh * n_chunks + c] for hh in range(hs))

    final = lax.fori_loop(0, n_chunks, step, tuple(s_ref[head] for head in heads), unroll=2)
    for head, s in zip(heads, final):
        s_ref[head] = s
    gain = on_ref[...]
    for c in hcols:
        o_ref[:, c] = _bf(_rms_rows(o_s[:, c], gain) * _silu(z_ref[:, c]))


def _deltanet(u, conv_w, alog_row, dtb_row, out_norm, *, tb_rows=256, hs=8):
    t = u.shape[0]
    width = hs * LANES
    nb = DN_QK // width
    prev_blk = tb_rows // SUBLANES
    assert DN_DK == LANES and DN_DV == LANES and DN_HEADS % hs == 0

    def col(seg):
        return lambda i, h: (i, U_DN // width + seg * nb + h)

    def prev(seg):
        return lambda i, h: (jnp.maximum(i * prev_blk - 1, 0), U_DN // width + seg * nb + h)

    def cw(seg):
        return lambda i, h: (0, seg * nb + h)

    row = pl.BlockSpec((1, LANES), lambda i, h: (0, 0))
    blk = lambda seg: pl.BlockSpec((tb_rows, width), col(seg))
    pblk = lambda seg: pl.BlockSpec((SUBLANES, width), prev(seg))
    cblk = lambda seg: pl.BlockSpec((DN_CONV, width), cw(seg))
    n_chunks = tb_rows // CHUNK
    rows_f32 = pltpu.VMEM((tb_rows, LANES), F32)
    return pl.pallas_call(
        functools.partial(_dn_kernel, tb_rows=tb_rows, hs=hs),
        grid=(t // tb_rows, DN_HEADS // hs),
        in_specs=[blk(0), blk(1), blk(2), blk(3), pblk(0), pblk(1), pblk(2),
                  pl.BlockSpec((tb_rows, LANES), lambda i, h: (i, U_SMALL // LANES)),
                  cblk(0), cblk(1), cblk(2), row, row, row],
        out_specs=pl.BlockSpec((tb_rows, width), lambda i, h: (i, h)),
        out_shape=jax.ShapeDtypeStruct((t, DN_WIDTH), BF16),
        scratch_shapes=[
            pltpu.VMEM((tb_rows + SUBLANES, width), F32),
            rows_f32, rows_f32, rows_f32,
            pltpu.VMEM((hs, tb_rows, DN_DK), BF16),
            pltpu.VMEM((hs, tb_rows, DN_DV), F32),
            pltpu.VMEM((hs * n_chunks, DN_DK, DN_DK), BF16),
            pltpu.VMEM((hs * n_chunks, DN_DK, DN_DV), F32),
            pltpu.VMEM((hs * n_chunks, LANES), F32),
            pltpu.VMEM((tb_rows, width), F32),
            pltpu.VMEM((DN_HEADS, DN_DK, DN_DV), F32),
        ],
        compiler_params=_cparams(2),
        name="deltanet",
    )(u, u, u, u, u, u, u, u, conv_w, conv_w, conv_w, alog_row, dtb_row, out_norm)


def _gla_kernel(q_ref, k_ref, v_ref, r_ref, sm_ref, gu_ref, gb_ref, on_ref, o_ref,
                qg_s, oi_s, kv_s, gl_s, st_ref, *, tb_rows):
    n_pairs = tb_rows // PAIR

    @pl.when(pl.program_id(1) == 0)
    def _():
        st_ref[...] = jnp.zeros_like(st_ref)

    logits = jnp.dot(_bf(sm_ref[...]), gu_ref[...], preferred_element_type=F32) + gb_ref[...]
    la = jax.nn.log_sigmoid(logits) * (1.0 / GLA_TAU)

    sub, lane, same = _pair_masks()
    causal = same & (sub >= lane)
    tri = jnp.where(causal, 1.0, 0.0).astype(BF16)
    first = sub < CHUNK
    gain = on_ref[...]

    rows = [slice(p * PAIR, (p + 1) * PAIR) for p in range(n_pairs)]
    halves = [slice(c * CHUNK, (c + 1) * CHUNK) for c in range(2)]
    g = _chunk_cumsum(tri, [la[r] for r in rows])
    k2 = [k_ref[r, :] for r in rows]
    v2 = [_bf(v_ref[r, :]) for r in rows]
    qg = [_bf(q_ref[r, :] * (GLA_DK ** -0.5) * jnp.exp(gt)) for r, gt in zip(rows, g)]
    kg = _each(lambda kt, gt: _bf(kt * jnp.exp(-gt)), k2, g)
    a = _each(lambda qt, kt: _bf(jnp.where(causal, _mm_nt(qt, kt), 0.0)), qg, kg)
    oi = _each(_mm, a, v2)
    g_end = [[gt[CHUNK - 1:CHUNK, :], gt[PAIR - 1:PAIR, :]] for gt in g]
    kd = _each(lambda kt, gt, ge: _bf(kt * jnp.exp(jnp.where(first, ge[0], ge[1]) - gt)),
               k2, g, g_end)
    for p, r in enumerate(rows):
        qg_s[r, :] = qg[p]
        oi_s[r, :] = oi[p]
        for c, sl in enumerate(halves):
            ci = 2 * p + c
            kv_s[ci] = _mm_tn(v2[p][sl], kd[p][sl])
            gl_s[ci:ci + 1, :] = jnp.exp(g_end[p][c])

    def step(c, st):
        rws = pl.ds(pl.multiple_of(c * CHUNK, CHUNK), CHUNK)
        o_c = _mm_nt(qg_s[rws, :], st) + oi_s[rws, :]
        o_ref[rws, :] = _bf(_rms_rows(o_c, gain) * _silu(r_ref[rws, :]))
        return st * gl_s[pl.ds(c, 1), :] + kv_s[c]

    st_ref[...] = lax.fori_loop(0, 2 * n_pairs, step, st_ref[...], unroll=4)


def _gla(u, gate_up_pad, gate_bias, out_norm, *, tb_rows=1024):
    t = u.shape[0]
    return pl.pallas_call(
        functools.partial(_gla_kernel, tb_rows=tb_rows),
        grid=(GLA_HEADS, t // tb_rows),
        in_specs=[
            pl.BlockSpec((tb_rows, GLA_DK), lambda h, i: (i, U_GLQ // GLA_DK + h)),
            pl.BlockSpec((tb_rows, GLA_DK), lambda h, i: (i, U_GLK // GLA_DK + h)),
            pl.BlockSpec((tb_rows, GLA_DV), lambda h, i: (i, U_GLV // GLA_DV + h)),
            pl.BlockSpec((tb_rows, GLA_DV), lambda h, i: (i, U_GLR // GLA_DV + h)),
            pl.BlockSpec((tb_rows, LANES), lambda h, i: (i, U_SMALL // LANES)),
            pl.BlockSpec((LANES, GLA_DK), lambda h, i: (0, h)),
            pl.BlockSpec((1, GLA_DK), lambda h, i: (0, h)),
            pl.BlockSpec((1, GLA_DV), lambda h, i: (0, 0)),
        ],
        out_specs=pl.BlockSpec((tb_rows, GLA_DV), lambda h, i: (i, h)),
        out_shape=jax.ShapeDtypeStruct((t, GLA_WIDTH), BF16),
        scratch_shapes=[
            pltpu.VMEM((tb_rows, GLA_DK), BF16),
            pltpu.VMEM((tb_rows, GLA_DV), F32),
            pltpu.VMEM((tb_rows // CHUNK, GLA_DV, GLA_DK), F32),
            pltpu.VMEM((tb_rows // CHUNK, GLA_DK), F32),
            pltpu.VMEM((GLA_DV, GLA_DK), F32),
        ],
        compiler_params=_cparams(2),
        name="gla",
    )(u, u, u, u, u, gate_up_pad, gate_bias, out_norm)


def _swa_kernel(q_ref, kc_ref, kp_ref, vc_ref, vp_ref, qg_ref, kg_ref, sink_ref, o_ref):
    n = pl.program_id(0)
    w = SWA_WINDOW
    lane = lax.broadcasted_iota(jnp.int32, (w, LANES), 1)
    lo = lane < SWA_HD
    lane2 = lax.broadcasted_iota(jnp.int32, (2 * w, LANES), 1)
    lo2 = lane2 < SWA_HD
    qi = lax.broadcasted_iota(jnp.int32, (w, 2 * w), 0)
    ki = lax.broadcasted_iota(jnp.int32, (w, 2 * w), 1)
    mask = (ki <= qi + w) & (ki > qi + w - SWA_WINDOW) & ((n > 0) | (ki >= w))
    kk = jnp.concatenate([kp_ref[...], kc_ref[...]], axis=0)
    vv = jnp.concatenate([vp_ref[...], vc_ref[...]], axis=0)
    qgain = qg_ref[...]
    kgain = kg_ref[...]
    heads_per_grp = SWA_HQ // SWA_HKV
    pcols = [slice(p * LANES, (p + 1) * LANES) for p in range(SWA_HQ // 2)]

    def head_halves(t, g):
        tile = t[:, (g // 2) * LANES:(g // 2 + 1) * LANES]
        swapped = pltpu.roll(tile, SWA_HD, axis=1)
        return (tile, swapped) if g % 2 == 0 else (swapped, tile)

    def norm_k(kg):
        ms = jnp.sum(kg * kg, axis=-1, keepdims=True) * (1.0 / LANES)
        return _bf(kg * lax.rsqrt(ms + EPS) * kgain)

    def norm_q(qp):
        sq = qp * qp
        ms_lo = jnp.sum(jnp.where(lo, sq, 0.0), axis=-1, keepdims=True) * (1.0 / SWA_HD)
        ms_hi = jnp.sum(jnp.where(lo, 0.0, sq), axis=-1, keepdims=True) * (1.0 / SWA_HD)
        return qp * jnp.where(lo, lax.rsqrt(ms_lo + EPS), lax.rsqrt(ms_hi + EPS)) * qgain

    def probs(s, sink):
        s = jnp.where(mask, s * (SWA_HD ** -0.5), -jnp.inf)
        m = jnp.maximum(jnp.max(s, axis=-1, keepdims=True), sink)
        p = jnp.exp(s - m)
        return _bf(p / (jnp.sum(p, axis=-1, keepdims=True) + jnp.exp(sink - m)))

    groups = range(SWA_HKV)
    k_lo_hi = [head_halves(kk, g) for g in groups]
    kn = [norm_k(jnp.where(lo2, k_lo, k_hi)) for k_lo, k_hi in k_lo_hi]
    v_lo_hi = [head_halves(vv, g) for g in groups]
    v_half = [(_bf(jnp.where(lo2, v_lo, 0.0)), _bf(jnp.where(lo2, 0.0, v_hi)))
              for v_lo, v_hi in v_lo_hi]
    qn = [norm_q(q_ref[:, c]) for c in pcols]
    heads = range(SWA_HQ)
    qm = [_bf(jnp.where(lo, qn[h // 2], 0.0) if h % 2 == 0 else jnp.where(lo, 0.0, qn[h // 2]))
          for h in heads]
    s = [_mm_nt(qm[h], kn[h // heads_per_grp]) for h in heads]
    p = [probs(s[h], sink_ref[h]) for h in heads]
    part = [jnp.dot(p[h], v_half[h // heads_per_grp][h % 2], preferred_element_type=F32)
            for h in heads]
    for pr, c in enumerate(pcols):
        o_ref[:, c] = _bf(part[2 * pr] + part[2 * pr + 1])


def _swa(u, qgain, kgain, sinks):
    t = u.shape[0]
    w = SWA_WINDOW
    kvw = SWA_KV
    assert U_SWK % kvw == 0 and U_SWV % kvw == 0 and U_SWQ % SWA_WIDTH == 0
    cur = lambda c: (lambda n: (n, c))
    prv = lambda c: (lambda n: (jnp.maximum(n - 1, 0), c))
    row = pl.BlockSpec((1, LANES), lambda n: (0, 0))
    return pl.pallas_call(
        _swa_kernel,
        grid=(t // w,),
        in_specs=[
            pl.BlockSpec((w, SWA_WIDTH), cur(U_SWQ // SWA_WIDTH)),
            pl.BlockSpec((w, kvw), cur(U_SWK // kvw)),
            pl.BlockSpec((w, kvw), prv(U_SWK // kvw)),
            pl.BlockSpec((w, kvw), cur(U_SWV // kvw)),
            pl.BlockSpec((w, kvw), prv(U_SWV // kvw)),
            row, row,
            pl.BlockSpec(memory_space=pltpu.SMEM),
        ],
        out_specs=pl.BlockSpec((w, SWA_WIDTH), lambda n: (n, 0)),
        out_shape=jax.ShapeDtypeStruct((t, SWA_WIDTH), BF16),
        compiler_params=_cparams(1),
        name="swa",
    )(u, u, u, u, u, qgain, kgain, sinks)


def _merge_kernel(x_ref, h_ref, od_ref, og_ref, os_ref, g0_ref, g1_ref, g2_ref,
                  wd_ref, wg_ref, ws_ref, wo_ref, o_ref):
    @pl.when(pl.program_id(1) == 0)
    def _():
        o_ref[...] = x_ref[...]

    dot = functools.partial(jnp.dot, preferred_element_type=F32)
    h = h_ref[...]
    y = (jax.nn.sigmoid(_mm_nt(h, g0_ref[...])) * dot(od_ref[...], wd_ref[...])
         + jax.nn.sigmoid(_mm_nt(h, g1_ref[...])) * dot(og_ref[...], wg_ref[...])
         + jax.nn.sigmoid(_mm_nt(h, g2_ref[...])) * dot(os_ref[...], ws_ref[...]))
    o_ref[...] += dot(_bf(y), wo_ref[...])


def _merge(x, h, o_dn, o_gla, o_swa, wp, layer, w_dn, w_gla, w_swa, w_o, *, tm=512, tj=512):
    t, d = x.shape
    nj = d // tj
    gate = lambda b: (lambda i, j: (layer, b * nj + j, 0))
    act = lambda width: pl.BlockSpec((tm, width), lambda i, j: (i, 0))
    wcol = lambda width: pl.BlockSpec((width, tj), lambda i, j: (0, j))
    return pl.pallas_call(
        _merge_kernel,
        grid=(t // tm, nj),
        in_specs=[
            pl.BlockSpec((tm, d), lambda i, j: (i, 0), pipeline_mode=pl.Buffered(1)),
            act(d), act(DN_WIDTH), act(GLA_WIDTH), act(SWA_WIDTH),
            pl.BlockSpec((None, tj, d), gate(0)), pl.BlockSpec((None, tj, d), gate(1)),
            pl.BlockSpec((None, tj, d), gate(2)),
            wcol(DN_WIDTH), wcol(GLA_WIDTH), wcol(SWA_WIDTH),
            pl.BlockSpec((tj, d), lambda i, j: (j, 0)),
        ],
        out_specs=pl.BlockSpec((tm, d), lambda i, j: (i, 0)),
        out_shape=jax.ShapeDtypeStruct((t, d), F32),
        compiler_params=_cparams(2),
        name="merge",
    )(x, h, o_dn, o_gla, o_swa, wp, wp, wp, w_dn, w_gla, w_swa, w_o)


def _lane_row(vec):
    return jnp.pad(vec.astype(F32), (0, LANES - vec.shape[0]))[None, :]


def kernel(x, ffn1_norm, ffn1_w1, ffn1_w3, ffn1_w2, mix_norm, w_in, dn_conv, dn_a_log, dn_dt_bias, dn_out_norm, gla_gate_up, gla_gate_bias, gla_out_norm, swa_q_norm, swa_k_norm, swa_sinks, w_branch_dn, w_branch_gla, w_branch_swa, w_out, ffn2_norm, ffn2_w1, ffn2_w3, ffn2_w2):
    assert x.shape[0] == 1 and x.shape[2] == D_MODEL
    xs = x[0]
    wp = _relayout_w_in(jnp.swapaxes(w_in, 1, 2))
    for l in range(DEPTH):
        xs = _ffn(xs, ffn1_norm[l][None], ffn1_w1, ffn1_w3, ffn1_w2, l)
        u, h = _inproj(xs, mix_norm[l][None], wp, l)
        o_dn = _deltanet(u, dn_conv[l], _lane_row(dn_a_log[l]), _lane_row(dn_dt_bias[l]),
                         dn_out_norm[l][None])
        gate_up_pad = jnp.zeros((LANES, GLA_QK), BF16).at[SM_LR:SM_LR + GLA_RANK].set(
            _bf(gla_gate_up[l]))
        o_gla = _gla(u, gate_up_pad, gla_gate_bias[l][None], gla_out_norm[l][None])
        o_swa = _swa(u, jnp.tile(swa_q_norm[l], 2)[None], jnp.tile(swa_k_norm[l], 2)[None],
                     swa_sinks[l])
        xs = _merge(xs, h, o_dn, o_gla, o_swa, wp, l, _bf(w_branch_dn[l]), _bf(w_branch_gla[l]),
                    _bf(w_branch_swa[l]), _bf(w_out[l]))
        xs = _ffn(xs, ffn2_norm[l][None], ffn2_w1, ffn2_w3, ffn2_w2, l)
    return xs[None]
```

```python
import functools

import numpy as np
import jax
import jax.numpy as jnp
from jax import lax
from jax.experimental import pallas as pl
from jax.experimental.pallas import tpu as pltpu

F32 = jnp.float32
BF16 = jnp.bfloat16

D_MODEL = 2048
DEPTH = 2
EPS = 1e-6
D_FF = 5632
FFN_RES_SCALE = 0.5
DN_HEADS, DN_DK, DN_DV, DN_CONV = 8, 128, 128, 4
GLA_HEADS, GLA_DK, GLA_DV, GLA_RANK, GLA_TAU = 4, 128, 256, 16, 16.0
SWA_HQ, SWA_HKV, SWA_HD, SWA_WINDOW = 16, 4, 64, 128
N_BRANCH = 3
CHUNK = 64
DN_QK = DN_HEADS * DN_DK
DN_WIDTH = DN_HEADS * DN_DV
GLA_QK = GLA_HEADS * GLA_DK
GLA_WIDTH = GLA_HEADS * GLA_DV
SWA_WIDTH = SWA_HQ * SWA_HD
SWA_KV = SWA_HKV * SWA_HD

_SPLITS = (DN_QK, DN_QK, DN_WIDTH, DN_WIDTH, DN_HEADS, DN_HEADS,
           GLA_QK, GLA_QK, GLA_WIDTH, GLA_WIDTH, GLA_RANK,
           SWA_WIDTH, SWA_KV, SWA_KV, N_BRANCH * D_MODEL)
_OFFS = tuple(sum(_SPLITS[:i]) for i in range(len(_SPLITS) + 1))
(W_DNQ, W_DNK, W_DNV, W_DNZ, W_DNA, W_DNB, W_GLQ, W_GLK, W_GLV, W_GLR, W_GLLR,
 W_SWQ, W_SWK, W_SWV, W_GATE, W_END) = _OFFS

LANES = 128
SUBLANES = 8
PAIR = 2 * CHUNK

U_DN = 0
U_GLQ = U_DN + 4 * DN_QK
U_GLK = U_GLQ + GLA_QK
U_GLV = U_GLK + GLA_QK
U_GLR = U_GLV + GLA_WIDTH
U_SWQ = U_GLR + GLA_WIDTH
U_SWK = U_SWQ + SWA_WIDTH
U_SWV = U_SWK + SWA_KV
U_SMALL = U_SWV + SWA_KV
SM_A, SM_B, SM_LR = 0, DN_HEADS, 2 * DN_HEADS

RELAY_BW = 512
_RELAY_REGIONS = ((W_GATE, N_BRANCH * D_MODEL // RELAY_BW), (W_DNQ, 4 * DN_QK // RELAY_BW),
                  (W_GLQ, (2 * GLA_QK + 2 * GLA_WIDTH) // RELAY_BW),
                  (W_SWQ, (SWA_WIDTH + 2 * SWA_KV) // RELAY_BW))
RELAY_NBLK = sum(n for _, n in _RELAY_REGIONS) + 1
WP_GATE_COLS = N_BRANCH * D_MODEL
U_COLS = (RELAY_NBLK * RELAY_BW) - WP_GATE_COLS

VMEM_LIMIT = 60 * 1024 * 1024


def _cparams(n_axes):
    return pltpu.CompilerParams(dimension_semantics=("arbitrary",) * n_axes,
                                vmem_limit_bytes=VMEM_LIMIT)


def _bf(t):
    return t.astype(BF16)


def _mm(a, b):
    return jnp.dot(_bf(a), _bf(b), preferred_element_type=F32)


def _mm_nt(a, b):
    return lax.dot_general(_bf(a), _bf(b), (((1,), (1,)), ((), ())), preferred_element_type=F32)


def _mm_tn(a, b):
    return lax.dot_general(_bf(a), _bf(b), (((0,), (0,)), ((), ())), preferred_element_type=F32)


def _silu(t):
    return t * jax.nn.sigmoid(t)


def _rms_rows(x, gain):
    ms = jnp.mean(x * x, axis=-1, keepdims=True)
    return x * lax.rsqrt(ms + EPS) * gain


def _each(fn, *lists):
    return [fn(*args) for args in zip(*lists)]


def _chunk_cumsum(tri, tiles):
    hi = _each(_bf, tiles)
    r1 = _each(lambda t, h: t - h.astype(F32), tiles, hi)
    mid = _each(_bf, r1)
    lo = _each(lambda r, m: _bf(r - m.astype(F32)), r1, mid)
    dot = functools.partial(jnp.dot, tri, preferred_element_type=F32)
    return _each(lambda h, m, l: dot(h) + dot(m) + dot(l), hi, mid, lo)


def _pair_masks():
    sub = lax.broadcasted_iota(jnp.int32, (PAIR, PAIR), 0)
    lane = lax.broadcasted_iota(jnp.int32, (PAIR, PAIR), 1)
    same = (sub >= CHUNK) == (lane >= CHUNK)
    return sub, lane, same


def _ffn_kernel(x_ref, g_ref, w1_ref, w3_ref, w2_ref, o_ref, h_ref):
    @pl.when(pl.program_id(1) == 0)
    def _():
        x = x_ref[...]
        h_ref[...] = _bf(_rms_rows(x, g_ref[...]))
        o_ref[...] = x

    h = h_ref[...]
    a = jnp.dot(h, _bf(w1_ref[...]), preferred_element_type=F32)
    b = jnp.dot(h, _bf(w3_ref[...]), preferred_element_type=F32)
    act = _silu(a) * (b * FFN_RES_SCALE)
    o_ref[...] += jnp.dot(_bf(act), _bf(w2_ref[...]), preferred_element_type=F32)


def _ffn(x, gain, w1, w3, w2, layer, *, tm=1024, tf=512):
    t, d = x.shape
    f = w1.shape[2]
    return pl.pallas_call(
        _ffn_kernel,
        grid=(t // tm, f // tf),
        in_specs=[
            pl.BlockSpec((tm, d), lambda i, j: (i, 0), pipeline_mode=pl.Buffered(1)),
            pl.BlockSpec((1, d), lambda i, j: (0, 0)),
            pl.BlockSpec((None, d, tf), lambda i, j: (layer, 0, j)),
            pl.BlockSpec((None, d, tf), lambda i, j: (layer, 0, j)),
            pl.BlockSpec((None, tf, d), lambda i, j: (layer, j, 0)),
        ],
        out_specs=pl.BlockSpec((tm, d), lambda i, j: (i, 0)),
        out_shape=jax.ShapeDtypeStruct((t, d), F32),
        scratch_shapes=[pltpu.VMEM((tm, d), BF16)],
        compiler_params=_cparams(2),
        name="ffn",
    )(x, gain, w1, w3, w2)


def _inproj_kernel(x_ref, g_ref, w_ref, u_ref, h_ref):
    @pl.when(pl.program_id(1) == 0)
    def _():
        h_ref[...] = _bf(_rms_rows(x_ref[...], g_ref[...]))

    u_ref[...] = _mm_nt(h_ref[...], w_ref[...])


def _inproj(x, gain, wp, layer, *, tm=1024, tn=1536):
    t, d = x.shape
    assert WP_GATE_COLS % tn == 0 and U_COLS % tn == 0
    return pl.pallas_call(
        _inproj_kernel,
        grid=(t // tm, U_COLS // tn),
        in_specs=[
            pl.BlockSpec((tm, d), lambda i, j: (i, 0), pipeline_mode=pl.Buffered(1)),
            pl.BlockSpec((1, d), lambda i, j: (0, 0)),
            pl.BlockSpec((None, tn, d), lambda i, j: (layer, WP_GATE_COLS // tn + j, 0)),
        ],
        out_specs=[pl.BlockSpec((tm, tn), lambda i, j: (i, j)),
                   pl.BlockSpec((tm, d), lambda i, j: (i, 0))],
        out_shape=[jax.ShapeDtypeStruct((t, U_COLS), F32), jax.ShapeDtypeStruct((t, d), BF16)],
        compiler_params=_cparams(2),
        name="inproj",
    )(x, gain, wp)


def _relay_rows():
    rows = [src + k * RELAY_BW for src, nblk in _RELAY_REGIONS for k in range(nblk)]
    assert all(r % SUBLANES == 0 for r in rows) and len(rows) == RELAY_NBLK - 1
    return np.asarray(rows + [0], np.int32) // SUBLANES


def _relayout_kernel(tab_ref, src_ref, ab_ref, lr_ref, o_ref):
    last = pl.program_id(1) == RELAY_NBLK - 1

    @pl.when(jnp.logical_not(last))
    def _():
        o_ref[...] = _bf(src_ref[0])

    @pl.when(last)
    def _():
        n_ab, n_lr = ab_ref.shape[0], lr_ref.shape[0]
        o_ref[...] = jnp.zeros(o_ref.shape, o_ref.dtype)
        o_ref[SM_A:SM_A + n_ab, :] = _bf(ab_ref[...])
        o_ref[SM_LR:SM_LR + n_lr, :] = _bf(lr_ref[...])


def _relayout_w_in(w_t):
    n_layers, _, d = w_t.shape
    n_ab, n_lr = 2 * DN_HEADS, GLA_RANK
    assert W_DNA % n_ab == 0 and W_GLLR % n_lr == 0 and SM_LR == n_ab
    return pl.pallas_call(
        _relayout_kernel,
        grid_spec=pltpu.PrefetchScalarGridSpec(
            num_scalar_prefetch=1,
            grid=(n_layers, RELAY_NBLK),
            in_specs=[
                pl.BlockSpec((pl.Element(1), pl.Element(RELAY_BW), pl.Element(d)),
                             lambda l, j, tab: (l, tab[j] * SUBLANES, 0)),
                pl.BlockSpec((None, n_ab, d), lambda l, j, tab: (l, W_DNA // n_ab, 0)),
                pl.BlockSpec((None, n_lr, d), lambda l, j, tab: (l, W_GLLR // n_lr, 0)),
            ],
            out_specs=pl.BlockSpec((None, RELAY_BW, d), lambda l, j, tab: (l, j, 0)),
        ),
        out_shape=jax.ShapeDtypeStruct((n_layers, RELAY_NBLK * RELAY_BW, d), BF16),
        compiler_params=_cparams(2),
        name="relayout_w_in",
    )(jnp.asarray(_relay_rows()), w_t, w_t, w_t)


def _tri_inv(a, sub, lane, eye):
    bd16 = (sub >> 4) == (lane >> 4)
    bd32 = (sub >> 5) == (lane >> 5)
    off32 = bd32 & jnp.logical_not(bd16)
    a16 = _each(lambda t: _bf(jnp.where(bd16, t, 0.0)), a)
    x = _each(lambda t: eye - t.astype(F32), a16)
    p = a16
    for _ in range(3):
        p = _each(lambda t: _bf(_mm(t, t)), p)
        x = _each(lambda xt, pt: xt + _mm(xt, pt), x, p)
    for blk in (lambda t: jnp.where(off32, t, 0.0), lambda t: jnp.where(bd32, 0.0, t)):
        xb = _each(_bf, x)
        y = _each(lambda at, xt: _mm(blk(at), xt), a, xb)
        x = _each(lambda xt, xbt, yt: xt - _mm(xbt, yt), x, xb, y)
    return x


def _dn_kernel(q_ref, k_ref, v_ref, z_ref, pq_ref, pk_ref, pv_ref, sm_ref,
               cq_ref, ck_ref, cv_ref, alog_ref, dtb_ref, on_ref, o_ref,
               xs_ref, gc_s, gt_s, b_s, qe_s, ol_s, pm_s, qc_s, gl_s, o_s, s_ref,
               *, tb_rows, hs):
    tb = pl.program_id(0)
    hstep = pl.program_id(1)
    n_pairs = tb_rows // PAIR
    n_chunks = 2 * n_pairs
    heads = [hstep * hs + hh for hh in range(hs)]
    hcols = [slice(hh * LANES, (hh + 1) * LANES) for hh in range(hs)]

    @pl.when(tb == 0)
    def _():
        for head in heads:
            s_ref[head] = jnp.zeros((DN_DK, DN_DV), F32)

    sub, lane, same = _pair_masks()
    causal = same & (sub >= lane)
    strict = same & (sub > lane)
    eye = jnp.where(sub == lane, 1.0, 0.0).astype(F32)
    tri = jnp.where(causal, 1.0, 0.0).astype(BF16)
    first = sub[:, :1] < CHUNK
    rows = [slice(p * PAIR, (p + 1) * PAIR) for p in range(n_pairs)]
    halves = [slice(c * CHUNK, (c + 1) * CHUNK) for c in range(2)]

    @pl.when(hstep == 0)
    def _():
        sm = sm_ref[...]
        b_s[...] = jax.nn.sigmoid(sm)
        gs = -jnp.exp(alog_ref[...]) * jax.nn.softplus(sm + dtb_ref[...])
        gall = _chunk_cumsum(tri, [gs[r] for r in rows])
        for r, gt in zip(rows, gall):
            gc_s[r, :] = gt
            gt_s[r, :] = gt.T

    keep = (tb > 0).astype(F32)

    def conv_silu(x_ref, p_ref, w_ref):
        xs_ref[0:SUBLANES, :] = p_ref[...] * keep
        xs_ref[SUBLANES:, :] = x_ref[...]
        w = w_ref[...]
        y = w[DN_CONV - 1:DN_CONV, :] * xs_ref[pl.ds(SUBLANES, tb_rows), :]
        for kk in range(DN_CONV - 1):
            off = SUBLANES - (DN_CONV - 1) + kk
            y = y + w[kk:kk + 1, :] * xs_ref[pl.ds(off, tb_rows), :]
        return _silu(y)

    def l2norm(t):
        return t * lax.rsqrt(jnp.sum(t * t, axis=-1, keepdims=True) + EPS)

    qc = conv_silu(q_ref, pq_ref, cq_ref)
    kc = conv_silu(k_ref, pk_ref, ck_ref)
    vc = conv_silu(v_ref, pv_ref, cv_ref)
    qn = [l2norm(qc[:, c]) * (DN_DK ** -0.5) for c in hcols]
    kn = [l2norm(kc[:, c]) for c in hcols]

    tiles = [(hh, p) for hh in range(hs) for p in range(n_pairs)]
    q2 = [qn[hh][rows[p]] for hh, p in tiles]
    k2 = [kn[hh][rows[p]] for hh, p in tiles]
    v2 = [vc[rows[p], hcols[hh]] for hh, p in tiles]
    g = [jnp.sum(jnp.where(lane == heads[hh] + SM_A, gc_s[rows[p], :], 0.0), axis=1, keepdims=True)
         for hh, p in tiles]
    beta = [jnp.sum(jnp.where(lane == heads[hh] + SM_B, b_s[rows[p], :], 0.0), axis=1, keepdims=True)
            for hh, p in tiles]
    grow = [jnp.sum(jnp.where(sub == heads[hh] + SM_A, gt_s[rows[p], :], 0.0), axis=0, keepdims=True)
            for hh, p in tiles]
    decay = _each(lambda gc, gr: jnp.where(causal, jnp.exp(jnp.where(causal, gc - gr, 0.0)), 0.0),
                  g, grow)
    eg = _each(jnp.exp, g)
    kb = _each(lambda kt, bt: kt * bt, k2, beta)
    k2b = _each(_bf, k2)
    a = _each(lambda kbt, kt, dt: jnp.where(strict, _mm_nt(kbt, kt) * dt, 0.0), kb, k2b, decay)
    at = _each(lambda qt, kt, dt: _bf(jnp.where(causal, _mm_nt(qt, kt) * dt, 0.0)), q2, k2b, decay)
    tm = _tri_inv(a, sub, lane, eye)
    rhs = _each(lambda vt, bt, kbt, egt: jnp.concatenate([vt * bt, kbt * egt], axis=1),
                v2, beta, kb, eg)
    sol = _each(_mm, tm, rhs)
    value = [_bf(t[:, :DN_DV]) for t in sol]
    kcum = [_bf(t[:, DN_DV:]) for t in sol]
    g_end = [[gt[CHUNK - 1:CHUNK, :], gt[PAIR - 1:PAIR, :]] for gt in g]
    kd = _each(lambda kt, gt, ge: _bf(kt * jnp.exp(jnp.where(first, ge[0], ge[1]) - gt)),
               k2, g, g_end)
    qe = _each(lambda qt, egt, att, kct: _bf(qt * egt - _mm(att, kct)), q2, eg, at, kcum)
    ol = _each(_mm, at, value)
    for t, (hh, p) in enumerate(tiles):
        qe_s[hh, rows[p], :] = qe[t]
        ol_s[hh, rows[p], :] = ol[t]
    for t, (hh, p) in enumerate(tiles):
        for c, sl in enumerate(halves):
            pm_s[hh * n_chunks + 2 * p + c] = _bf(-_mm_tn(kd[t][sl], kcum[t][sl]))
    for t, (hh, p) in enumerate(tiles):
        for c, sl in enumerate(halves):
            ci = hh * n_chunks + 2 * p + c
            qc_s[ci] = _mm_tn(kd[t][sl], value[t][sl])
            gl_s[ci:ci + 1, :] = jnp.broadcast_to(jnp.exp(g_end[t][c]), (1, LANES))

    def step(c, states):
        crow = pl.ds(pl.multiple_of(c * CHUNK, CHUNK), CHUNK)
        sb = [_bf(s) for s in states]
        for hh in range(hs):
            o_s[crow, hcols[hh]] = (jnp.dot(qe_s[hh, crow, :], sb[hh], preferred_element_type=F32)
                                    + ol_s[hh, crow, :])
        upd = [jnp.dot(pm_s[hh * n_chunks + c], sb[hh], preferred_element_type=F32)
               for hh in range(hs)]
        return tuple(gl_s[pl.ds(hh * n_chunks + c, 1), :] * states[hh] + upd[hh]
                     + qc_s[hh * n_chunks + c] for hh in range(hs))

    final = lax.fori_loop(0, n_chunks, step, tuple(s_ref[head] for head in heads), unroll=2)
    for head, s in zip(heads, final):
        s_ref[head] = s
    gain = on_ref[...]
    for c in hcols:
        o_ref[:, c] = _bf(_rms_rows(o_s[:, c], gain) * _silu(z_ref[:, c]))


def _deltanet(u, conv_w, alog_row, dtb_row, out_norm, *, tb_rows=256, hs=8):
    t = u.shape[0]
    width = hs * LANES
    nb = DN_QK // width
    prev_blk = tb_rows // SUBLANES
    assert DN_DK == LANES and DN_DV == LANES and DN_HEADS % hs == 0

    def col(seg):
        return lambda i, h: (i, U_DN // width + seg * nb + h)

    def prev(seg):
        return lambda i, h: (jnp.maximum(i * prev_blk - 1, 0), U_DN // width + seg * nb + h)

    def cw(seg):
        return lambda i, h: (0, seg * nb + h)

    row = pl.BlockSpec((1, LANES), lambda i, h: (0, 0))
    blk = lambda seg: pl.BlockSpec((tb_rows, width), col(seg))
    pblk = lambda seg: pl.BlockSpec((SUBLANES, width), prev(seg))
    cblk = lambda seg: pl.BlockSpec((DN_CONV, width), cw(seg))
    n_chunks = tb_rows // CHUNK
    rows_f32 = pltpu.VMEM((tb_rows, LANES), F32)
    return pl.pallas_call(
        functools.partial(_dn_kernel, tb_rows=tb_rows, hs=hs),
        grid=(t // tb_rows, DN_HEADS // hs),
        in_specs=[blk(0), blk(1), blk(2), blk(3), pblk(0), pblk(1), pblk(2),
                  pl.BlockSpec((tb_rows, LANES), lambda i, h: (i, U_SMALL // LANES)),
                  cblk(0), cblk(1), cblk(2), row, row, row],
        out_specs=pl.BlockSpec((tb_rows, width), lambda i, h: (i, h)),
        out_shape=jax.ShapeDtypeStruct((t, DN_WIDTH), BF16),
        scratch_shapes=[
            pltpu.VMEM((tb_rows + SUBLANES, width), F32),
            rows_f32, rows_f32, rows_f32,
            pltpu.VMEM((hs, tb_rows, DN_DK), BF16),
            pltpu.VMEM((hs, tb_rows, DN_DV), F32),
            pltpu.VMEM((hs * n_chunks, DN_DK, DN_DK), BF16),
            pltpu.VMEM((hs * n_chunks, DN_DK, DN_DV), F32),
            pltpu.VMEM((hs * n_chunks, LANES), F32),
            pltpu.VMEM((tb_rows, width), F32),
            pltpu.VMEM((DN_HEADS, DN_DK, DN_DV), F32),
        ],
        compiler_params=_cparams(2),
        name="deltanet",
    )(u, u, u, u, u, u, u, u, conv_w, conv_w, conv_w, alog_row, dtb_row, out_norm)


def _gla_kernel(q_ref, k_ref, v_ref, r_ref, sm_ref, gu_ref, gb_ref, on_ref, o_ref,
                qg_s, oi_s, kv_s, gl_s, st_ref, *, tb_rows):
    n_pairs = tb_rows // PAIR

    @pl.when(pl.program_id(1) == 0)
    def _():
        st_ref[...] = jnp.zeros_like(st_ref)

    logits = jnp.dot(_bf(sm_ref[...]), gu_ref[...], preferred_element_type=F32) + gb_ref[...]
    la = jax.nn.log_sigmoid(logits) * (1.0 / GLA_TAU)

    sub, lane, same = _pair_masks()
    causal = same & (sub >= lane)
    tri = jnp.where(causal, 1.0, 0.0).astype(BF16)
    first = sub < CHUNK
    gain = on_ref[...]

    rows = [slice(p * PAIR, (p + 1) * PAIR) for p in range(n_pairs)]
    halves = [slice(c * CHUNK, (c + 1) * CHUNK) for c in range(2)]
    g = _chunk_cumsum(tri, [la[r] for r in rows])
    k2 = [k_ref[r, :] for r in rows]
    v2 = [_bf(v_ref[r, :]) for r in rows]
    qg = [_bf(q_ref[r, :] * (GLA_DK ** -0.5) * jnp.exp(gt)) for r, gt in zip(rows, g)]
    kg = _each(lambda kt, gt: _bf(kt * jnp.exp(-gt)), k2, g)
    a = _each(lambda qt, kt: _bf(jnp.where(causal, _mm_nt(qt, kt), 0.0)), qg, kg)
    oi = _each(_mm, a, v2)
    g_end = [[gt[CHUNK - 1:CHUNK, :], gt[PAIR - 1:PAIR, :]] for gt in g]
    kd = _each(lambda kt, gt, ge: _bf(kt * jnp.exp(jnp.where(first, ge[0], ge[1]) - gt)),
               k2, g, g_end)
    for p, r in enumerate(rows):
        qg_s[r, :] = qg[p]
        oi_s[r, :] = oi[p]
        for c, sl in enumerate(halves):
            ci = 2 * p + c
            kv_s[ci] = _mm_tn(v2[p][sl], kd[p][sl])
            gl_s[ci:ci + 1, :] = jnp.exp(g_end[p][c])

    def step(c, st):
        rws = pl.ds(pl.multiple_of(c * CHUNK, CHUNK), CHUNK)
        o_c = _mm_nt(qg_s[rws, :], st) + oi_s[rws, :]
        o_ref[rws, :] = _bf(_rms_rows(o_c, gain) * _silu(r_ref[rws, :]))
        return st * gl_s[pl.ds(c, 1), :] + kv_s[c]

    st_ref[...] = lax.fori_loop(0, 2 * n_pairs, step, st_ref[...], unroll=4)


def _gla(u, gate_up_pad, gate_bias, out_norm, *, tb_rows=1024):
    t = u.shape[0]
    return pl.pallas_call(
        functools.partial(_gla_kernel, tb_rows=tb_rows),
        grid=(GLA_HEADS, t // tb_rows),
        in_specs=[
            pl.BlockSpec((tb_rows, GLA_DK), lambda h, i: (i, U_GLQ // GLA_DK + h)),
            pl.BlockSpec((tb_rows, GLA_DK), lambda h, i: (i, U_GLK // GLA_DK + h)),
            pl.BlockSpec((tb_rows, GLA_DV), lambda h, i: (i, U_GLV // GLA_DV + h)),
            pl.BlockSpec((tb_rows, GLA_DV), lambda h, i: (i, U_GLR // GLA_DV + h)),
            pl.BlockSpec((tb_rows, LANES), lambda h, i: (i, U_SMALL // LANES)),
            pl.BlockSpec((LANES, GLA_DK), lambda h, i: (0, h)),
            pl.BlockSpec((1, GLA_DK), lambda h, i: (0, h)),
            pl.BlockSpec((1, GLA_DV), lambda h, i: (0, 0)),
        ],
        out_specs=pl.BlockSpec((tb_rows, GLA_DV), lambda h, i: (i, h)),
        out_shape=jax.ShapeDtypeStruct((t, GLA_WIDTH), BF16),
        scratch_shapes=[
            pltpu.VMEM((tb_rows, GLA_DK), BF16),
            pltpu.VMEM((tb_rows, GLA_DV), F32),
            pltpu.VMEM((tb_rows // CHUNK, GLA_DV, GLA_DK), F32),
            pltpu.VMEM((tb_rows // CHUNK, GLA_DK), F32),
            pltpu.VMEM((GLA_DV, GLA_DK), F32),
        ],
        compiler_params=_cparams(2),
        name="gla",
    )(u, u, u, u, u, gate_up_pad, gate_bias, out_norm)


def _swa_kernel(q_ref, kc_ref, kp_ref, vc_ref, vp_ref, qg_ref, kg_ref, sink_ref, o_ref):
    n = pl.program_id(0)
    w = SWA_WINDOW
    lane = lax.broadcasted_iota(jnp.int32, (w, LANES), 1)
    lo = lane < SWA_HD
    lane2 = lax.broadcasted_iota(jnp.int32, (2 * w, LANES), 1)
    lo2 = lane2 < SWA_HD
    qi = lax.broadcasted_iota(jnp.int32, (w, 2 * w), 0)
    ki = lax.broadcasted_iota(jnp.int32, (w, 2 * w), 1)
    mask = (ki <= qi + w) & (ki > qi + w - SWA_WINDOW) & ((n > 0) | (ki >= w))
    kk = jnp.concatenate([kp_ref[...], kc_ref[...]], axis=0)
    vv = jnp.concatenate([vp_ref[...], vc_ref[...]], axis=0)
    qgain = qg_ref[...]
    kgain = kg_ref[...]
    heads_per_grp = SWA_HQ // SWA_HKV
    pcols = [slice(p * LANES, (p + 1) * LANES) for p in range(SWA_HQ // 2)]

    def head_halves(t, g):
        tile = t[:, (g // 2) * LANES:(g // 2 + 1) * LANES]
        swapped = pltpu.roll(tile, SWA_HD, axis=1)
        return (tile, swapped) if g % 2 == 0 else (swapped, tile)

    def norm_k(kg):
        ms = jnp.sum(kg * kg, axis=-1, keepdims=True) * (1.0 / LANES)
        return _bf(kg * lax.rsqrt(ms + EPS) * kgain)

    def norm_q(qp):
        sq = qp * qp
        ms_lo = jnp.sum(jnp.where(lo, sq, 0.0), axis=-1, keepdims=True) * (1.0 / SWA_HD)
        ms_hi = jnp.sum(jnp.where(lo, 0.0, sq), axis=-1, keepdims=True) * (1.0 / SWA_HD)
        return qp * jnp.where(lo, lax.rsqrt(ms_lo + EPS), lax.rsqrt(ms_hi + EPS)) * qgain

    def probs(s, sink):
        s = jnp.where(mask, s * (SWA_HD ** -0.5), -jnp.inf)
        m = jnp.maximum(jnp.max(s, axis=-1, keepdims=True), sink)
        p = jnp.exp(s - m)
        return _bf(p / (jnp.sum(p, axis=-1, keepdims=True) + jnp.exp(sink - m)))

    groups = range(SWA_HKV)
    k_lo_hi = [head_halves(kk, g) for g in groups]
    kn = [norm_k(jnp.where(lo2, k_lo, k_hi)) for k_lo, k_hi in k_lo_hi]
    v_lo_hi = [head_halves(vv, g) for g in groups]
    v_half = [(_bf(jnp.where(lo2, v_lo, 0.0)), _bf(jnp.where(lo2, 0.0, v_hi)))
              for v_lo, v_hi in v_lo_hi]
    qn = [norm_q(q_ref[:, c]) for c in pcols]
    heads = range(SWA_HQ)
    qm = [_bf(jnp.where(lo, qn[h // 2], 0.0) if h % 2 == 0 else jnp.where(lo, 0.0, qn[h // 2]))
          for h in heads]
    s = [_mm_nt(qm[h], kn[h // heads_per_grp]) for h in heads]
    p = [probs(s[h], sink_ref[h]) for h in heads]
    part = [jnp.dot(p[h], v_half[h // heads_per_grp][h % 2], preferred_element_type=F32)
            for h in heads]
    for pr, c in enumerate(pcols):
        o_ref[:, c] = _bf(part[2 * pr] + part[2 * pr + 1])


def _swa(u, qgain, kgain, sinks):
    t = u.shape[0]
    w = SWA_WINDOW
    kvw = SWA_KV
    assert U_SWK % kvw == 0 and U_SWV % kvw == 0 and U_SWQ % SWA_WIDTH == 0
    cur = lambda c: (lambda n: (n, c))
    prv = lambda c: (lambda n: (jnp.maximum(n - 1, 0), c))
    row = pl.BlockSpec((1, LANES), lambda n: (0, 0))
    return pl.pallas_call(
        _swa_kernel,
        grid=(t // w,),
        in_specs=[
            pl.BlockSpec((w, SWA_WIDTH), cur(U_SWQ // SWA_WIDTH)),
            pl.BlockSpec((w, kvw), cur(U_SWK // kvw)),
            pl.BlockSpec((w, kvw), prv(U_SWK // kvw)),
            pl.BlockSpec((w, kvw), cur(U_SWV // kvw)),
            pl.BlockSpec((w, kvw), prv(U_SWV // kvw)),
            row, row,
            pl.BlockSpec(memory_space=pltpu.SMEM),
        ],
        out_specs=pl.BlockSpec((w, SWA_WIDTH), lambda n: (n, 0)),
        out_shape=jax.ShapeDtypeStruct((t, SWA_WIDTH), BF16),
        compiler_params=_cparams(1),
        name="swa",
    )(u, u, u, u, u, qgain, kgain, sinks)


def _gatemix_kernel(h_ref, od_ref, og_ref, os_ref, g0_ref, g1_ref, g2_ref,
                    wd_ref, wg_ref, ws_ref, y_ref):
    dot = functools.partial(jnp.dot, preferred_element_type=F32)
    h = h_ref[...]
    y = (jax.nn.sigmoid(_mm_nt(h, g0_ref[...])) * dot(od_ref[...], wd_ref[...])
         + jax.nn.sigmoid(_mm_nt(h, g1_ref[...])) * dot(og_ref[...], wg_ref[...])
         + jax.nn.sigmoid(_mm_nt(h, g2_ref[...])) * dot(os_ref[...], ws_ref[...]))
    y_ref[...] = _bf(y)


def _gatemix(h, o_dn, o_gla, o_swa, wp, layer, w_dn, w_gla, w_swa, *, tm=1024, tj=512):
    t, d = h.shape
    nj = d // tj
    gate = lambda b: (lambda j, i: (layer, b * nj + j, 0))
    act = lambda width: pl.BlockSpec((tm, width), lambda j, i: (i, 0))
    wcol = lambda width: pl.BlockSpec((width, tj), lambda j, i: (0, j))
    return pl.pallas_call(
        _gatemix_kernel,
        grid=(nj, t // tm),
        in_specs=[
            act(d), act(DN_WIDTH), act(GLA_WIDTH), act(SWA_WIDTH),
            pl.BlockSpec((None, tj, d), gate(0)), pl.BlockSpec((None, tj, d), gate(1)),
            pl.BlockSpec((None, tj, d), gate(2)),
            wcol(DN_WIDTH), wcol(GLA_WIDTH), wcol(SWA_WIDTH),
        ],
        out_specs=pl.BlockSpec((tm, tj), lambda j, i: (i, j)),
        out_shape=jax.ShapeDtypeStruct((t, d), BF16),
        compiler_params=_cparams(2),
        name="gatemix",
    )(h, o_dn, o_gla, o_swa, wp, wp, wp, w_dn, w_gla, w_swa)


def _outproj_kernel(x_ref, y_ref, w_ref, o_ref):
    o_ref[...] = x_ref[...] + jnp.dot(y_ref[...], w_ref[...], preferred_element_type=F32)


def _outproj(x, y, w_o, *, tm=512):
    t, d = x.shape
    return pl.pallas_call(
        _outproj_kernel,
        grid=(t // tm,),
        in_specs=[
            pl.BlockSpec((tm, d), lambda i: (i, 0)),
            pl.BlockSpec((tm, d), lambda i: (i, 0)),
            pl.BlockSpec((d, d), lambda i: (0, 0)),
        ],
        out_specs=pl.BlockSpec((tm, d), lambda i: (i, 0)),
        out_shape=jax.ShapeDtypeStruct((t, d), F32),
        compiler_params=_cparams(1),
        name="outproj",
    )(x, y, w_o)


def _lane_row(vec):
    return jnp.pad(vec.astype(F32), (0, LANES - vec.shape[0]))[None, :]


def kernel(x, ffn1_norm, ffn1_w1, ffn1_w3, ffn1_w2, mix_norm, w_in, dn_conv, dn_a_log, dn_dt_bias, dn_out_norm, gla_gate_up, gla_gate_bias, gla_out_norm, swa_q_norm, swa_k_norm, swa_sinks, w_branch_dn, w_branch_gla, w_branch_swa, w_out, ffn2_norm, ffn2_w1, ffn2_w3, ffn2_w2):
    assert x.shape[0] == 1 and x.shape[2] == D_MODEL
    xs = x[0]
    wp = _relayout_w_in(jnp.swapaxes(w_in, 1, 2))
    for l in range(DEPTH):
        xs = _ffn(xs, ffn1_norm[l][None], ffn1_w1, ffn1_w3, ffn1_w2, l)
        u, h = _inproj(xs, mix_norm[l][None], wp, l)
        o_dn = _deltanet(u, dn_conv[l], _lane_row(dn_a_log[l]), _lane_row(dn_dt_bias[l]),
                         dn_out_norm[l][None])
        gate_up_pad = jnp.zeros((LANES, GLA_QK), BF16).at[SM_LR:SM_LR + GLA_RANK].set(
            _bf(gla_gate_up[l]))
        o_gla = _gla(u, gate_up_pad, gla_gate_bias[l][None], gla_out_norm[l][None])
        o_swa = _swa(u, jnp.tile(swa_q_norm[l], 2)[None], jnp.tile(swa_k_norm[l], 2)[None],
                     swa_sinks[l])
        y = _gatemix(h, o_dn, o_gla, o_swa, wp, l, _bf(w_branch_dn[l]), _bf(w_branch_gla[l]),
                     _bf(w_branch_swa[l]))
        xs = _outproj(xs, y, _bf(w_out[l]))
        xs = _ffn(xs, ffn2_norm[l][None], ffn2_w1, ffn2_w3, ffn2_w2, l)
    return xs[None]
```

```python
import functools

import numpy as np
import jax
import jax.numpy as jnp
from jax import lax
from jax.experimental import pallas as pl
from jax.experimental.pallas import tpu as pltpu

F32 = jnp.float32
BF16 = jnp.bfloat16

D_MODEL = 2048
DEPTH = 2
EPS = 1e-6
D_FF = 5632
FFN_RES_SCALE = 0.5
DN_HEADS, DN_DK, DN_DV, DN_CONV = 8, 128, 128, 4
GLA_HEADS, GLA_DK, GLA_DV, GLA_RANK, GLA_TAU = 4, 128, 256, 16, 16.0
SWA_HQ, SWA_HKV, SWA_HD, SWA_WINDOW = 16, 4, 64, 128
N_BRANCH = 3
CHUNK = 64
DN_QK = DN_HEADS * DN_DK
DN_WIDTH = DN_HEADS * DN_DV
GLA_QK = GLA_HEADS * GLA_DK
GLA_WIDTH = GLA_HEADS * GLA_DV
SWA_WIDTH = SWA_HQ * SWA_HD
SWA_KV = SWA_HKV * SWA_HD
SWA_SCALE = SWA_HD ** -0.5
assert SWA_SCALE == 2.0 ** -3

_SPLITS = (DN_QK, DN_QK, DN_WIDTH, DN_WIDTH, DN_HEADS, DN_HEADS,
           GLA_QK, GLA_QK, GLA_WIDTH, GLA_WIDTH, GLA_RANK,
           SWA_WIDTH, SWA_KV, SWA_KV, N_BRANCH * D_MODEL)
_OFFS = tuple(sum(_SPLITS[:i]) for i in range(len(_SPLITS) + 1))
(W_DNQ, W_DNK, W_DNV, W_DNZ, W_DNA, W_DNB, W_GLQ, W_GLK, W_GLV, W_GLR, W_GLLR,
 W_SWQ, W_SWK, W_SWV, W_GATE, W_END) = _OFFS

LANES = 128
SUBLANES = 8
PAIR = 2 * CHUNK

U_DN = 0
U_GLQ = U_DN + 4 * DN_QK
U_GLK = U_GLQ + GLA_QK
U_GLV = U_GLK + GLA_QK
U_GLR = U_GLV + GLA_WIDTH
U_SWQ = U_GLR + GLA_WIDTH
U_SWK = U_SWQ + SWA_WIDTH
U_SWV = U_SWK + SWA_KV
U_SMALL = U_SWV + SWA_KV
SM_A, SM_B, SM_LR = 0, DN_HEADS, 2 * DN_HEADS

RELAY_BW = 512
_RELAY_REGIONS = ((W_GATE, N_BRANCH * D_MODEL // RELAY_BW), (W_DNQ, 4 * DN_QK // RELAY_BW),
                  (W_GLQ, (2 * GLA_QK + 2 * GLA_WIDTH) // RELAY_BW),
                  (W_SWQ, (SWA_WIDTH + 2 * SWA_KV) // RELAY_BW))
RELAY_NBLK = sum(n for _, n in _RELAY_REGIONS) + 1
WP_GATE_COLS = N_BRANCH * D_MODEL
U_COLS = (RELAY_NBLK * RELAY_BW) - WP_GATE_COLS

VMEM_LIMIT = 60 * 1024 * 1024


def _cparams(n_axes):
    return pltpu.CompilerParams(dimension_semantics=("arbitrary",) * n_axes,
                                vmem_limit_bytes=VMEM_LIMIT)


def _bf(t):
    return t.astype(BF16)


def _mm(a, b):
    return jnp.dot(_bf(a), _bf(b), preferred_element_type=F32)


def _mm_nt(a, b):
    return lax.dot_general(_bf(a), _bf(b), (((1,), (1,)), ((), ())), preferred_element_type=F32)


def _mm_tn(a, b):
    return lax.dot_general(_bf(a), _bf(b), (((0,), (0,)), ((), ())), preferred_element_type=F32)


def _silu(t):
    return t * jax.nn.sigmoid(t)


def _rms_rows(x, gain):
    ms = jnp.mean(x * x, axis=-1, keepdims=True)
    return x * lax.rsqrt(ms + EPS) * gain


def _each(fn, *lists):
    return [fn(*args) for args in zip(*lists)]


def _chunk_cumsum(tri, tiles):
    hi = _each(_bf, tiles)
    r1 = _each(lambda t, h: t - h.astype(F32), tiles, hi)
    mid = _each(_bf, r1)
    lo = _each(lambda r, m: _bf(r - m.astype(F32)), r1, mid)
    dot = functools.partial(jnp.dot, tri, preferred_element_type=F32)
    return _each(lambda h, m, l: dot(h) + dot(m) + dot(l), hi, mid, lo)


def _pair_masks():
    sub = lax.broadcasted_iota(jnp.int32, (PAIR, PAIR), 0)
    lane = lax.broadcasted_iota(jnp.int32, (PAIR, PAIR), 1)
    same = (sub >= CHUNK) == (lane >= CHUNK)
    return sub, lane, same


def _ffn_kernel(x_ref, g_ref, w1_ref, w3_ref, w2_ref, o_ref, h_ref):
    @pl.when(pl.program_id(1) == 0)
    def _():
        x = x_ref[...]
        h_ref[...] = _bf(_rms_rows(x, g_ref[...]))
        o_ref[...] = x

    h = h_ref[...]
    a = jnp.dot(h, _bf(w1_ref[...]), preferred_element_type=F32)
    b = jnp.dot(h, _bf(w3_ref[...]), preferred_element_type=F32)
    act = _silu(a) * (b * FFN_RES_SCALE)
    o_ref[...] += jnp.dot(_bf(act), _bf(w2_ref[...]), preferred_element_type=F32)


def _ffn(x, gain, w1, w3, w2, layer, *, tm=1024, tf=512):
    t, d = x.shape
    f = w1.shape[2]
    return pl.pallas_call(
        _ffn_kernel,
        grid=(t // tm, f // tf),
        in_specs=[
            pl.BlockSpec((tm, d), lambda i, j: (i, 0), pipeline_mode=pl.Buffered(1)),
            pl.BlockSpec((1, d), lambda i, j: (0, 0)),
            pl.BlockSpec((None, d, tf), lambda i, j: (layer, 0, j)),
            pl.BlockSpec((None, d, tf), lambda i, j: (layer, 0, j)),
            pl.BlockSpec((None, tf, d), lambda i, j: (layer, j, 0)),
        ],
        out_specs=pl.BlockSpec((tm, d), lambda i, j: (i, 0)),
        out_shape=jax.ShapeDtypeStruct((t, d), F32),
        scratch_shapes=[pltpu.VMEM((tm, d), BF16)],
        compiler_params=_cparams(2),
        name="ffn",
    )(x, gain, w1, w3, w2)


def _inproj_kernel(x_ref, g_ref, w_ref, u_ref, h_ref):
    @pl.when(pl.program_id(1) == 0)
    def _():
        h_ref[...] = _bf(_rms_rows(x_ref[...], g_ref[...]))

    u_ref[...] = _mm_nt(h_ref[...], w_ref[...])


def _inproj(x, gain, wp, layer, *, tm=1024, tn=1536):
    t, d = x.shape
    assert WP_GATE_COLS % tn == 0 and U_COLS % tn == 0
    return pl.pallas_call(
        _inproj_kernel,
        grid=(t // tm, U_COLS // tn),
        in_specs=[
            pl.BlockSpec((tm, d), lambda i, j: (i, 0)),
            pl.BlockSpec((1, d), lambda i, j: (0, 0)),
            pl.BlockSpec((None, tn, d), lambda i, j: (layer, WP_GATE_COLS // tn + j, 0)),
        ],
        out_specs=[pl.BlockSpec((tm, tn), lambda i, j: (i, j)),
                   pl.BlockSpec((tm, d), lambda i, j: (i, 0))],
        out_shape=[jax.ShapeDtypeStruct((t, U_COLS), F32), jax.ShapeDtypeStruct((t, d), BF16)],
        compiler_params=_cparams(2),
        name="inproj",
    )(x, gain, wp)


def _relay_rows():
    rows = [src + k * RELAY_BW for src, nblk in _RELAY_REGIONS for k in range(nblk)]
    assert all(r % SUBLANES == 0 for r in rows) and len(rows) == RELAY_NBLK - 1
    return np.asarray(rows + [0], np.int32) // SUBLANES


def _relayout_kernel(tab_ref, src_ref, ab_ref, lr_ref, o_ref):
    last = pl.program_id(1) == RELAY_NBLK - 1

    @pl.when(jnp.logical_not(last))
    def _():
        o_ref[...] = _bf(src_ref[0])

    @pl.when(last)
    def _():
        n_ab, n_lr = ab_ref.shape[0], lr_ref.shape[0]
        o_ref[...] = jnp.zeros(o_ref.shape, o_ref.dtype)
        o_ref[SM_A:SM_A + n_ab, :] = _bf(ab_ref[...])
        o_ref[SM_LR:SM_LR + n_lr, :] = _bf(lr_ref[...])


def _relayout_w_in(w_t):
    n_layers, _, d = w_t.shape
    n_ab, n_lr = 2 * DN_HEADS, GLA_RANK
    assert W_DNA % n_ab == 0 and W_GLLR % n_lr == 0 and SM_LR == n_ab
    return pl.pallas_call(
        _relayout_kernel,
        grid_spec=pltpu.PrefetchScalarGridSpec(
            num_scalar_prefetch=1,
            grid=(n_layers, RELAY_NBLK),
            in_specs=[
                pl.BlockSpec((pl.Element(1), pl.Element(RELAY_BW), pl.Element(d)),
                             lambda l, j, tab: (l, tab[j] * SUBLANES, 0)),
                pl.BlockSpec((None, n_ab, d), lambda l, j, tab: (l, W_DNA // n_ab, 0)),
                pl.BlockSpec((None, n_lr, d), lambda l, j, tab: (l, W_GLLR // n_lr, 0)),
            ],
            out_specs=pl.BlockSpec((None, RELAY_BW, d), lambda l, j, tab: (l, j, 0)),
        ),
        out_shape=jax.ShapeDtypeStruct((n_layers, RELAY_NBLK * RELAY_BW, d), BF16),
        compiler_params=_cparams(2),
        name="relayout_w_in",
    )(jnp.asarray(_relay_rows()), w_t, w_t, w_t)


def _tri_inv(a, sub, lane, eye):
    bd16 = (sub >> 4) == (lane >> 4)
    bd32 = (sub >> 5) == (lane >> 5)
    off32 = bd32 & jnp.logical_not(bd16)
    a16 = _each(lambda t: _bf(jnp.where(bd16, t, 0.0)), a)
    x = _each(lambda t: eye - t.astype(F32), a16)
    p = a16
    for _ in range(3):
        p = _each(lambda t: _bf(_mm(t, t)), p)
        x = _each(lambda xt, pt: xt + _mm(xt, pt), x, p)
    for blk in (lambda t: jnp.where(off32, t, 0.0), lambda t: jnp.where(bd32, 0.0, t)):
        xb = _each(_bf, x)
        y = _each(lambda at, xt: _mm(blk(at), xt), a, xb)
        x = _each(lambda xt, xbt, yt: xt - _mm(xbt, yt), x, xb, y)
    return x


def _dn_kernel(q_ref, k_ref, v_ref, z_ref, pq_ref, pk_ref, pv_ref, sm_ref,
               cq_ref, ck_ref, cv_ref, alog_ref, dtb_ref, on_ref, o_ref,
               xs_ref, gc_s, gt_s, b_s, qe_s, ol_s, pm_s, qc_s, gl_s, o_s, s_ref,
               *, tb_rows, hs):
    tb = pl.program_id(0)
    hstep = pl.program_id(1)
    n_pairs = tb_rows // PAIR
    n_chunks = 2 * n_pairs
    heads = [hstep * hs + hh for hh in range(hs)]
    hcols = [slice(hh * LANES, (hh + 1) * LANES) for hh in range(hs)]

    @pl.when(tb == 0)
    def _():
        for head in heads:
            s_ref[head] = jnp.zeros((DN_DK, DN_DV), F32)

    sub, lane, same = _pair_masks()
    causal = same & (sub >= lane)
    strict = same & (sub > lane)
    eye = jnp.where(sub == lane, 1.0, 0.0).astype(F32)
    tri = jnp.where(causal, 1.0, 0.0).astype(BF16)
    first = sub[:, :1] < CHUNK
    rows = [slice(p * PAIR, (p + 1) * PAIR) for p in range(n_pairs)]
    halves = [slice(c * CHUNK, (c + 1) * CHUNK) for c in range(2)]

    @pl.when(hstep == 0)
    def _():
        sm = sm_ref[...]
        b_s[...] = jax.nn.sigmoid(sm)
        gs = -jnp.exp(alog_ref[...]) * jax.nn.softplus(sm + dtb_ref[...])
        gall = _chunk_cumsum(tri, [gs[r] for r in rows])
        for r, gt in zip(rows, gall):
            gc_s[r, :] = gt
            gt_s[r, :] = gt.T

    keep = (tb > 0).astype(F32)

    def conv_silu(x_ref, p_ref, w_ref):
        xs_ref[0:SUBLANES, :] = p_ref[...] * keep
        xs_ref[SUBLANES:, :] = x_ref[...]
        w = w_ref[...]
        y = w[DN_CONV - 1:DN_CONV, :] * xs_ref[pl.ds(SUBLANES, tb_rows), :]
        for kk in range(DN_CONV - 1):
            off = SUBLANES - (DN_CONV - 1) + kk
            y = y + w[kk:kk + 1, :] * xs_ref[pl.ds(off, tb_rows), :]
        return _silu(y)

    def l2norm(t):
        return t * lax.rsqrt(jnp.sum(t * t, axis=-1, keepdims=True) + EPS)

    qc = conv_silu(q_ref, pq_ref, cq_ref)
    kc = conv_silu(k_ref, pk_ref, ck_ref)
    vc = conv_silu(v_ref, pv_ref, cv_ref)
    qn = [l2norm(qc[:, c]) * (DN_DK ** -0.5) for c in hcols]
    kn = [l2norm(kc[:, c]) for c in hcols]

    tiles = [(hh, p) for hh in range(hs) for p in range(n_pairs)]
    q2 = [qn[hh][rows[p]] for hh, p in tiles]
    k2 = [kn[hh][rows[p]] for hh, p in tiles]
    v2 = [vc[rows[p], hcols[hh]] for hh, p in tiles]
    g = [jnp.sum(jnp.where(lane == heads[hh] + SM_A, gc_s[rows[p], :], 0.0), axis=1, keepdims=True)
         for hh, p in tiles]
    beta = [jnp.sum(jnp.where(lane == heads[hh] + SM_B, b_s[rows[p], :], 0.0), axis=1, keepdims=True)
            for hh, p in tiles]
    grow = [jnp.sum(jnp.where(sub == heads[hh] + SM_A, gt_s[rows[p], :], 0.0), axis=0, keepdims=True)
            for hh, p in tiles]
    decay = _each(lambda gc, gr: jnp.exp(jnp.where(causal, gc - gr, 0.0)), g, grow)
    eg = _each(jnp.exp, g)
    kb = _each(lambda kt, bt: kt * bt, k2, beta)
    k2b = _each(_bf, k2)
    a = _each(lambda kbt, kt, dt: jnp.where(strict, _mm_nt(kbt, kt) * dt, 0.0), kb, k2b, decay)
    at = _each(lambda qt, kt, dt: _bf(jnp.where(causal, _mm_nt(qt, kt) * dt, 0.0)), q2, k2b, decay)
    tm = _tri_inv(a, sub, lane, eye)
    rhs = _each(lambda vt, bt, kbt, egt: jnp.concatenate([vt * bt, kbt * egt], axis=1),
                v2, beta, kb, eg)
    sol = _each(_mm, tm, rhs)
    value = [_bf(t[:, :DN_DV]) for t in sol]
    kcum = [_bf(t[:, DN_DV:]) for t in sol]
    g_end = [[gt[CHUNK - 1:CHUNK, :], gt[PAIR - 1:PAIR, :]] for gt in g]
    kd = _each(lambda kt, gt, ge: _bf(kt * jnp.exp(jnp.where(first, ge[0], ge[1]) - gt)),
               k2, g, g_end)
    qe = _each(lambda qt, egt, att, kct: _bf(qt * egt - _mm(att, kct)), q2, eg, at, kcum)
    ol = _each(_mm, at, value)
    for t, (hh, p) in enumerate(tiles):
        qe_s[hh, rows[p], :] = qe[t]
        ol_s[hh, rows[p], :] = ol[t]
    for t, (hh, p) in enumerate(tiles):
        for c, sl in enumerate(halves):
            pm_s[hh * n_chunks + 2 * p + c] = _bf(-_mm_tn(kd[t][sl], kcum[t][sl]))
    for t, (hh, p) in enumerate(tiles):
        for c, sl in enumerate(halves):
            ci = hh * n_chunks + 2 * p + c
            qc_s[ci] = _mm_tn(kd[t][sl], value[t][sl])
            gl_s[ci:ci + 1, :] = jnp.broadcast_to(jnp.exp(g_end[t][c]), (1, LANES))

    def step(c, states):
        crow = pl.ds(pl.multiple_of(c * CHUNK, CHUNK), CHUNK)
        sb = [_bf(s) for s in states]
        for hh in range(hs):
            o_s[crow, hcols[hh]] = (jnp.dot(qe_s[hh, crow, :], sb[hh], preferred_element_type=F32)
                                    + ol_s[hh, crow, :])
        upd = [jnp.dot(pm_s[hh * n_chunks + c], sb[hh], preferred_element_type=F32)
               for hh in range(hs)]
        return tuple(gl_s[pl.ds(hh * n_chunks + c, 1), :] * states[hh] + upd[hh]
                     + qc_s[hh * n_chunks + c] for hh in range(hs))

    final = lax.fori_loop(0, n_chunks, step, tuple(s_ref[head] for head in heads), unroll=2)
    for head, s in zip(heads, final):
        s_ref[head] = s
    gain = on_ref[...]
    for c in hcols:
        o_ref[:, c] = _bf(_rms_rows(o_s[:, c], gain) * _silu(z_ref[:, c]))


def _deltanet(u, conv_w, alog_row, dtb_row, out_norm, *, tb_rows=256, hs=8):
    t = u.shape[0]
    width = hs * LANES
    nb = DN_QK // width
    prev_blk = tb_rows // SUBLANES
    assert DN_DK == LANES and DN_DV == LANES and DN_HEADS % hs == 0

    def col(seg):
        return lambda i, h: (i, U_DN // width + seg * nb + h)

    def prev(seg):
        return lambda i, h: (jnp.maximum(i * prev_blk - 1, 0), U_DN // width + seg * nb + h)

    def cw(seg):
        return lambda i, h: (0, seg * nb + h)

    row = pl.BlockSpec((1, LANES), lambda i, h: (0, 0))
    blk = lambda seg: pl.BlockSpec((tb_rows, width), col(seg))
    pblk = lambda seg: pl.BlockSpec((SUBLANES, width), prev(seg))
    cblk = lambda seg: pl.BlockSpec((DN_CONV, width), cw(seg))
    n_chunks = tb_rows // CHUNK
    rows_f32 = pltpu.VMEM((tb_rows, LANES), F32)
    return pl.pallas_call(
        functools.partial(_dn_kernel, tb_rows=tb_rows, hs=hs),
        grid=(t // tb_rows, DN_HEADS // hs),
        in_specs=[blk(0), blk(1), blk(2), blk(3), pblk(0), pblk(1), pblk(2),
                  pl.BlockSpec((tb_rows, LANES), lambda i, h: (i, U_SMALL // LANES)),
                  cblk(0), cblk(1), cblk(2), row, row, row],
        out_specs=pl.BlockSpec((tb_rows, width), lambda i, h: (i, h)),
        out_shape=jax.ShapeDtypeStruct((t, DN_WIDTH), BF16),
        scratch_shapes=[
            pltpu.VMEM((tb_rows + SUBLANES, width), F32),
            rows_f32, rows_f32, rows_f32,
            pltpu.VMEM((hs, tb_rows, DN_DK), BF16),
            pltpu.VMEM((hs, tb_rows, DN_DV), F32),
            pltpu.VMEM((hs * n_chunks, DN_DK, DN_DK), BF16),
            pltpu.VMEM((hs * n_chunks, DN_DK, DN_DV), F32),
            pltpu.VMEM((hs * n_chunks, LANES), F32),
            pltpu.VMEM((tb_rows, width), F32),
            pltpu.VMEM((DN_HEADS, DN_DK, DN_DV), F32),
        ],
        compiler_params=_cparams(2),
        name="deltanet",
    )(u, u, u, u, u, u, u, u, conv_w, conv_w, conv_w, alog_row, dtb_row, out_norm)


def _gla_kernel(q_ref, k_ref, v_ref, r_ref, sm_ref, gu_ref, gb_ref, on_ref, o_ref,
                qg_s, oi_s, kv_s, gl_s, st_ref, *, tb_rows):
    n_pairs = tb_rows // PAIR

    @pl.when(pl.program_id(1) == 0)
    def _():
        st_ref[...] = jnp.zeros_like(st_ref)

    logits = jnp.dot(_bf(sm_ref[...]), gu_ref[...], preferred_element_type=F32) + gb_ref[...]
    la = jax.nn.log_sigmoid(logits) * (1.0 / GLA_TAU)

    sub, lane, same = _pair_masks()
    causal = same & (sub >= lane)
    tri = jnp.where(causal, 1.0, 0.0).astype(BF16)
    first = sub < CHUNK
    gain = on_ref[...]

    rows = [slice(p * PAIR, (p + 1) * PAIR) for p in range(n_pairs)]
    halves = [slice(c * CHUNK, (c + 1) * CHUNK) for c in range(2)]
    g = _chunk_cumsum(tri, [la[r] for r in rows])
    k2 = [k_ref[r, :] for r in rows]
    v2 = [_bf(v_ref[r, :]) for r in rows]
    qg = [_bf(q_ref[r, :] * (GLA_DK ** -0.5) * jnp.exp(gt)) for r, gt in zip(rows, g)]
    kg = _each(lambda kt, gt: _bf(kt * jnp.exp(-gt)), k2, g)
    a = _each(lambda qt, kt: _bf(jnp.where(causal, _mm_nt(qt, kt), 0.0)), qg, kg)
    oi = _each(_mm, a, v2)
    g_end = [[gt[CHUNK - 1:CHUNK, :], gt[PAIR - 1:PAIR, :]] for gt in g]
    kd = _each(lambda kt, gt, ge: _bf(kt * jnp.exp(jnp.where(first, ge[0], ge[1]) - gt)),
               k2, g, g_end)
    for p, r in enumerate(rows):
        qg_s[r, :] = qg[p]
        oi_s[r, :] = oi[p]
        for c, sl in enumerate(halves):
            ci = 2 * p + c
            kv_s[ci] = _mm_tn(v2[p][sl], kd[p][sl])
            gl_s[ci:ci + 1, :] = jnp.exp(g_end[p][c])

    def step(c, st):
        rws = pl.ds(pl.multiple_of(c * CHUNK, CHUNK), CHUNK)
        o_c = _mm_nt(qg_s[rws, :], st) + oi_s[rws, :]
        o_ref[rws, :] = _bf(_rms_rows(o_c, gain) * _silu(r_ref[rws, :]))
        return st * gl_s[pl.ds(c, 1), :] + kv_s[c]

    st_ref[...] = lax.fori_loop(0, 2 * n_pairs, step, st_ref[...], unroll=4)


def _gla(u, gate_up_pad, gate_bias, out_norm, *, tb_rows=1024):
    t = u.shape[0]
    return pl.pallas_call(
        functools.partial(_gla_kernel, tb_rows=tb_rows),
        grid=(GLA_HEADS, t // tb_rows),
        in_specs=[
            pl.BlockSpec((tb_rows, GLA_DK), lambda h, i: (i, U_GLQ // GLA_DK + h)),
            pl.BlockSpec((tb_rows, GLA_DK), lambda h, i: (i, U_GLK // GLA_DK + h)),
            pl.BlockSpec((tb_rows, GLA_DV), lambda h, i: (i, U_GLV // GLA_DV + h)),
            pl.BlockSpec((tb_rows, GLA_DV), lambda h, i: (i, U_GLR // GLA_DV + h)),
            pl.BlockSpec((tb_rows, LANES), lambda h, i: (i, U_SMALL // LANES)),
            pl.BlockSpec((LANES, GLA_DK), lambda h, i: (0, h)),
            pl.BlockSpec((1, GLA_DK), lambda h, i: (0, h)),
            pl.BlockSpec((1, GLA_DV), lambda h, i: (0, 0)),
        ],
        out_specs=pl.BlockSpec((tb_rows, GLA_DV), lambda h, i: (i, h)),
        out_shape=jax.ShapeDtypeStruct((t, GLA_WIDTH), BF16),
        scratch_shapes=[
            pltpu.VMEM((tb_rows, GLA_DK), BF16),
            pltpu.VMEM((tb_rows, GLA_DV), F32),
            pltpu.VMEM((tb_rows // CHUNK, GLA_DV, GLA_DK), F32),
            pltpu.VMEM((tb_rows // CHUNK, GLA_DK), F32),
            pltpu.VMEM((GLA_DV, GLA_DK), F32),
        ],
        compiler_params=_cparams(2),
        name="gla",
    )(u, u, u, u, u, gate_up_pad, gate_bias, out_norm)


def _swa_kernel(q_ref, kc_ref, kp_ref, vc_ref, vp_ref, qg_ref, kg_ref, sink_ref, o_ref):
    n = pl.program_id(0)
    w = SWA_WINDOW
    lane = lax.broadcasted_iota(jnp.int32, (w, LANES), 1)
    lo = lane < SWA_HD
    lane2 = lax.broadcasted_iota(jnp.int32, (2 * w, LANES), 1)
    lo2 = lane2 < SWA_HD
    qi = lax.broadcasted_iota(jnp.int32, (w, 2 * w), 0)
    ki = lax.broadcasted_iota(jnp.int32, (w, 2 * w), 1)
    mask = (ki <= qi + w) & (ki > qi + w - SWA_WINDOW) & ((n > 0) | (ki >= w))
    kk = jnp.concatenate([kp_ref[...], kc_ref[...]], axis=0)
    vv = jnp.concatenate([vp_ref[...], vc_ref[...]], axis=0)
    qgain = qg_ref[...]
    kgain = kg_ref[...]
    heads_per_grp = SWA_HQ // SWA_HKV
    pcols = [slice(p * LANES, (p + 1) * LANES) for p in range(SWA_HQ // 2)]

    def head_halves(t, g):
        tile = t[:, (g // 2) * LANES:(g // 2 + 1) * LANES]
        swapped = pltpu.roll(tile, SWA_HD, axis=1)
        return (tile, swapped) if g % 2 == 0 else (swapped, tile)

    def norm_k(kg):
        ms = jnp.sum(kg * kg, axis=-1, keepdims=True) * (1.0 / LANES)
        return _bf(kg * lax.rsqrt(ms + EPS) * kgain)

    def norm_q(qp):
        sq = qp * qp
        ms_lo = jnp.sum(jnp.where(lo, sq, 0.0), axis=-1, keepdims=True) * (1.0 / SWA_HD)
        ms_hi = jnp.sum(jnp.where(lo, 0.0, sq), axis=-1, keepdims=True) * (1.0 / SWA_HD)
        qn = qp * jnp.where(lo, lax.rsqrt(ms_lo + EPS), lax.rsqrt(ms_hi + EPS)) * qgain
        return qn * SWA_SCALE

    def probs(s, sink):
        s = jnp.where(mask, s, -jnp.inf)
        m = jnp.maximum(jnp.max(s, axis=-1, keepdims=True), sink)
        p = jnp.exp(s - m)
        inv = 1.0 / (jnp.sum(p, axis=-1, keepdims=True) + jnp.exp(sink - m))
        return _bf(p * inv)

    groups = range(SWA_HKV)
    k_lo_hi = [head_halves(kk, g) for g in groups]
    kn = [norm_k(jnp.where(lo2, k_lo, k_hi)) for k_lo, k_hi in k_lo_hi]
    v_lo_hi = [head_halves(vv, g) for g in groups]
    v_half = [(_bf(jnp.where(lo2, v_lo, 0.0)), _bf(jnp.where(lo2, 0.0, v_hi)))
              for v_lo, v_hi in v_lo_hi]
    qn = [norm_q(q_ref[:, c]) for c in pcols]
    heads = range(SWA_HQ)
    qm = [_bf(jnp.where(lo, qn[h // 2], 0.0) if h % 2 == 0 else jnp.where(lo, 0.0, qn[h // 2]))
          for h in heads]
    s = [_mm_nt(qm[h], kn[h // heads_per_grp]) for h in heads]
    p = [probs(s[h], sink_ref[h]) for h in heads]
    part = [jnp.dot(p[h], v_half[h // heads_per_grp][h % 2], preferred_element_type=F32)
            for h in heads]
    for pr, c in enumerate(pcols):
        o_ref[:, c] = _bf(part[2 * pr] + part[2 * pr + 1])


def _swa(u, qgain, kgain, sinks):
    t = u.shape[0]
    w = SWA_WINDOW
    kvw = SWA_KV
    assert U_SWK % kvw == 0 and U_SWV % kvw == 0 and U_SWQ % SWA_WIDTH == 0
    cur = lambda c: (lambda n: (n, c))
    prv = lambda c: (lambda n: (jnp.maximum(n - 1, 0), c))
    row = pl.BlockSpec((1, LANES), lambda n: (0, 0))
    return pl.pallas_call(
        _swa_kernel,
        grid=(t // w,),
        in_specs=[
            pl.BlockSpec((w, SWA_WIDTH), cur(U_SWQ // SWA_WIDTH)),
            pl.BlockSpec((w, kvw), cur(U_SWK // kvw)),
            pl.BlockSpec((w, kvw), prv(U_SWK // kvw)),
            pl.BlockSpec((w, kvw), cur(U_SWV // kvw)),
            pl.BlockSpec((w, kvw), prv(U_SWV // kvw)),
            row, row,
            pl.BlockSpec(memory_space=pltpu.SMEM),
        ],
        out_specs=pl.BlockSpec((w, SWA_WIDTH), lambda n: (n, 0)),
        out_shape=jax.ShapeDtypeStruct((t, SWA_WIDTH), BF16),
        compiler_params=_cparams(1),
        name="swa",
    )(u, u, u, u, u, qgain, kgain, sinks)


def _gatemix_kernel(h_ref, od_ref, og_ref, os_ref, g0_ref, g1_ref, g2_ref,
                    wd_ref, wg_ref, ws_ref, y_ref):
    dot = functools.partial(jnp.dot, preferred_element_type=F32)
    h = h_ref[...]
    y = (jax.nn.sigmoid(_mm_nt(h, g0_ref[...])) * dot(od_ref[...], wd_ref[...])
         + jax.nn.sigmoid(_mm_nt(h, g1_ref[...])) * dot(og_ref[...], wg_ref[...])
         + jax.nn.sigmoid(_mm_nt(h, g2_ref[...])) * dot(os_ref[...], ws_ref[...]))
    y_ref[...] = _bf(y)


def _gatemix(h, o_dn, o_gla, o_swa, wp, layer, w_dn, w_gla, w_swa, *, tm=1024, tj=512):
    t, d = h.shape
    nj = d // tj
    gate = lambda b: (lambda j, i: (layer, b * nj + j, 0))
    act = lambda width: pl.BlockSpec((tm, width), lambda j, i: (i, 0))
    wcol = lambda width: pl.BlockSpec((width, tj), lambda j, i: (0, j))
    return pl.pallas_call(
        _gatemix_kernel,
        grid=(nj, t // tm),
        in_specs=[
            act(d), act(DN_WIDTH), act(GLA_WIDTH), act(SWA_WIDTH),
            pl.BlockSpec((None, tj, d), gate(0)), pl.BlockSpec((None, tj, d), gate(1)),
            pl.BlockSpec((None, tj, d), gate(2)),
            wcol(DN_WIDTH), wcol(GLA_WIDTH), wcol(SWA_WIDTH),
        ],
        out_specs=pl.BlockSpec((tm, tj), lambda j, i: (i, j)),
        out_shape=jax.ShapeDtypeStruct((t, d), BF16),
        compiler_params=_cparams(2),
        name="gatemix",
    )(h, o_dn, o_gla, o_swa, wp, wp, wp, w_dn, w_gla, w_swa)


def _outproj_kernel(x_ref, y_ref, w_ref, o_ref):
    o_ref[...] = x_ref[...] + jnp.dot(y_ref[...], w_ref[...], preferred_element_type=F32)


def _outproj(x, y, w_o, *, tm=512):
    t, d = x.shape
    return pl.pallas_call(
        _outproj_kernel,
        grid=(t // tm,),
        in_specs=[
            pl.BlockSpec((tm, d), lambda i: (i, 0)),
            pl.BlockSpec((tm, d), lambda i: (i, 0)),
            pl.BlockSpec((d, d), lambda i: (0, 0)),
        ],
        out_specs=pl.BlockSpec((tm, d), lambda i: (i, 0)),
        out_shape=jax.ShapeDtypeStruct((t, d), F32),
        compiler_params=_cparams(1),
        name="outproj",
    )(x, y, w_o)


def _lane_row(vec):
    return jnp.pad(vec.astype(F32), (0, LANES - vec.shape[0]))[None, :]


def kernel(x, ffn1_norm, ffn1_w1, ffn1_w3, ffn1_w2, mix_norm, w_in, dn_conv, dn_a_log, dn_dt_bias, dn_out_norm, gla_gate_up, gla_gate_bias, gla_out_norm, swa_q_norm, swa_k_norm, swa_sinks, w_branch_dn, w_branch_gla, w_branch_swa, w_out, ffn2_norm, ffn2_w1, ffn2_w3, ffn2_w2):
    assert x.shape[0] == 1 and x.shape[2] == D_MODEL
    xs = x[0]
    wp = _relayout_w_in(jnp.swapaxes(w_in, 1, 2))
    for l in range(DEPTH):
        xs = _ffn(xs, ffn1_norm[l][None], ffn1_w1, ffn1_w3, ffn1_w2, l)
        u, h = _inproj(xs, mix_norm[l][None], wp, l)
        o_dn = _deltanet(u, dn_conv[l], _lane_row(dn_a_log[l]), _lane_row(dn_dt_bias[l]),
                         dn_out_norm[l][None])
        gate_up_pad = jnp.zeros((LANES, GLA_QK), BF16).at[SM_LR:SM_LR + GLA_RANK].set(
            _bf(gla_gate_up[l]))
        o_gla = _gla(u, gate_up_pad, gla_gate_bias[l][None], gla_out_norm[l][None])
        o_swa = _swa(u, jnp.tile(swa_q_norm[l], 2)[None], jnp.tile(swa_k_norm[l], 2)[None],
                     swa_sinks[l])
        y = _gatemix(h, o_dn, o_gla, o_swa, wp, l, _bf(w_branch_dn[l]), _bf(w_branch_gla[l]),
                     _bf(w_branch_swa[l]))
        xs = _outproj(xs, y, _bf(w_out[l]))
        xs = _ffn(xs, ffn2_norm[l][None], ffn2_w1, ffn2_w3, ffn2_w2, l)
    return xs[None]
```

```python
import functools

import numpy as np
import jax
import jax.numpy as jnp
from jax import lax
from jax.experimental import pallas as pl
from jax.experimental.pallas import tpu as pltpu

F32 = jnp.float32
BF16 = jnp.bfloat16

D_MODEL = 2048
DEPTH = 2
EPS = 1e-6
D_FF = 5632
FFN_RES_SCALE = 0.5
DN_HEADS, DN_DK, DN_DV, DN_CONV = 8, 128, 128, 4
GLA_HEADS, GLA_DK, GLA_DV, GLA_RANK, GLA_TAU = 4, 128, 256, 16, 16.0
SWA_HQ, SWA_HKV, SWA_HD, SWA_WINDOW = 16, 4, 64, 128
N_BRANCH = 3
CHUNK = 64
DN_QK = DN_HEADS * DN_DK
DN_WIDTH = DN_HEADS * DN_DV
GLA_QK = GLA_HEADS * GLA_DK
GLA_WIDTH = GLA_HEADS * GLA_DV
SWA_WIDTH = SWA_HQ * SWA_HD
SWA_KV = SWA_HKV * SWA_HD
SWA_SCALE = SWA_HD ** -0.5
assert SWA_SCALE == 2.0 ** -3

_SPLITS = (DN_QK, DN_QK, DN_WIDTH, DN_WIDTH, DN_HEADS, DN_HEADS,
           GLA_QK, GLA_QK, GLA_WIDTH, GLA_WIDTH, GLA_RANK,
           SWA_WIDTH, SWA_KV, SWA_KV, N_BRANCH * D_MODEL)
_OFFS = tuple(sum(_SPLITS[:i]) for i in range(len(_SPLITS) + 1))
(W_DNQ, W_DNK, W_DNV, W_DNZ, W_DNA, W_DNB, W_GLQ, W_GLK, W_GLV, W_GLR, W_GLLR,
 W_SWQ, W_SWK, W_SWV, W_GATE, W_END) = _OFFS

LANES = 128
SUBLANES = 8
PAIR = 2 * CHUNK

U_DN = 0
U_GLQ = U_DN + 4 * DN_QK
U_GLK = U_GLQ + GLA_QK
U_GLV = U_GLK + GLA_QK
U_GLR = U_GLV + GLA_WIDTH
U_SWQ = U_GLR + GLA_WIDTH
U_SWK = U_SWQ + SWA_WIDTH
U_SWV = U_SWK + SWA_KV
U_SMALL = U_SWV + SWA_KV
SM_A, SM_B, SM_LR = 0, DN_HEADS, 2 * DN_HEADS

RELAY_BW = 512
_RELAY_REGIONS = ((W_GATE, N_BRANCH * D_MODEL // RELAY_BW), (W_DNQ, 4 * DN_QK // RELAY_BW),
                  (W_GLQ, (2 * GLA_QK + 2 * GLA_WIDTH) // RELAY_BW),
                  (W_SWQ, (SWA_WIDTH + 2 * SWA_KV) // RELAY_BW))
RELAY_NBLK = sum(n for _, n in _RELAY_REGIONS) + 1
WP_GATE_COLS = N_BRANCH * D_MODEL
U_COLS = (RELAY_NBLK * RELAY_BW) - WP_GATE_COLS

VMEM_LIMIT = 60 * 1024 * 1024


def _cparams(n_axes):
    return pltpu.CompilerParams(dimension_semantics=("arbitrary",) * n_axes,
                                vmem_limit_bytes=VMEM_LIMIT)


def _bf(t):
    return t.astype(BF16)


def _mm(a, b):
    return jnp.dot(_bf(a), _bf(b), preferred_element_type=F32)


def _mm_nt(a, b):
    return lax.dot_general(_bf(a), _bf(b), (((1,), (1,)), ((), ())), preferred_element_type=F32)


def _mm_tn(a, b):
    return lax.dot_general(_bf(a), _bf(b), (((0,), (0,)), ((), ())), preferred_element_type=F32)


def _silu(t):
    return t * jax.nn.sigmoid(t)


def _rms_rows(x, gain):
    ms = jnp.mean(x * x, axis=-1, keepdims=True)
    return x * lax.rsqrt(ms + EPS) * gain


def _each(fn, *lists):
    return [fn(*args) for args in zip(*lists)]


def _chunk_cumsum(tri, tiles):
    hi = _each(_bf, tiles)
    r1 = _each(lambda t, h: t - h.astype(F32), tiles, hi)
    mid = _each(_bf, r1)
    lo = _each(lambda r, m: _bf(r - m.astype(F32)), r1, mid)
    dot = functools.partial(jnp.dot, tri, preferred_element_type=F32)
    return _each(lambda h, m, l: dot(h) + dot(m) + dot(l), hi, mid, lo)


def _pair_masks():
    sub = lax.broadcasted_iota(jnp.int32, (PAIR, PAIR), 0)
    lane = lax.broadcasted_iota(jnp.int32, (PAIR, PAIR), 1)
    same = (sub >= CHUNK) == (lane >= CHUNK)
    return sub, lane, same


def _ffn_kernel(x_ref, g_ref, w1_ref, w3_ref, w2_ref, o_ref, h_ref):
    @pl.when(pl.program_id(1) == 0)
    def _():
        x = x_ref[...]
        h_ref[...] = _bf(_rms_rows(x, g_ref[...]))
        o_ref[...] = x

    h = h_ref[...]
    a = jnp.dot(h, _bf(w1_ref[...]), preferred_element_type=F32)
    b = jnp.dot(h, _bf(w3_ref[...]), preferred_element_type=F32)
    act = _silu(a) * (b * FFN_RES_SCALE)
    o_ref[...] += jnp.dot(_bf(act), _bf(w2_ref[...]), preferred_element_type=F32)


def _ffn(x, gain, w1, w3, w2, layer, *, tm=1024, tf=512):
    t, d = x.shape
    f = w1.shape[2]
    return pl.pallas_call(
        _ffn_kernel,
        grid=(t // tm, f // tf),
        in_specs=[
            pl.BlockSpec((tm, d), lambda i, j: (i, 0), pipeline_mode=pl.Buffered(1)),
            pl.BlockSpec((1, d), lambda i, j: (0, 0)),
            pl.BlockSpec((None, d, tf), lambda i, j: (layer, 0, j)),
            pl.BlockSpec((None, d, tf), lambda i, j: (layer, 0, j)),
            pl.BlockSpec((None, tf, d), lambda i, j: (layer, j, 0)),
        ],
        out_specs=pl.BlockSpec((tm, d), lambda i, j: (i, 0)),
        out_shape=jax.ShapeDtypeStruct((t, d), F32),
        scratch_shapes=[pltpu.VMEM((tm, d), BF16)],
        compiler_params=_cparams(2),
        name="ffn",
    )(x, gain, w1, w3, w2)


def _inproj_kernel(x_ref, g_ref, w_ref, u_ref, h_ref):
    @pl.when(pl.program_id(1) == 0)
    def _():
        h_ref[...] = _bf(_rms_rows(x_ref[...], g_ref[...]))

    u_ref[...] = _mm_nt(h_ref[...], w_ref[...])


def _inproj(x, gain, wp, layer, *, tm=1024, tn=1536):
    t, d = x.shape
    assert WP_GATE_COLS % tn == 0 and U_COLS % tn == 0
    return pl.pallas_call(
        _inproj_kernel,
        grid=(t // tm, U_COLS // tn),
        in_specs=[
            pl.BlockSpec((tm, d), lambda i, j: (i, 0)),
            pl.BlockSpec((1, d), lambda i, j: (0, 0)),
            pl.BlockSpec((None, tn, d), lambda i, j: (layer, WP_GATE_COLS // tn + j, 0)),
        ],
        out_specs=[pl.BlockSpec((tm, tn), lambda i, j: (i, j)),
                   pl.BlockSpec((tm, d), lambda i, j: (i, 0))],
        out_shape=[jax.ShapeDtypeStruct((t, U_COLS), F32), jax.ShapeDtypeStruct((t, d), BF16)],
        compiler_params=_cparams(2),
        name="inproj",
    )(x, gain, wp)


def _relay_rows():
    rows = [src + k * RELAY_BW for src, nblk in _RELAY_REGIONS for k in range(nblk)]
    assert all(r % SUBLANES == 0 for r in rows) and len(rows) == RELAY_NBLK - 1
    return np.asarray(rows + [0], np.int32) // SUBLANES


def _relayout_kernel(tab_ref, src_ref, ab_ref, lr_ref, o_ref):
    last = pl.program_id(1) == RELAY_NBLK - 1

    @pl.when(jnp.logical_not(last))
    def _():
        o_ref[...] = _bf(src_ref[0])

    @pl.when(last)
    def _():
        n_ab, n_lr = ab_ref.shape[0], lr_ref.shape[0]
        o_ref[...] = jnp.zeros(o_ref.shape, o_ref.dtype)
        o_ref[SM_A:SM_A + n_ab, :] = _bf(ab_ref[...])
        o_ref[SM_LR:SM_LR + n_lr, :] = _bf(lr_ref[...])


def _relayout_w_in(w_t):
    n_layers, _, d = w_t.shape
    n_ab, n_lr = 2 * DN_HEADS, GLA_RANK
    assert W_DNA % n_ab == 0 and W_GLLR % n_lr == 0 and SM_LR == n_ab
    return pl.pallas_call(
        _relayout_kernel,
        grid_spec=pltpu.PrefetchScalarGridSpec(
            num_scalar_prefetch=1,
            grid=(n_layers, RELAY_NBLK),
            in_specs=[
                pl.BlockSpec((pl.Element(1), pl.Element(RELAY_BW), pl.Element(d)),
                             lambda l, j, tab: (l, tab[j] * SUBLANES, 0)),
                pl.BlockSpec((None, n_ab, d), lambda l, j, tab: (l, W_DNA // n_ab, 0)),
                pl.BlockSpec((None, n_lr, d), lambda l, j, tab: (l, W_GLLR // n_lr, 0)),
            ],
            out_specs=pl.BlockSpec((None, RELAY_BW, d), lambda l, j, tab: (l, j, 0)),
        ),
        out_shape=jax.ShapeDtypeStruct((n_layers, RELAY_NBLK * RELAY_BW, d), BF16),
        compiler_params=_cparams(2),
        name="relayout_w_in",
    )(jnp.asarray(_relay_rows()), w_t, w_t, w_t)


def _tri_inv(a, sub, lane, eye):
    bd16 = (sub >> 4) == (lane >> 4)
    bd32 = (sub >> 5) == (lane >> 5)
    off32 = bd32 & jnp.logical_not(bd16)
    a16 = _each(lambda t: _bf(jnp.where(bd16, t, 0.0)), a)
    x = _each(lambda t: eye - t.astype(F32), a16)
    p = a16
    for _ in range(3):
        p = _each(lambda t: _bf(_mm(t, t)), p)
        x = _each(lambda xt, pt: xt + _mm(xt, pt), x, p)
    for blk in (lambda t: jnp.where(off32, t, 0.0), lambda t: jnp.where(bd32, 0.0, t)):
        xb = _each(_bf, x)
        y = _each(lambda at, xt: _mm(blk(at), xt), a, xb)
        x = _each(lambda xt, xbt, yt: xt - _mm(xbt, yt), x, xb, y)
    return x


def _dn_kernel(q_ref, k_ref, v_ref, z_ref, pq_ref, pk_ref, pv_ref, sm_ref,
               cq_ref, ck_ref, cv_ref, alog_ref, dtb_ref, on_ref, o_ref,
               xs_ref, gc_s, gt_s, b_s, qe_s, ol_s, pm_s, qc_s, gl_s, o_s, s_ref,
               *, tb_rows, hs):
    tb = pl.program_id(0)
    hstep = pl.program_id(1)
    n_pairs = tb_rows // PAIR
    n_chunks = 2 * n_pairs
    heads = [hstep * hs + hh for hh in range(hs)]
    hcols = [slice(hh * LANES, (hh + 1) * LANES) for hh in range(hs)]

    @pl.when(tb == 0)
    def _():
        for head in heads:
            s_ref[head] = jnp.zeros((DN_DK, DN_DV), F32)

    sub, lane, same = _pair_masks()
    causal = same & (sub >= lane)
    strict = same & (sub > lane)
    eye = jnp.where(sub == lane, 1.0, 0.0).astype(F32)
    tri = jnp.where(causal, 1.0, 0.0).astype(BF16)
    first = sub[:, :1] < CHUNK
    rows = [slice(p * PAIR, (p + 1) * PAIR) for p in range(n_pairs)]
    halves = [slice(c * CHUNK, (c + 1) * CHUNK) for c in range(2)]

    @pl.when(hstep == 0)
    def _():
        sm = sm_ref[...]
        b_s[...] = jax.nn.sigmoid(sm)
        gs = -jnp.exp(alog_ref[...]) * jax.nn.softplus(sm + dtb_ref[...])
        gall = _chunk_cumsum(tri, [gs[r] for r in rows])
        for r, gt in zip(rows, gall):
            gc_s[r, :] = gt
            gt_s[r, :] = gt.T

    keep = (tb > 0).astype(F32)

    def conv_silu(x_ref, p_ref, w_ref):
        xs_ref[0:SUBLANES, :] = p_ref[...] * keep
        xs_ref[SUBLANES:, :] = x_ref[...]
        w = w_ref[...]
        y = w[DN_CONV - 1:DN_CONV, :] * xs_ref[pl.ds(SUBLANES, tb_rows), :]
        for kk in range(DN_CONV - 1):
            off = SUBLANES - (DN_CONV - 1) + kk
            y = y + w[kk:kk + 1, :] * xs_ref[pl.ds(off, tb_rows), :]
        return _silu(y)

    def l2norm(t):
        return t * lax.rsqrt(jnp.sum(t * t, axis=-1, keepdims=True) + EPS)

    qc = conv_silu(q_ref, pq_ref, cq_ref)
    kc = conv_silu(k_ref, pk_ref, ck_ref)
    vc = conv_silu(v_ref, pv_ref, cv_ref)
    qn = [l2norm(qc[:, c]) * (DN_DK ** -0.5) for c in hcols]
    kn = [l2norm(kc[:, c]) for c in hcols]

    tiles = [(hh, p) for hh in range(hs) for p in range(n_pairs)]
    q2 = [qn[hh][rows[p]] for hh, p in tiles]
    k2 = [kn[hh][rows[p]] for hh, p in tiles]
    v2 = [vc[rows[p], hcols[hh]] for hh, p in tiles]
    g = [jnp.sum(jnp.where(lane == heads[hh] + SM_A, gc_s[rows[p], :], 0.0), axis=1, keepdims=True)
         for hh, p in tiles]
    beta = [jnp.sum(jnp.where(lane == heads[hh] + SM_B, b_s[rows[p], :], 0.0), axis=1, keepdims=True)
            for hh, p in tiles]
    grow = [jnp.sum(jnp.where(sub == heads[hh] + SM_A, gt_s[rows[p], :], 0.0), axis=0, keepdims=True)
            for hh, p in tiles]
    decay = _each(lambda gc, gr: jnp.exp(jnp.where(causal, gc - gr, 0.0)), g, grow)
    eg = _each(jnp.exp, g)
    kb = _each(lambda kt, bt: kt * bt, k2, beta)
    k2b = _each(_bf, k2)
    a = _each(lambda kbt, kt, dt: jnp.where(strict, _mm_nt(kbt, kt) * dt, 0.0), kb, k2b, decay)
    at = _each(lambda qt, kt, dt: _bf(jnp.where(causal, _mm_nt(qt, kt) * dt, 0.0)), q2, k2b, decay)
    tm = _tri_inv(a, sub, lane, eye)
    rhs = _each(lambda vt, bt, kbt, egt: jnp.concatenate([vt * bt, kbt * egt], axis=1),
                v2, beta, kb, eg)
    sol = _each(_mm, tm, rhs)
    value = [_bf(t[:, :DN_DV]) for t in sol]
    kcum = [_bf(t[:, DN_DV:]) for t in sol]
    g_end = [[gt[CHUNK - 1:CHUNK, :], gt[PAIR - 1:PAIR, :]] for gt in g]
    kd = _each(lambda kt, gt, ge: _bf(kt * jnp.exp(jnp.where(first, ge[0], ge[1]) - gt)),
               k2, g, g_end)
    qe = _each(lambda qt, egt, att, kct: _bf(qt * egt - _mm(att, kct)), q2, eg, at, kcum)
    ol = _each(_mm, at, value)
    for t, (hh, p) in enumerate(tiles):
        qe_s[hh, rows[p], :] = qe[t]
        ol_s[hh, rows[p], :] = ol[t]
    for t, (hh, p) in enumerate(tiles):
        for c, sl in enumerate(halves):
            pm_s[hh * n_chunks + 2 * p + c] = _bf(-_mm_tn(kd[t][sl], kcum[t][sl]))
    for t, (hh, p) in enumerate(tiles):
        for c, sl in enumerate(halves):
            ci = hh * n_chunks + 2 * p + c
            qc_s[ci] = _mm_tn(kd[t][sl], value[t][sl])
            gl_s[ci:ci + 1, :] = jnp.broadcast_to(jnp.exp(g_end[t][c]), (1, LANES))

    def step(c, states):
        crow = pl.ds(pl.multiple_of(c * CHUNK, CHUNK), CHUNK)
        sb = [_bf(s) for s in states]
        for hh in range(hs):
            o_s[crow, hcols[hh]] = (jnp.dot(qe_s[hh, crow, :], sb[hh], preferred_element_type=F32)
                                    + ol_s[hh, crow, :])
        upd = [jnp.dot(pm_s[hh * n_chunks + c], sb[hh], preferred_element_type=F32)
               for hh in range(hs)]
        return tuple(gl_s[pl.ds(hh * n_chunks + c, 1), :] * states[hh] + upd[hh]
                     + qc_s[hh * n_chunks + c] for hh in range(hs))

    final = lax.fori_loop(0, n_chunks, step, tuple(s_ref[head] for head in heads), unroll=2)
    for head, s in zip(heads, final):
        s_ref[head] = s
    gain = on_ref[...]
    for c in hcols:
        o_ref[:, c] = _bf(_rms_rows(o_s[:, c], gain) * _silu(z_ref[:, c]))


def _deltanet(u, conv_w, alog_row, dtb_row, out_norm, *, tb_rows=256, hs=8):
    t = u.shape[0]
    width = hs * LANES
    nb = DN_QK // width
    prev_blk = tb_rows // SUBLANES
    assert DN_DK == LANES and DN_DV == LANES and DN_HEADS % hs == 0

    def col(seg):
        return lambda i, h: (i, U_DN // width + seg * nb + h)

    def prev(seg):
        return lambda i, h: (jnp.maximum(i * prev_blk - 1, 0), U_DN // width + seg * nb + h)

    def cw(seg):
        return lambda i, h: (0, seg * nb + h)

    row = pl.BlockSpec((1, LANES), lambda i, h: (0, 0))
    blk = lambda seg: pl.BlockSpec((tb_rows, width), col(seg))
    pblk = lambda seg: pl.BlockSpec((SUBLANES, width), prev(seg))
    cblk = lambda seg: pl.BlockSpec((DN_CONV, width), cw(seg))
    n_chunks = tb_rows // CHUNK
    rows_f32 = pltpu.VMEM((tb_rows, LANES), F32)
    return pl.pallas_call(
        functools.partial(_dn_kernel, tb_rows=tb_rows, hs=hs),
        grid=(t // tb_rows, DN_HEADS // hs),
        in_specs=[blk(0), blk(1), blk(2), blk(3), pblk(0), pblk(1), pblk(2),
                  pl.BlockSpec((tb_rows, LANES), lambda i, h: (i, U_SMALL // LANES)),
                  cblk(0), cblk(1), cblk(2), row, row, row],
        out_specs=pl.BlockSpec((tb_rows, width), lambda i, h: (i, h)),
        out_shape=jax.ShapeDtypeStruct((t, DN_WIDTH), BF16),
        scratch_shapes=[
            pltpu.VMEM((tb_rows + SUBLANES, width), F32),
            rows_f32, rows_f32, rows_f32,
            pltpu.VMEM((hs, tb_rows, DN_DK), BF16),
            pltpu.VMEM((hs, tb_rows, DN_DV), F32),
            pltpu.VMEM((hs * n_chunks, DN_DK, DN_DK), BF16),
            pltpu.VMEM((hs * n_chunks, DN_DK, DN_DV), F32),
            pltpu.VMEM((hs * n_chunks, LANES), F32),
            pltpu.VMEM((tb_rows, width), F32),
            pltpu.VMEM((DN_HEADS, DN_DK, DN_DV), F32),
        ],
        compiler_params=_cparams(2),
        name="deltanet",
    )(u, u, u, u, u, u, u, u, conv_w, conv_w, conv_w, alog_row, dtb_row, out_norm)


def _gla_kernel(q_ref, k_ref, v_ref, r_ref, sm_ref, gu_ref, gb_ref, on_ref, o_ref,
                qg_s, oi_s, kv_s, gl_s, st_ref, *, tb_rows, hs):
    hstep = pl.program_id(1)
    n_pairs = tb_rows // PAIR
    n_chunks = 2 * n_pairs
    heads = [hstep * hs + hh for hh in range(hs)]
    kcols = [slice(hh * GLA_DK, (hh + 1) * GLA_DK) for hh in range(hs)]
    vcols = [slice(hh * GLA_DV, (hh + 1) * GLA_DV) for hh in range(hs)]

    @pl.when(pl.program_id(0) == 0)
    def _():
        for head in heads:
            st_ref[head] = jnp.zeros((GLA_DV, GLA_DK), F32)

    logits = jnp.dot(_bf(sm_ref[...]), gu_ref[...], preferred_element_type=F32) + gb_ref[...]
    la = jax.nn.log_sigmoid(logits) * (1.0 / GLA_TAU)

    sub, lane, same = _pair_masks()
    causal = same & (sub >= lane)
    tri = jnp.where(causal, 1.0, 0.0).astype(BF16)
    first = sub < CHUNK
    gain = on_ref[...]

    rows = [slice(p * PAIR, (p + 1) * PAIR) for p in range(n_pairs)]
    halves = [slice(c * CHUNK, (c + 1) * CHUNK) for c in range(2)]
    tiles = [(hh, p) for hh in range(hs) for p in range(n_pairs)]
    g = _chunk_cumsum(tri, [la[rows[p], kcols[hh]] for hh, p in tiles])
    k2 = [k_ref[rows[p], kcols[hh]] for hh, p in tiles]
    v2 = [_bf(v_ref[rows[p], vcols[hh]]) for hh, p in tiles]
    qg = [_bf(q_ref[rows[p], kcols[hh]] * (GLA_DK ** -0.5) * jnp.exp(gt))
          for (hh, p), gt in zip(tiles, g)]
    kg = _each(lambda kt, gt: _bf(kt * jnp.exp(-gt)), k2, g)
    a = _each(lambda qt, kt: _bf(jnp.where(causal, _mm_nt(qt, kt), 0.0)), qg, kg)
    oi = _each(_mm, a, v2)
    g_end = [[gt[CHUNK - 1:CHUNK, :], gt[PAIR - 1:PAIR, :]] for gt in g]
    kd = _each(lambda kt, gt, ge: _bf(kt * jnp.exp(jnp.where(first, ge[0], ge[1]) - gt)),
               k2, g, g_end)
    for t, (hh, p) in enumerate(tiles):
        qg_s[hh, rows[p], :] = qg[t]
        oi_s[hh, rows[p], :] = oi[t]
        for c, sl in enumerate(halves):
            ci = hh * n_chunks + 2 * p + c
            kv_s[ci] = _mm_tn(v2[t][sl], kd[t][sl])
            gl_s[ci:ci + 1, :] = jnp.exp(g_end[t][c])

    def step(c, states):
        rws = pl.ds(pl.multiple_of(c * CHUNK, CHUNK), CHUNK)
        for hh in range(hs):
            o_c = _mm_nt(qg_s[hh, rws, :], states[hh]) + oi_s[hh, rws, :]
            o_ref[rws, vcols[hh]] = _bf(_rms_rows(o_c, gain) * _silu(r_ref[rws, vcols[hh]]))
        return tuple(states[hh] * gl_s[pl.ds(hh * n_chunks + c, 1), :] + kv_s[hh * n_chunks + c]
                     for hh in range(hs))

    final = lax.fori_loop(0, n_chunks, step, tuple(st_ref[head] for head in heads), unroll=2)
    for head, st in zip(heads, final):
        st_ref[head] = st


def _gla(u, gate_up_pad, gate_bias, out_norm, *, tb_rows=256, hs=4):
    t = u.shape[0]
    kw, vw = hs * GLA_DK, hs * GLA_DV
    assert GLA_HEADS % hs == 0
    n_chunks = tb_rows // CHUNK
    return pl.pallas_call(
        functools.partial(_gla_kernel, tb_rows=tb_rows, hs=hs),
        grid=(t // tb_rows, GLA_HEADS // hs),
        in_specs=[
            pl.BlockSpec((tb_rows, kw), lambda i, h: (i, U_GLQ // kw + h)),
            pl.BlockSpec((tb_rows, kw), lambda i, h: (i, U_GLK // kw + h)),
            pl.BlockSpec((tb_rows, vw), lambda i, h: (i, U_GLV // vw + h)),
            pl.BlockSpec((tb_rows, vw), lambda i, h: (i, U_GLR // vw + h)),
            pl.BlockSpec((tb_rows, LANES), lambda i, h: (i, U_SMALL // LANES)),
            pl.BlockSpec((LANES, kw), lambda i, h: (0, h)),
            pl.BlockSpec((1, kw), lambda i, h: (0, h)),
            pl.BlockSpec((1, GLA_DV), lambda i, h: (0, 0)),
        ],
        out_specs=pl.BlockSpec((tb_rows, vw), lambda i, h: (i, h)),
        out_shape=jax.ShapeDtypeStruct((t, GLA_WIDTH), BF16),
        scratch_shapes=[
            pltpu.VMEM((hs, tb_rows, GLA_DK), BF16),
            pltpu.VMEM((hs, tb_rows, GLA_DV), F32),
            pltpu.VMEM((hs * n_chunks, GLA_DV, GLA_DK), F32),
            pltpu.VMEM((hs * n_chunks, GLA_DK), F32),
            pltpu.VMEM((GLA_HEADS, GLA_DV, GLA_DK), F32),
        ],
        compiler_params=_cparams(2),
        name="gla",
    )(u, u, u, u, u, gate_up_pad, gate_bias, out_norm)


def _swa_kernel(q_ref, kc_ref, kp_ref, vc_ref, vp_ref, qg_ref, kg_ref, sink_ref, o_ref):
    n = pl.program_id(0)
    w = SWA_WINDOW
    lane = lax.broadcasted_iota(jnp.int32, (w, LANES), 1)
    lo = lane < SWA_HD
    lane2 = lax.broadcasted_iota(jnp.int32, (2 * w, LANES), 1)
    lo2 = lane2 < SWA_HD
    qi = lax.broadcasted_iota(jnp.int32, (w, 2 * w), 0)
    ki = lax.broadcasted_iota(jnp.int32, (w, 2 * w), 1)
    mask = (ki <= qi + w) & (ki > qi + w - SWA_WINDOW) & ((n > 0) | (ki >= w))
    kk = jnp.concatenate([kp_ref[...], kc_ref[...]], axis=0)
    vv = jnp.concatenate([vp_ref[...], vc_ref[...]], axis=0)
    qgain = qg_ref[...]
    kgain = kg_ref[...]
    heads_per_grp = SWA_HQ // SWA_HKV
    pcols = [slice(p * LANES, (p + 1) * LANES) for p in range(SWA_HQ // 2)]

    def head_halves(t, g):
        tile = t[:, (g // 2) * LANES:(g // 2 + 1) * LANES]
        swapped = pltpu.roll(tile, SWA_HD, axis=1)
        return (tile, swapped) if g % 2 == 0 else (swapped, tile)

    def norm_k(kg):
        ms = jnp.sum(kg * kg, axis=-1, keepdims=True) * (1.0 / LANES)
        return _bf(kg * lax.rsqrt(ms + EPS) * kgain)

    def norm_q(qp):
        sq = qp * qp
        ms_lo = jnp.sum(jnp.where(lo, sq, 0.0), axis=-1, keepdims=True) * (1.0 / SWA_HD)
        ms_hi = jnp.sum(jnp.where(lo, 0.0, sq), axis=-1, keepdims=True) * (1.0 / SWA_HD)
        qn = qp * jnp.where(lo, lax.rsqrt(ms_lo + EPS), lax.rsqrt(ms_hi + EPS)) * qgain
        return qn * SWA_SCALE

    def probs(s, sink):
        s = jnp.where(mask, s, -jnp.inf)
        m = jnp.maximum(jnp.max(s, axis=-1, keepdims=True), sink)
        p = jnp.exp(s - m)
        inv = 1.0 / (jnp.sum(p, axis=-1, keepdims=True) + jnp.exp(sink - m))
        return _bf(p * inv)

    groups = range(SWA_HKV)
    k_lo_hi = [head_halves(kk, g) for g in groups]
    kn = [norm_k(jnp.where(lo2, k_lo, k_hi)) for k_lo, k_hi in k_lo_hi]
    v_lo_hi = [head_halves(vv, g) for g in groups]
    v_half = [(_bf(jnp.where(lo2, v_lo, 0.0)), _bf(jnp.where(lo2, 0.0, v_hi)))
              for v_lo, v_hi in v_lo_hi]
    qn = [norm_q(q_ref[:, c]) for c in pcols]
    heads = range(SWA_HQ)
    qm = [_bf(jnp.where(lo, qn[h // 2], 0.0) if h % 2 == 0 else jnp.where(lo, 0.0, qn[h // 2]))
          for h in heads]
    s = [_mm_nt(qm[h], kn[h // heads_per_grp]) for h in heads]
    p = [probs(s[h], sink_ref[h]) for h in heads]
    part = [jnp.dot(p[h], v_half[h // heads_per_grp][h % 2], preferred_element_type=F32)
            for h in heads]
    for pr, c in enumerate(pcols):
        o_ref[:, c] = _bf(part[2 * pr] + part[2 * pr + 1])


def _swa(u, qgain, kgain, sinks):
    t = u.shape[0]
    w = SWA_WINDOW
    kvw = SWA_KV
    assert U_SWK % kvw == 0 and U_SWV % kvw == 0 and U_SWQ % SWA_WIDTH == 0
    cur = lambda c: (lambda n: (n, c))
    prv = lambda c: (lambda n: (jnp.maximum(n - 1, 0), c))
    row = pl.BlockSpec((1, LANES), lambda n: (0, 0))
    return pl.pallas_call(
        _swa_kernel,
        grid=(t // w,),
        in_specs=[
            pl.BlockSpec((w, SWA_WIDTH), cur(U_SWQ // SWA_WIDTH)),
            pl.BlockSpec((w, kvw), cur(U_SWK // kvw)),
            pl.BlockSpec((w, kvw), prv(U_SWK // kvw)),
            pl.BlockSpec((w, kvw), cur(U_SWV // kvw)),
            pl.BlockSpec((w, kvw), prv(U_SWV // kvw)),
            row, row,
            pl.BlockSpec(memory_space=pltpu.SMEM),
        ],
        out_specs=pl.BlockSpec((w, SWA_WIDTH), lambda n: (n, 0)),
        out_shape=jax.ShapeDtypeStruct((t, SWA_WIDTH), BF16),
        compiler_params=_cparams(1),
        name="swa",
    )(u, u, u, u, u, qgain, kgain, sinks)


def _gatemix_kernel(h_ref, od_ref, og_ref, os_ref, g0_ref, g1_ref, g2_ref,
                    wd_ref, wg_ref, ws_ref, y_ref):
    dot = functools.partial(jnp.dot, preferred_element_type=F32)
    h = h_ref[...]
    y = (jax.nn.sigmoid(_mm_nt(h, g0_ref[...])) * dot(od_ref[...], wd_ref[...])
         + jax.nn.sigmoid(_mm_nt(h, g1_ref[...])) * dot(og_ref[...], wg_ref[...])
         + jax.nn.sigmoid(_mm_nt(h, g2_ref[...])) * dot(os_ref[...], ws_ref[...]))
    y_ref[...] = _bf(y)


def _gatemix(h, o_dn, o_gla, o_swa, wp, layer, w_dn, w_gla, w_swa, *, tm=1024, tj=512):
    t, d = h.shape
    nj = d // tj
    gate = lambda b: (lambda j, i: (layer, b * nj + j, 0))
    act = lambda width: pl.BlockSpec((tm, width), lambda j, i: (i, 0))
    wcol = lambda width: pl.BlockSpec((width, tj), lambda j, i: (0, j))
    return pl.pallas_call(
        _gatemix_kernel,
        grid=(nj, t // tm),
        in_specs=[
            act(d), act(DN_WIDTH), act(GLA_WIDTH), act(SWA_WIDTH),
            pl.BlockSpec((None, tj, d), gate(0)), pl.BlockSpec((None, tj, d), gate(1)),
            pl.BlockSpec((None, tj, d), gate(2)),
            wcol(DN_WIDTH), wcol(GLA_WIDTH), wcol(SWA_WIDTH),
        ],
        out_specs=pl.BlockSpec((tm, tj), lambda j, i: (i, j)),
        out_shape=jax.ShapeDtypeStruct((t, d), BF16),
        compiler_params=_cparams(2),
        name="gatemix",
    )(h, o_dn, o_gla, o_swa, wp, wp, wp, w_dn, w_gla, w_swa)


def _outproj_kernel(x_ref, y_ref, w_ref, o_ref):
    o_ref[...] = x_ref[...] + jnp.dot(y_ref[...], w_ref[...], preferred_element_type=F32)


def _outproj(x, y, w_o, *, tm=512):
    t, d = x.shape
    return pl.pallas_call(
        _outproj_kernel,
        grid=(t // tm,),
        in_specs=[
            pl.BlockSpec((tm, d), lambda i: (i, 0)),
            pl.BlockSpec((tm, d), lambda i: (i, 0)),
            pl.BlockSpec((d, d), lambda i: (0, 0)),
        ],
        out_specs=pl.BlockSpec((tm, d), lambda i: (i, 0)),
        out_shape=jax.ShapeDtypeStruct((t, d), F32),
        compiler_params=_cparams(1),
        name="outproj",
    )(x, y, w_o)


def _lane_row(vec):
    return jnp.pad(vec.astype(F32), (0, LANES - vec.shape[0]))[None, :]


def kernel(x, ffn1_norm, ffn1_w1, ffn1_w3, ffn1_w2, mix_norm, w_in, dn_conv, dn_a_log, dn_dt_bias, dn_out_norm, gla_gate_up, gla_gate_bias, gla_out_norm, swa_q_norm, swa_k_norm, swa_sinks, w_branch_dn, w_branch_gla, w_branch_swa, w_out, ffn2_norm, ffn2_w1, ffn2_w3, ffn2_w2):
    assert x.shape[0] == 1 and x.shape[2] == D_MODEL
    xs = x[0]
    wp = _relayout_w_in(jnp.swapaxes(w_in, 1, 2))
    for l in range(DEPTH):
        xs = _ffn(xs, ffn1_norm[l][None], ffn1_w1, ffn1_w3, ffn1_w2, l)
        u, h = _inproj(xs, mix_norm[l][None], wp, l)
        o_dn = _deltanet(u, dn_conv[l], _lane_row(dn_a_log[l]), _lane_row(dn_dt_bias[l]),
                         dn_out_norm[l][None])
        gate_up_pad = jnp.zeros((LANES, GLA_QK), BF16).at[SM_LR:SM_LR + GLA_RANK].set(
            _bf(gla_gate_up[l]))
        o_gla = _gla(u, gate_up_pad, gla_gate_bias[l][None], gla_out_norm[l][None])
        o_swa = _swa(u, jnp.tile(swa_q_norm[l], 2)[None], jnp.tile(swa_k_norm[l], 2)[None],
                     swa_sinks[l])
        y = _gatemix(h, o_dn, o_gla, o_swa, wp, l, _bf(w_branch_dn[l]), _bf(w_branch_gla[l]),
                     _bf(w_branch_swa[l]))
        xs = _outproj(xs, y, _bf(w_out[l]))
        xs = _ffn(xs, ffn2_norm[l][None], ffn2_w1, ffn2_w3, ffn2_w2, l)
    return xs[None]
```

```python
import functools

import numpy as np
import jax
import jax.numpy as jnp
from jax import lax
from jax.experimental import pallas as pl
from jax.experimental.pallas import tpu as pltpu

F32 = jnp.float32
BF16 = jnp.bfloat16

D_MODEL = 2048
DEPTH = 2
EPS = 1e-6
D_FF = 5632
FFN_RES_SCALE = 0.5
DN_HEADS, DN_DK, DN_DV, DN_CONV = 8, 128, 128, 4
GLA_HEADS, GLA_DK, GLA_DV, GLA_RANK, GLA_TAU = 4, 128, 256, 16, 16.0
SWA_HQ, SWA_HKV, SWA_HD, SWA_WINDOW = 16, 4, 64, 128
N_BRANCH = 3
CHUNK = 64
DN_QK = DN_HEADS * DN_DK
DN_WIDTH = DN_HEADS * DN_DV
GLA_QK = GLA_HEADS * GLA_DK
GLA_WIDTH = GLA_HEADS * GLA_DV
SWA_WIDTH = SWA_HQ * SWA_HD
SWA_KV = SWA_HKV * SWA_HD
SWA_SCALE = SWA_HD ** -0.5
assert SWA_SCALE == 2.0 ** -3

_SPLITS = (DN_QK, DN_QK, DN_WIDTH, DN_WIDTH, DN_HEADS, DN_HEADS,
           GLA_QK, GLA_QK, GLA_WIDTH, GLA_WIDTH, GLA_RANK,
           SWA_WIDTH, SWA_KV, SWA_KV, N_BRANCH * D_MODEL)
_OFFS = tuple(sum(_SPLITS[:i]) for i in range(len(_SPLITS) + 1))
(W_DNQ, W_DNK, W_DNV, W_DNZ, W_DNA, W_DNB, W_GLQ, W_GLK, W_GLV, W_GLR, W_GLLR,
 W_SWQ, W_SWK, W_SWV, W_GATE, W_END) = _OFFS

LANES = 128
SUBLANES = 8
PAIR = 2 * CHUNK

U_DN = 0
U_GLQ = U_DN + 4 * DN_QK
U_GLK = U_GLQ + GLA_QK
U_GLV = U_GLK + GLA_QK
U_GLR = U_GLV + GLA_WIDTH
U_SWQ = U_GLR + GLA_WIDTH
U_SWK = U_SWQ + SWA_WIDTH
U_SWV = U_SWK + SWA_KV
U_SMALL = U_SWV + SWA_KV
SM_A, SM_B, SM_LR = 0, DN_HEADS, 2 * DN_HEADS

RELAY_BW = 512
_RELAY_REGIONS = ((W_GATE, N_BRANCH * D_MODEL // RELAY_BW), (W_DNQ, 4 * DN_QK // RELAY_BW),
                  (W_GLQ, (2 * GLA_QK + 2 * GLA_WIDTH) // RELAY_BW),
                  (W_SWQ, (SWA_WIDTH + 2 * SWA_KV) // RELAY_BW))
RELAY_NBLK = sum(n for _, n in _RELAY_REGIONS) + 1
WP_GATE_COLS = N_BRANCH * D_MODEL
U_COLS = (RELAY_NBLK * RELAY_BW) - WP_GATE_COLS

VMEM_LIMIT = 60 * 1024 * 1024


def _cparams(n_axes):
    return pltpu.CompilerParams(dimension_semantics=("arbitrary",) * n_axes,
                                vmem_limit_bytes=VMEM_LIMIT)


def _bf(t):
    return t.astype(BF16)


def _mm(a, b):
    return jnp.dot(_bf(a), _bf(b), preferred_element_type=F32)


def _mm_nt(a, b):
    return lax.dot_general(_bf(a), _bf(b), (((1,), (1,)), ((), ())), preferred_element_type=F32)


def _mm_tn(a, b):
    return lax.dot_general(_bf(a), _bf(b), (((0,), (0,)), ((), ())), preferred_element_type=F32)


def _silu(t):
    return t * jax.nn.sigmoid(t)


def _rms_rows(x, gain):
    ms = jnp.mean(x * x, axis=-1, keepdims=True)
    return x * lax.rsqrt(ms + EPS) * gain


def _each(fn, *lists):
    return [fn(*args) for args in zip(*lists)]


def _chunk_cumsum(tri, tiles):
    hi = _each(_bf, tiles)
    r1 = _each(lambda t, h: t - h.astype(F32), tiles, hi)
    mid = _each(_bf, r1)
    lo = _each(lambda r, m: _bf(r - m.astype(F32)), r1, mid)
    dot = functools.partial(jnp.dot, tri, preferred_element_type=F32)
    return _each(lambda h, m, l: dot(h) + dot(m) + dot(l), hi, mid, lo)


def _pair_masks():
    sub = lax.broadcasted_iota(jnp.int32, (PAIR, PAIR), 0)
    lane = lax.broadcasted_iota(jnp.int32, (PAIR, PAIR), 1)
    same = (sub >= CHUNK) == (lane >= CHUNK)
    return sub, lane, same


def _ffn_kernel(x_ref, g_ref, w1_ref, w3_ref, w2_ref, o_ref, h_ref):
    @pl.when(pl.program_id(1) == 0)
    def _():
        x = x_ref[...]
        h_ref[...] = _bf(_rms_rows(x, g_ref[...]))
        o_ref[...] = x

    h = h_ref[...]
    a = jnp.dot(h, _bf(w1_ref[...]), preferred_element_type=F32)
    b = jnp.dot(h, _bf(w3_ref[...]), preferred_element_type=F32)
    act = _silu(a) * (b * FFN_RES_SCALE)
    o_ref[...] += jnp.dot(_bf(act), _bf(w2_ref[...]), preferred_element_type=F32)


def _ffn(x, gain, w1, w3, w2, layer, *, tm=1024, tf=512):
    t, d = x.shape
    f = w1.shape[2]
    return pl.pallas_call(
        _ffn_kernel,
        grid=(t // tm, f // tf),
        in_specs=[
            pl.BlockSpec((tm, d), lambda i, j: (i, 0), pipeline_mode=pl.Buffered(1)),
            pl.BlockSpec((1, d), lambda i, j: (0, 0)),
            pl.BlockSpec((None, d, tf), lambda i, j: (layer, 0, j)),
            pl.BlockSpec((None, d, tf), lambda i, j: (layer, 0, j)),
            pl.BlockSpec((None, tf, d), lambda i, j: (layer, j, 0)),
        ],
        out_specs=pl.BlockSpec((tm, d), lambda i, j: (i, 0)),
        out_shape=jax.ShapeDtypeStruct((t, d), F32),
        scratch_shapes=[pltpu.VMEM((tm, d), BF16)],
        compiler_params=_cparams(2),
        name="ffn",
    )(x, gain, w1, w3, w2)


def _inproj_kernel(x_ref, g_ref, w_ref, u_ref, h_ref):
    @pl.when(pl.program_id(1) == 0)
    def _():
        h_ref[...] = _bf(_rms_rows(x_ref[...], g_ref[...]))

    u_ref[...] = _mm_nt(h_ref[...], w_ref[...])


def _inproj(x, gain, wp, layer, *, tm=1024, tn=1536):
    t, d = x.shape
    assert WP_GATE_COLS % tn == 0 and U_COLS % tn == 0
    return pl.pallas_call(
        _inproj_kernel,
        grid=(t // tm, U_COLS // tn),
        in_specs=[
            pl.BlockSpec((tm, d), lambda i, j: (i, 0)),
            pl.BlockSpec((1, d), lambda i, j: (0, 0)),
            pl.BlockSpec((None, tn, d), lambda i, j: (layer, WP_GATE_COLS // tn + j, 0)),
        ],
        out_specs=[pl.BlockSpec((tm, tn), lambda i, j: (i, j)),
                   pl.BlockSpec((tm, d), lambda i, j: (i, 0))],
        out_shape=[jax.ShapeDtypeStruct((t, U_COLS), F32), jax.ShapeDtypeStruct((t, d), BF16)],
        compiler_params=_cparams(2),
        name="inproj",
    )(x, gain, wp)


def _relay_rows():
    rows = [src + k * RELAY_BW for src, nblk in _RELAY_REGIONS for k in range(nblk)]
    assert all(r % SUBLANES == 0 for r in rows) and len(rows) == RELAY_NBLK - 1
    return np.asarray(rows + [0], np.int32) // SUBLANES


def _relayout_kernel(tab_ref, src_ref, ab_ref, lr_ref, o_ref):
    last = pl.program_id(1) == RELAY_NBLK - 1

    @pl.when(jnp.logical_not(last))
    def _():
        o_ref[...] = _bf(src_ref[0])

    @pl.when(last)
    def _():
        n_ab, n_lr = ab_ref.shape[0], lr_ref.shape[0]
        o_ref[...] = jnp.zeros(o_ref.shape, o_ref.dtype)
        o_ref[SM_A:SM_A + n_ab, :] = _bf(ab_ref[...])
        o_ref[SM_LR:SM_LR + n_lr, :] = _bf(lr_ref[...])


def _relayout_w_in(w_t):
    n_layers, _, d = w_t.shape
    n_ab, n_lr = 2 * DN_HEADS, GLA_RANK
    assert W_DNA % n_ab == 0 and W_GLLR % n_lr == 0 and SM_LR == n_ab
    return pl.pallas_call(
        _relayout_kernel,
        grid_spec=pltpu.PrefetchScalarGridSpec(
            num_scalar_prefetch=1,
            grid=(n_layers, RELAY_NBLK),
            in_specs=[
                pl.BlockSpec((pl.Element(1), pl.Element(RELAY_BW), pl.Element(d)),
                             lambda l, j, tab: (l, tab[j] * SUBLANES, 0)),
                pl.BlockSpec((None, n_ab, d), lambda l, j, tab: (l, W_DNA // n_ab, 0)),
                pl.BlockSpec((None, n_lr, d), lambda l, j, tab: (l, W_GLLR // n_lr, 0)),
            ],
            out_specs=pl.BlockSpec((None, RELAY_BW, d), lambda l, j, tab: (l, j, 0)),
        ),
        out_shape=jax.ShapeDtypeStruct((n_layers, RELAY_NBLK * RELAY_BW, d), BF16),
        compiler_params=_cparams(2),
        name="relayout_w_in",
    )(jnp.asarray(_relay_rows()), w_t, w_t, w_t)


def _tri_inv(a, sub, lane, eye):
    bd16 = (sub >> 4) == (lane >> 4)
    bd32 = (sub >> 5) == (lane >> 5)
    off32 = bd32 & jnp.logical_not(bd16)
    a16 = _each(lambda t: _bf(jnp.where(bd16, t, 0.0)), a)
    x = _each(lambda t: eye - t.astype(F32), a16)
    p = a16
    for _ in range(3):
        p = _each(lambda t: _bf(_mm(t, t)), p)
        x = _each(lambda xt, pt: xt + _mm(xt, pt), x, p)
    for blk in (lambda t: jnp.where(off32, t, 0.0), lambda t: jnp.where(bd32, 0.0, t)):
        xb = _each(_bf, x)
        y = _each(lambda at, xt: _mm(blk(at), xt), a, xb)
        x = _each(lambda xt, xbt, yt: xt - _mm(xbt, yt), x, xb, y)
    return x


def _dn_kernel(q_ref, k_ref, v_ref, z_ref, pq_ref, pk_ref, pv_ref, sm_ref,
               cq_ref, ck_ref, cv_ref, alog_ref, dtb_ref, on_ref, o_ref,
               xs_ref, gc_s, gt_s, b_s, qe_s, ol_s, pm_s, qc_s, gl_s, o_s, s_ref,
               *, tb_rows, hs):
    tb = pl.program_id(0)
    hstep = pl.program_id(1)
    n_pairs = tb_rows // PAIR
    n_chunks = 2 * n_pairs
    heads = [hstep * hs + hh for hh in range(hs)]
    hcols = [slice(hh * LANES, (hh + 1) * LANES) for hh in range(hs)]

    @pl.when(tb == 0)
    def _():
        for head in heads:
            s_ref[head] = jnp.zeros((DN_DK, DN_DV), F32)

    sub, lane, same = _pair_masks()
    causal = same & (sub >= lane)
    strict = same & (sub > lane)
    eye = jnp.where(sub == lane, 1.0, 0.0).astype(F32)
    tri = jnp.where(causal, 1.0, 0.0).astype(BF16)
    first = sub[:, :1] < CHUNK
    rows = [slice(p * PAIR, (p + 1) * PAIR) for p in range(n_pairs)]
    halves = [slice(c * CHUNK, (c + 1) * CHUNK) for c in range(2)]

    @pl.when(hstep == 0)
    def _():
        sm = sm_ref[...]
        b_s[...] = jax.nn.sigmoid(sm)
        gs = -jnp.exp(alog_ref[...]) * jax.nn.softplus(sm + dtb_ref[...])
        gall = _chunk_cumsum(tri, [gs[r] for r in rows])
        for r, gt in zip(rows, gall):
            gc_s[r, :] = gt
            gt_s[r, :] = gt.T

    keep = (tb > 0).astype(F32)

    def conv_silu(x_ref, p_ref, w_ref):
        xs_ref[0:SUBLANES, :] = p_ref[...] * keep
        xs_ref[SUBLANES:, :] = x_ref[...]
        w = w_ref[...]
        y = w[DN_CONV - 1:DN_CONV, :] * xs_ref[pl.ds(SUBLANES, tb_rows), :]
        for kk in range(DN_CONV - 1):
            off = SUBLANES - (DN_CONV - 1) + kk
            y = y + w[kk:kk + 1, :] * xs_ref[pl.ds(off, tb_rows), :]
        return _silu(y)

    def l2norm(t):
        return t * lax.rsqrt(jnp.sum(t * t, axis=-1, keepdims=True) + EPS)

    qc = conv_silu(q_ref, pq_ref, cq_ref)
    kc = conv_silu(k_ref, pk_ref, ck_ref)
    vc = conv_silu(v_ref, pv_ref, cv_ref)
    qn = [l2norm(qc[:, c]) * (DN_DK ** -0.5) for c in hcols]
    kn = [l2norm(kc[:, c]) for c in hcols]

    tiles = [(hh, p) for hh in range(hs) for p in range(n_pairs)]
    q2 = [qn[hh][rows[p]] for hh, p in tiles]
    k2 = [kn[hh][rows[p]] for hh, p in tiles]
    v2 = [vc[rows[p], hcols[hh]] for hh, p in tiles]
    g = [jnp.sum(jnp.where(lane == heads[hh] + SM_A, gc_s[rows[p], :], 0.0), axis=1, keepdims=True)
         for hh, p in tiles]
    beta = [jnp.sum(jnp.where(lane == heads[hh] + SM_B, b_s[rows[p], :], 0.0), axis=1, keepdims=True)
            for hh, p in tiles]
    grow = [jnp.sum(jnp.where(sub == heads[hh] + SM_A, gt_s[rows[p], :], 0.0), axis=0, keepdims=True)
            for hh, p in tiles]
    decay = _each(lambda gc, gr: jnp.exp(jnp.where(causal, gc - gr, 0.0)), g, grow)
    eg = _each(jnp.exp, g)
    kb = _each(lambda kt, bt: kt * bt, k2, beta)
    k2b = _each(_bf, k2)
    a = _each(lambda kbt, kt, dt: jnp.where(strict, _mm_nt(kbt, kt) * dt, 0.0), kb, k2b, decay)
    at = _each(lambda qt, kt, dt: _bf(jnp.where(causal, _mm_nt(qt, kt) * dt, 0.0)), q2, k2b, decay)
    tm = _tri_inv(a, sub, lane, eye)
    rhs = _each(lambda vt, bt, kbt, egt: jnp.concatenate([vt * bt, kbt * egt], axis=1),
                v2, beta, kb, eg)
    sol = _each(_mm, tm, rhs)
    value = [_bf(t[:, :DN_DV]) for t in sol]
    kcum = [_bf(t[:, DN_DV:]) for t in sol]
    g_end = [[gt[CHUNK - 1:CHUNK, :], gt[PAIR - 1:PAIR, :]] for gt in g]
    kd = _each(lambda kt, gt, ge: _bf(kt * jnp.exp(jnp.where(first, ge[0], ge[1]) - gt)),
               k2, g, g_end)
    qe = _each(lambda qt, egt, att, kct: _bf(qt * egt - _mm(att, kct)), q2, eg, at, kcum)
    ol = _each(_mm, at, value)
    for t, (hh, p) in enumerate(tiles):
        qe_s[hh, rows[p], :] = qe[t]
        ol_s[hh, rows[p], :] = ol[t]
    for t, (hh, p) in enumerate(tiles):
        for c, sl in enumerate(halves):
            pm_s[hh * n_chunks + 2 * p + c] = _bf(-_mm_tn(kd[t][sl], kcum[t][sl]))
    for t, (hh, p) in enumerate(tiles):
        for c, sl in enumerate(halves):
            ci = hh * n_chunks + 2 * p + c
            qc_s[ci] = _mm_tn(kd[t][sl], value[t][sl])
            gl_s[ci:ci + 1, :] = jnp.broadcast_to(jnp.exp(g_end[t][c]), (1, LANES))

    def step(c, states):
        crow = pl.ds(pl.multiple_of(c * CHUNK, CHUNK), CHUNK)
        sb = [_bf(s) for s in states]
        for hh in range(hs):
            o_s[crow, hcols[hh]] = (jnp.dot(qe_s[hh, crow, :], sb[hh], preferred_element_type=F32)
                                    + ol_s[hh, crow, :])
        upd = [jnp.dot(pm_s[hh * n_chunks + c], sb[hh], preferred_element_type=F32)
               for hh in range(hs)]
        return tuple(gl_s[pl.ds(hh * n_chunks + c, 1), :] * states[hh] + upd[hh]
                     + qc_s[hh * n_chunks + c] for hh in range(hs))

    final = lax.fori_loop(0, n_chunks, step, tuple(s_ref[head] for head in heads), unroll=True)
    for head, s in zip(heads, final):
        s_ref[head] = s
    gain = on_ref[...]
    for c in hcols:
        o_ref[:, c] = _bf(_rms_rows(o_s[:, c], gain) * _silu(z_ref[:, c]))


def _deltanet(u, conv_w, alog_row, dtb_row, out_norm, *, tb_rows=256, hs=8):
    t = u.shape[0]
    width = hs * LANES
    nb = DN_QK // width
    prev_blk = tb_rows // SUBLANES
    assert DN_DK == LANES and DN_DV == LANES and DN_HEADS % hs == 0

    def col(seg):
        return lambda i, h: (i, U_DN // width + seg * nb + h)

    def prev(seg):
        return lambda i, h: (jnp.maximum(i * prev_blk - 1, 0), U_DN // width + seg * nb + h)

    def cw(seg):
        return lambda i, h: (0, seg * nb + h)

    row = pl.BlockSpec((1, LANES), lambda i, h: (0, 0))
    blk = lambda seg: pl.BlockSpec((tb_rows, width), col(seg))
    pblk = lambda seg: pl.BlockSpec((SUBLANES, width), prev(seg))
    cblk = lambda seg: pl.BlockSpec((DN_CONV, width), cw(seg))
    n_chunks = tb_rows // CHUNK
    rows_f32 = pltpu.VMEM((tb_rows, LANES), F32)
    return pl.pallas_call(
        functools.partial(_dn_kernel, tb_rows=tb_rows, hs=hs),
        grid=(t // tb_rows, DN_HEADS // hs),
        in_specs=[blk(0), blk(1), blk(2), blk(3), pblk(0), pblk(1), pblk(2),
                  pl.BlockSpec((tb_rows, LANES), lambda i, h: (i, U_SMALL // LANES)),
                  cblk(0), cblk(1), cblk(2), row, row, row],
        out_specs=pl.BlockSpec((tb_rows, width), lambda i, h: (i, h)),
        out_shape=jax.ShapeDtypeStruct((t, DN_WIDTH), BF16),
        scratch_shapes=[
            pltpu.VMEM((tb_rows + SUBLANES, width), F32),
            rows_f32, rows_f32, rows_f32,
            pltpu.VMEM((hs, tb_rows, DN_DK), BF16),
            pltpu.VMEM((hs, tb_rows, DN_DV), F32),
            pltpu.VMEM((hs * n_chunks, DN_DK, DN_DK), BF16),
            pltpu.VMEM((hs * n_chunks, DN_DK, DN_DV), F32),
            pltpu.VMEM((hs * n_chunks, LANES), F32),
            pltpu.VMEM((tb_rows, width), F32),
            pltpu.VMEM((DN_HEADS, DN_DK, DN_DV), F32),
        ],
        compiler_params=_cparams(2),
        name="deltanet",
    )(u, u, u, u, u, u, u, u, conv_w, conv_w, conv_w, alog_row, dtb_row, out_norm)


def _gla_kernel(q_ref, k_ref, v_ref, r_ref, sm_ref, gu_ref, gb_ref, on_ref, o_ref,
                qg_s, oi_s, kv_s, gl_s, st_ref, *, tb_rows, hs):
    hstep = pl.program_id(1)
    n_pairs = tb_rows // PAIR
    n_chunks = 2 * n_pairs
    heads = [hstep * hs + hh for hh in range(hs)]
    kcols = [slice(hh * GLA_DK, (hh + 1) * GLA_DK) for hh in range(hs)]
    vcols = [slice(hh * GLA_DV, (hh + 1) * GLA_DV) for hh in range(hs)]

    @pl.when(pl.program_id(0) == 0)
    def _():
        for head in heads:
            st_ref[head] = jnp.zeros((GLA_DV, GLA_DK), F32)

    logits = jnp.dot(_bf(sm_ref[...]), gu_ref[...], preferred_element_type=F32) + gb_ref[...]
    la = jax.nn.log_sigmoid(logits) * (1.0 / GLA_TAU)

    sub, lane, same = _pair_masks()
    causal = same & (sub >= lane)
    tri = jnp.where(causal, 1.0, 0.0).astype(BF16)
    first = sub < CHUNK
    gain = on_ref[...]

    rows = [slice(p * PAIR, (p + 1) * PAIR) for p in range(n_pairs)]
    halves = [slice(c * CHUNK, (c + 1) * CHUNK) for c in range(2)]
    tiles = [(hh, p) for hh in range(hs) for p in range(n_pairs)]
    g = _chunk_cumsum(tri, [la[rows[p], kcols[hh]] for hh, p in tiles])
    k2 = [k_ref[rows[p], kcols[hh]] for hh, p in tiles]
    v2 = [_bf(v_ref[rows[p], vcols[hh]]) for hh, p in tiles]
    qg = [_bf(q_ref[rows[p], kcols[hh]] * (GLA_DK ** -0.5) * jnp.exp(gt))
          for (hh, p), gt in zip(tiles, g)]
    kg = _each(lambda kt, gt: _bf(kt * jnp.exp(-gt)), k2, g)
    a = _each(lambda qt, kt: _bf(jnp.where(causal, _mm_nt(qt, kt), 0.0)), qg, kg)
    oi = _each(_mm, a, v2)
    g_end = [[gt[CHUNK - 1:CHUNK, :], gt[PAIR - 1:PAIR, :]] for gt in g]
    kd = _each(lambda kt, gt, ge: _bf(kt * jnp.exp(jnp.where(first, ge[0], ge[1]) - gt)),
               k2, g, g_end)
    for t, (hh, p) in enumerate(tiles):
        qg_s[hh, rows[p], :] = qg[t]
        oi_s[hh, rows[p], :] = oi[t]
        for c, sl in enumerate(halves):
            ci = hh * n_chunks + 2 * p + c
            kv_s[ci] = _mm_tn(v2[t][sl], kd[t][sl])
            gl_s[ci:ci + 1, :] = jnp.exp(g_end[t][c])

    def step(c, states):
        rws = pl.ds(pl.multiple_of(c * CHUNK, CHUNK), CHUNK)
        for hh in range(hs):
            o_c = _mm_nt(qg_s[hh, rws, :], states[hh]) + oi_s[hh, rws, :]
            o_ref[rws, vcols[hh]] = _bf(_rms_rows(o_c, gain) * _silu(r_ref[rws, vcols[hh]]))
        return tuple(states[hh] * gl_s[pl.ds(hh * n_chunks + c, 1), :] + kv_s[hh * n_chunks + c]
                     for hh in range(hs))

    final = lax.fori_loop(0, n_chunks, step, tuple(st_ref[head] for head in heads), unroll=True)
    for head, st in zip(heads, final):
        st_ref[head] = st


def _gla(u, gate_up_pad, gate_bias, out_norm, *, tb_rows=512, hs=4):
    t = u.shape[0]
    kw, vw = hs * GLA_DK, hs * GLA_DV
    assert GLA_HEADS % hs == 0
    n_chunks = tb_rows // CHUNK
    return pl.pallas_call(
        functools.partial(_gla_kernel, tb_rows=tb_rows, hs=hs),
        grid=(t // tb_rows, GLA_HEADS // hs),
        in_specs=[
            pl.BlockSpec((tb_rows, kw), lambda i, h: (i, U_GLQ // kw + h)),
            pl.BlockSpec((tb_rows, kw), lambda i, h: (i, U_GLK // kw + h)),
            pl.BlockSpec((tb_rows, vw), lambda i, h: (i, U_GLV // vw + h)),
            pl.BlockSpec((tb_rows, vw), lambda i, h: (i, U_GLR // vw + h)),
            pl.BlockSpec((tb_rows, LANES), lambda i, h: (i, U_SMALL // LANES)),
            pl.BlockSpec((LANES, kw), lambda i, h: (0, h)),
            pl.BlockSpec((1, kw), lambda i, h: (0, h)),
            pl.BlockSpec((1, GLA_DV), lambda i, h: (0, 0)),
        ],
        out_specs=pl.BlockSpec((tb_rows, vw), lambda i, h: (i, h)),
        out_shape=jax.ShapeDtypeStruct((t, GLA_WIDTH), BF16),
        scratch_shapes=[
            pltpu.VMEM((hs, tb_rows, GLA_DK), BF16),
            pltpu.VMEM((hs, tb_rows, GLA_DV), F32),
            pltpu.VMEM((hs * n_chunks, GLA_DV, GLA_DK), F32),
            pltpu.VMEM((hs * n_chunks, GLA_DK), F32),
            pltpu.VMEM((GLA_HEADS, GLA_DV, GLA_DK), F32),
        ],
        compiler_params=_cparams(2),
        name="gla",
    )(u, u, u, u, u, gate_up_pad, gate_bias, out_norm)


def _swa_kernel(q_ref, kc_ref, kp_ref, vc_ref, vp_ref, qg_ref, kg_ref, sink_ref, o_ref):
    n = pl.program_id(0)
    w = SWA_WINDOW
    lane = lax.broadcasted_iota(jnp.int32, (w, LANES), 1)
    lo = lane < SWA_HD
    lane2 = lax.broadcasted_iota(jnp.int32, (2 * w, LANES), 1)
    lo2 = lane2 < SWA_HD
    qi = lax.broadcasted_iota(jnp.int32, (w, 2 * w), 0)
    ki = lax.broadcasted_iota(jnp.int32, (w, 2 * w), 1)
    mask = (ki <= qi + w) & (ki > qi + w - SWA_WINDOW) & ((n > 0) | (ki >= w))
    kk = jnp.concatenate([kp_ref[...], kc_ref[...]], axis=0)
    vv = jnp.concatenate([vp_ref[...], vc_ref[...]], axis=0)
    qgain = qg_ref[...]
    kgain = kg_ref[...]
    heads_per_grp = SWA_HQ // SWA_HKV
    pcols = [slice(p * LANES, (p + 1) * LANES) for p in range(SWA_HQ // 2)]

    def head_halves(t, g):
        tile = t[:, (g // 2) * LANES:(g // 2 + 1) * LANES]
        swapped = pltpu.roll(tile, SWA_HD, axis=1)
        return (tile, swapped) if g % 2 == 0 else (swapped, tile)

    def norm_k(kg):
        ms = jnp.sum(kg * kg, axis=-1, keepdims=True) * (1.0 / LANES)
        return _bf(kg * lax.rsqrt(ms + EPS) * kgain)

    def norm_q(qp):
        sq = qp * qp
        ms_lo = jnp.sum(jnp.where(lo, sq, 0.0), axis=-1, keepdims=True) * (1.0 / SWA_HD)
        ms_hi = jnp.sum(jnp.where(lo, 0.0, sq), axis=-1, keepdims=True) * (1.0 / SWA_HD)
        qn = qp * jnp.where(lo, lax.rsqrt(ms_lo + EPS), lax.rsqrt(ms_hi + EPS)) * qgain
        return qn * SWA_SCALE

    def probs(s, sink):
        s = jnp.where(mask, s, -jnp.inf)
        m = jnp.maximum(jnp.max(s, axis=-1, keepdims=True), sink)
        p = jnp.exp(s - m)
        inv = 1.0 / (jnp.sum(p, axis=-1, keepdims=True) + jnp.exp(sink - m))
        return _bf(p * inv)

    groups = range(SWA_HKV)
    k_lo_hi = [head_halves(kk, g) for g in groups]
    kn = [norm_k(jnp.where(lo2, k_lo, k_hi)) for k_lo, k_hi in k_lo_hi]
    v_lo_hi = [head_halves(vv, g) for g in groups]
    v_half = [(_bf(jnp.where(lo2, v_lo, 0.0)), _bf(jnp.where(lo2, 0.0, v_hi)))
              for v_lo, v_hi in v_lo_hi]
    qn = [norm_q(q_ref[:, c]) for c in pcols]
    heads = range(SWA_HQ)
    qm = [_bf(jnp.where(lo, qn[h // 2], 0.0) if h % 2 == 0 else jnp.where(lo, 0.0, qn[h // 2]))
          for h in heads]
    s = [_mm_nt(qm[h], kn[h // heads_per_grp]) for h in heads]
    p = [probs(s[h], sink_ref[h]) for h in heads]
    part = [jnp.dot(p[h], v_half[h // heads_per_grp][h % 2], preferred_element_type=F32)
            for h in heads]
    for pr, c in enumerate(pcols):
        o_ref[:, c] = _bf(part[2 * pr] + part[2 * pr + 1])


def _swa(u, qgain, kgain, sinks):
    t = u.shape[0]
    w = SWA_WINDOW
    kvw = SWA_KV
    assert U_SWK % kvw == 0 and U_SWV % kvw == 0 and U_SWQ % SWA_WIDTH == 0
    cur = lambda c: (lambda n: (n, c))
    prv = lambda c: (lambda n: (jnp.maximum(n - 1, 0), c))
    row = pl.BlockSpec((1, LANES), lambda n: (0, 0))
    return pl.pallas_call(
        _swa_kernel,
        grid=(t // w,),
        in_specs=[
            pl.BlockSpec((w, SWA_WIDTH), cur(U_SWQ // SWA_WIDTH)),
            pl.BlockSpec((w, kvw), cur(U_SWK // kvw)),
            pl.BlockSpec((w, kvw), prv(U_SWK // kvw)),
            pl.BlockSpec((w, kvw), cur(U_SWV // kvw)),
            pl.BlockSpec((w, kvw), prv(U_SWV // kvw)),
            row, row,
            pl.BlockSpec(memory_space=pltpu.SMEM),
        ],
        out_specs=pl.BlockSpec((w, SWA_WIDTH), lambda n: (n, 0)),
        out_shape=jax.ShapeDtypeStruct((t, SWA_WIDTH), BF16),
        compiler_params=_cparams(1),
        name="swa",
    )(u, u, u, u, u, qgain, kgain, sinks)


def _gatemix_kernel(h_ref, od_ref, og_ref, os_ref, g0_ref, g1_ref, g2_ref,
                    wd_ref, wg_ref, ws_ref, y_ref):
    dot = functools.partial(jnp.dot, preferred_element_type=F32)
    h = h_ref[...]
    y = (jax.nn.sigmoid(_mm_nt(h, g0_ref[...])) * dot(od_ref[...], wd_ref[...])
         + jax.nn.sigmoid(_mm_nt(h, g1_ref[...])) * dot(og_ref[...], wg_ref[...])
         + jax.nn.sigmoid(_mm_nt(h, g2_ref[...])) * dot(os_ref[...], ws_ref[...]))
    y_ref[...] = _bf(y)


def _gatemix(h, o_dn, o_gla, o_swa, wp, layer, w_dn, w_gla, w_swa, *, tm=1024, tj=512):
    t, d = h.shape
    nj = d // tj
    gate = lambda b: (lambda j, i: (layer, b * nj + j, 0))
    act = lambda width: pl.BlockSpec((tm, width), lambda j, i: (i, 0))
    wcol = lambda width: pl.BlockSpec((width, tj), lambda j, i: (0, j))
    return pl.pallas_call(
        _gatemix_kernel,
        grid=(nj, t // tm),
        in_specs=[
            act(d), act(DN_WIDTH), act(GLA_WIDTH), act(SWA_WIDTH),
            pl.BlockSpec((None, tj, d), gate(0)), pl.BlockSpec((None, tj, d), gate(1)),
            pl.BlockSpec((None, tj, d), gate(2)),
            wcol(DN_WIDTH), wcol(GLA_WIDTH), wcol(SWA_WIDTH),
        ],
        out_specs=pl.BlockSpec((tm, tj), lambda j, i: (i, j)),
        out_shape=jax.ShapeDtypeStruct((t, d), BF16),
        compiler_params=_cparams(2),
        name="gatemix",
    )(h, o_dn, o_gla, o_swa, wp, wp, wp, w_dn, w_gla, w_swa)


def _outproj_kernel(x_ref, y_ref, w_ref, o_ref):
    o_ref[...] = x_ref[...] + jnp.dot(y_ref[...], w_ref[...], preferred_element_type=F32)


def _outproj(x, y, w_o, *, tm=512):
    t, d = x.shape
    return pl.pallas_call(
        _outproj_kernel,
        grid=(t // tm,),
        in_specs=[
            pl.BlockSpec((tm, d), lambda i: (i, 0)),
            pl.BlockSpec((tm, d), lambda i: (i, 0)),
            pl.BlockSpec((d, d), lambda i: (0, 0)),
        ],
        out_specs=pl.BlockSpec((tm, d), lambda i: (i, 0)),
        out_shape=jax.ShapeDtypeStruct((t, d), F32),
        compiler_params=_cparams(1),
        name="outproj",
    )(x, y, w_o)


def _lane_row(vec):
    return jnp.pad(vec.astype(F32), (0, LANES - vec.shape[0]))[None, :]


def kernel(x, ffn1_norm, ffn1_w1, ffn1_w3, ffn1_w2, mix_norm, w_in, dn_conv, dn_a_log, dn_dt_bias, dn_out_norm, gla_gate_up, gla_gate_bias, gla_out_norm, swa_q_norm, swa_k_norm, swa_sinks, w_branch_dn, w_branch_gla, w_branch_swa, w_out, ffn2_norm, ffn2_w1, ffn2_w3, ffn2_w2):
    assert x.shape[0] == 1 and x.shape[2] == D_MODEL
    xs = x[0]
    wp = _relayout_w_in(jnp.swapaxes(w_in, 1, 2))
    for l in range(DEPTH):
        xs = _ffn(xs, ffn1_norm[l][None], ffn1_w1, ffn1_w3, ffn1_w2, l)
        u, h = _inproj(xs, mix_norm[l][None], wp, l)
        o_dn = _deltanet(u, dn_conv[l], _lane_row(dn_a_log[l]), _lane_row(dn_dt_bias[l]),
                         dn_out_norm[l][None])
        gate_up_pad = jnp.zeros((LANES, GLA_QK), BF16).at[SM_LR:SM_LR + GLA_RANK].set(
            _bf(gla_gate_up[l]))
        o_gla = _gla(u, gate_up_pad, gla_gate_bias[l][None], gla_out_norm[l][None])
        o_swa = _swa(u, jnp.tile(swa_q_norm[l], 2)[None], jnp.tile(swa_k_norm[l], 2)[None],
                     swa_sinks[l])
        y = _gatemix(h, o_dn, o_gla, o_swa, wp, l, _bf(w_branch_dn[l]), _bf(w_branch_gla[l]),
                     _bf(w_branch_swa[l]))
        xs = _outproj(xs, y, _bf(w_out[l]))
        xs = _ffn(xs, ffn2_norm[l][None], ffn2_w1, ffn2_w3, ffn2_w2, l)
    return xs[None]
```

```python
import functools

import numpy as np
import jax
import jax.numpy as jnp
from jax import lax
from jax.experimental import pallas as pl
from jax.experimental.pallas import tpu as pltpu

F32 = jnp.float32
BF16 = jnp.bfloat16

D_MODEL = 2048
DEPTH = 2
EPS = 1e-6
D_FF = 5632
FFN_RES_SCALE = 0.5
DN_HEADS, DN_DK, DN_DV, DN_CONV = 8, 128, 128, 4
GLA_HEADS, GLA_DK, GLA_DV, GLA_RANK, GLA_TAU = 4, 128, 256, 16, 16.0
SWA_HQ, SWA_HKV, SWA_HD, SWA_WINDOW = 16, 4, 64, 128
N_BRANCH = 3
CHUNK = 64
DN_QK = DN_HEADS * DN_DK
DN_WIDTH = DN_HEADS * DN_DV
GLA_QK = GLA_HEADS * GLA_DK
GLA_WIDTH = GLA_HEADS * GLA_DV
SWA_WIDTH = SWA_HQ * SWA_HD
SWA_KV = SWA_HKV * SWA_HD
SWA_SCALE = SWA_HD ** -0.5
assert SWA_SCALE == 2.0 ** -3

_SPLITS = (DN_QK, DN_QK, DN_WIDTH, DN_WIDTH, DN_HEADS, DN_HEADS,
           GLA_QK, GLA_QK, GLA_WIDTH, GLA_WIDTH, GLA_RANK,
           SWA_WIDTH, SWA_KV, SWA_KV, N_BRANCH * D_MODEL)
_OFFS = tuple(sum(_SPLITS[:i]) for i in range(len(_SPLITS) + 1))
(W_DNQ, W_DNK, W_DNV, W_DNZ, W_DNA, W_DNB, W_GLQ, W_GLK, W_GLV, W_GLR, W_GLLR,
 W_SWQ, W_SWK, W_SWV, W_GATE, W_END) = _OFFS

LANES = 128
SUBLANES = 8
PAIR = 2 * CHUNK

U_DN = 0
U_GLQ = U_DN + 4 * DN_QK
U_GLK = U_GLQ + GLA_QK
U_GLV = U_GLK + GLA_QK
U_GLR = U_GLV + GLA_WIDTH
U_SWQ = U_GLR + GLA_WIDTH
U_SWK = U_SWQ + SWA_WIDTH
U_SWV = U_SWK + SWA_KV
U_SMALL = U_SWV + SWA_KV
SM_A, SM_B, SM_LR = 0, DN_HEADS, 2 * DN_HEADS

RELAY_BW = 512
_RELAY_REGIONS = ((W_GATE, N_BRANCH * D_MODEL // RELAY_BW), (W_DNQ, 4 * DN_QK // RELAY_BW),
                  (W_GLQ, (2 * GLA_QK + 2 * GLA_WIDTH) // RELAY_BW),
                  (W_SWQ, (SWA_WIDTH + 2 * SWA_KV) // RELAY_BW))
RELAY_NBLK = sum(n for _, n in _RELAY_REGIONS) + 1
WP_GATE_COLS = N_BRANCH * D_MODEL
U_COLS = (RELAY_NBLK * RELAY_BW) - WP_GATE_COLS

VMEM_LIMIT = 60 * 1024 * 1024


def _cparams(n_axes):
    return pltpu.CompilerParams(dimension_semantics=("arbitrary",) * n_axes,
                                vmem_limit_bytes=VMEM_LIMIT)


def _bf(t):
    return t.astype(BF16)


def _mm(a, b):
    return jnp.dot(_bf(a), _bf(b), preferred_element_type=F32)


def _mm_nt(a, b):
    return lax.dot_general(_bf(a), _bf(b), (((1,), (1,)), ((), ())), preferred_element_type=F32)


def _mm_tn(a, b):
    return lax.dot_general(_bf(a), _bf(b), (((0,), (0,)), ((), ())), preferred_element_type=F32)


def _silu(t):
    return t * jax.nn.sigmoid(t)


def _rms_rows(x, gain):
    ms = jnp.mean(x * x, axis=-1, keepdims=True)
    return x * lax.rsqrt(ms + EPS) * gain


def _each(fn, *lists):
    return [fn(*args) for args in zip(*lists)]


def _chunk_cumsum(tri, tiles):
    hi = _each(_bf, tiles)
    r1 = _each(lambda t, h: t - h.astype(F32), tiles, hi)
    mid = _each(_bf, r1)
    lo = _each(lambda r, m: _bf(r - m.astype(F32)), r1, mid)
    dot = functools.partial(jnp.dot, tri, preferred_element_type=F32)
    return _each(lambda h, m, l: dot(h) + dot(m) + dot(l), hi, mid, lo)


def _pair_masks():
    sub = lax.broadcasted_iota(jnp.int32, (PAIR, PAIR), 0)
    lane = lax.broadcasted_iota(jnp.int32, (PAIR, PAIR), 1)
    same = (sub >= CHUNK) == (lane >= CHUNK)
    return sub, lane, same


def _ffn_kernel(x_ref, g_ref, w1_ref, w3_ref, w2_ref, o_ref, h_ref):
    @pl.when(pl.program_id(1) == 0)
    def _():
        x = x_ref[...]
        h_ref[...] = _bf(_rms_rows(x, g_ref[...]))
        o_ref[...] = x

    h = h_ref[...]
    a = jnp.dot(h, _bf(w1_ref[...]), preferred_element_type=F32)
    b = jnp.dot(h, _bf(w3_ref[...]), preferred_element_type=F32)
    act = _silu(a) * (b * FFN_RES_SCALE)
    o_ref[...] += jnp.dot(_bf(act), _bf(w2_ref[...]), preferred_element_type=F32)


def _ffn(x, gain, w1, w3, w2, layer, *, tm=1024, tf=512):
    t, d = x.shape
    f = w1.shape[2]
    return pl.pallas_call(
        _ffn_kernel,
        grid=(t // tm, f // tf),
        in_specs=[
            pl.BlockSpec((tm, d), lambda i, j: (i, 0), pipeline_mode=pl.Buffered(1)),
            pl.BlockSpec((1, d), lambda i, j: (0, 0)),
            pl.BlockSpec((None, d, tf), lambda i, j: (layer, 0, j)),
            pl.BlockSpec((None, d, tf), lambda i, j: (layer, 0, j)),
            pl.BlockSpec((None, tf, d), lambda i, j: (layer, j, 0)),
        ],
        out_specs=pl.BlockSpec((tm, d), lambda i, j: (i, 0)),
        out_shape=jax.ShapeDtypeStruct((t, d), F32),
        scratch_shapes=[pltpu.VMEM((tm, d), BF16)],
        compiler_params=_cparams(2),
        name="ffn",
    )(x, gain, w1, w3, w2)


def _inproj_kernel(x_ref, g_ref, w_ref, u_ref, h_ref):
    @pl.when(pl.program_id(1) == 0)
    def _():
        h_ref[...] = _bf(_rms_rows(x_ref[...], g_ref[...]))

    u_ref[...] = _mm_nt(h_ref[...], w_ref[...])


def _inproj(x, gain, wp, layer, *, tm=1024, tn=1536):
    t, d = x.shape
    assert WP_GATE_COLS % tn == 0 and U_COLS % tn == 0
    return pl.pallas_call(
        _inproj_kernel,
        grid=(t // tm, U_COLS // tn),
        in_specs=[
            pl.BlockSpec((tm, d), lambda i, j: (i, 0)),
            pl.BlockSpec((1, d), lambda i, j: (0, 0)),
            pl.BlockSpec((None, tn, d), lambda i, j: (layer, WP_GATE_COLS // tn + j, 0)),
        ],
        out_specs=[pl.BlockSpec((tm, tn), lambda i, j: (i, j)),
                   pl.BlockSpec((tm, d), lambda i, j: (i, 0))],
        out_shape=[jax.ShapeDtypeStruct((t, U_COLS), F32), jax.ShapeDtypeStruct((t, d), BF16)],
        compiler_params=_cparams(2),
        name="inproj",
    )(x, gain, wp)


def _relay_rows():
    rows = [src + k * RELAY_BW for src, nblk in _RELAY_REGIONS for k in range(nblk)]
    assert all(r % SUBLANES == 0 for r in rows) and len(rows) == RELAY_NBLK - 1
    return np.asarray(rows + [0], np.int32) // SUBLANES


def _relayout_kernel(tab_ref, src_ref, ab_ref, lr_ref, o_ref):
    last = pl.program_id(1) == RELAY_NBLK - 1

    @pl.when(jnp.logical_not(last))
    def _():
        o_ref[...] = _bf(src_ref[0])

    @pl.when(last)
    def _():
        n_ab, n_lr = ab_ref.shape[0], lr_ref.shape[0]
        o_ref[...] = jnp.zeros(o_ref.shape, o_ref.dtype)
        o_ref[SM_A:SM_A + n_ab, :] = _bf(ab_ref[...])
        o_ref[SM_LR:SM_LR + n_lr, :] = _bf(lr_ref[...])


def _relayout_w_in(w_t):
    n_layers, _, d = w_t.shape
    n_ab, n_lr = 2 * DN_HEADS, GLA_RANK
    assert W_DNA % n_ab == 0 and W_GLLR % n_lr == 0 and SM_LR == n_ab
    return pl.pallas_call(
        _relayout_kernel,
        grid_spec=pltpu.PrefetchScalarGridSpec(
            num_scalar_prefetch=1,
            grid=(n_layers, RELAY_NBLK),
            in_specs=[
                pl.BlockSpec((pl.Element(1), pl.Element(RELAY_BW), pl.Element(d)),
                             lambda l, j, tab: (l, tab[j] * SUBLANES, 0)),
                pl.BlockSpec((None, n_ab, d), lambda l, j, tab: (l, W_DNA // n_ab, 0)),
                pl.BlockSpec((None, n_lr, d), lambda l, j, tab: (l, W_GLLR // n_lr, 0)),
            ],
            out_specs=pl.BlockSpec((None, RELAY_BW, d), lambda l, j, tab: (l, j, 0)),
        ),
        out_shape=jax.ShapeDtypeStruct((n_layers, RELAY_NBLK * RELAY_BW, d), BF16),
        compiler_params=_cparams(2),
        name="relayout_w_in",
    )(jnp.asarray(_relay_rows()), w_t, w_t, w_t)


def _tri_inv(a, sub, lane, eye):
    bd16 = (sub >> 4) == (lane >> 4)
    bd32 = (sub >> 5) == (lane >> 5)
    off32 = bd32 & jnp.logical_not(bd16)
    a16 = _each(lambda t: _bf(jnp.where(bd16, t, 0.0)), a)
    x = _each(lambda t: eye - t.astype(F32), a16)
    p = a16
    for _ in range(3):
        p = _each(lambda t: _bf(_mm(t, t)), p)
        x = _each(lambda xt, pt: xt + _mm(xt, pt), x, p)
    for blk in (lambda t: jnp.where(off32, t, 0.0), lambda t: jnp.where(bd32, 0.0, t)):
        xb = _each(_bf, x)
        y = _each(lambda at, xt: _mm(blk(at), xt), a, xb)
        x = _each(lambda xt, xbt, yt: xt - _mm(xbt, yt), x, xb, y)
    return x


def _dn_kernel(q_ref, k_ref, v_ref, z_ref, pq_ref, pk_ref, pv_ref, sm_ref,
               cq_ref, ck_ref, cv_ref, alog_ref, dtb_ref, on_ref, o_ref,
               xs_ref, gc_s, gt_s, b_s, qe_s, ol_s, pm_s, qc_s, gl_s, o_s, s_ref,
               *, tb_rows, hs):
    tb = pl.program_id(0)
    hstep = pl.program_id(1)
    n_pairs = tb_rows // PAIR
    n_chunks = 2 * n_pairs
    heads = [hstep * hs + hh for hh in range(hs)]
    hcols = [slice(hh * LANES, (hh + 1) * LANES) for hh in range(hs)]

    @pl.when(tb == 0)
    def _():
        for head in heads:
            s_ref[head] = jnp.zeros((DN_DK, DN_DV), F32)

    sub, lane, same = _pair_masks()
    causal = same & (sub >= lane)
    strict = same & (sub > lane)
    eye = jnp.where(sub == lane, 1.0, 0.0).astype(F32)
    tri = jnp.where(causal, 1.0, 0.0).astype(BF16)
    first = sub[:, :1] < CHUNK
    rows = [slice(p * PAIR, (p + 1) * PAIR) for p in range(n_pairs)]
    halves = [slice(c * CHUNK, (c + 1) * CHUNK) for c in range(2)]

    @pl.when(hstep == 0)
    def _():
        sm = sm_ref[...]
        b_s[...] = jax.nn.sigmoid(sm)
        gs = -jnp.exp(alog_ref[...]) * jax.nn.softplus(sm + dtb_ref[...])
        gall = _chunk_cumsum(tri, [gs[r] for r in rows])
        for r, gt in zip(rows, gall):
            gc_s[r, :] = gt
            gt_s[r, :] = gt.T

    keep = (tb > 0).astype(F32)

    def conv_silu(x_ref, p_ref, w_ref):
        xs_ref[0:SUBLANES, :] = p_ref[...] * keep
        xs_ref[SUBLANES:, :] = x_ref[...]
        w = w_ref[...]
        y = w[DN_CONV - 1:DN_CONV, :] * xs_ref[pl.ds(SUBLANES, tb_rows), :]
        for kk in range(DN_CONV - 1):
            off = SUBLANES - (DN_CONV - 1) + kk
            y = y + w[kk:kk + 1, :] * xs_ref[pl.ds(off, tb_rows), :]
        return _silu(y)

    def l2norm(t):
        return t * lax.rsqrt(jnp.sum(t * t, axis=-1, keepdims=True) + EPS)

    qc = conv_silu(q_ref, pq_ref, cq_ref)
    kc = conv_silu(k_ref, pk_ref, ck_ref)
    vc = conv_silu(v_ref, pv_ref, cv_ref)
    qn = [l2norm(qc[:, c]) * (DN_DK ** -0.5) for c in hcols]
    kn = [l2norm(kc[:, c]) for c in hcols]

    tiles = [(hh, p) for hh in range(hs) for p in range(n_pairs)]
    q2 = [qn[hh][rows[p]] for hh, p in tiles]
    k2 = [kn[hh][rows[p]] for hh, p in tiles]
    v2 = [vc[rows[p], hcols[hh]] for hh, p in tiles]
    g = [jnp.sum(jnp.where(lane == heads[hh] + SM_A, gc_s[rows[p], :], 0.0), axis=1, keepdims=True)
         for hh, p in tiles]
    beta = [jnp.sum(jnp.where(lane == heads[hh] + SM_B, b_s[rows[p], :], 0.0), axis=1, keepdims=True)
            for hh, p in tiles]
    grow = [jnp.sum(jnp.where(sub == heads[hh] + SM_A, gt_s[rows[p], :], 0.0), axis=0, keepdims=True)
            for hh, p in tiles]
    decay = _each(lambda gc, gr: jnp.exp(jnp.where(causal, gc - gr, 0.0)), g, grow)
    eg = _each(jnp.exp, g)
    kb = _each(lambda kt, bt: kt * bt, k2, beta)
    k2b = _each(_bf, k2)
    a = _each(lambda kbt, kt, dt: jnp.where(strict, _mm_nt(kbt, kt) * dt, 0.0), kb, k2b, decay)
    at = _each(lambda qt, kt, dt: _bf(jnp.where(causal, _mm_nt(qt, kt) * dt, 0.0)), q2, k2b, decay)
    tm = _tri_inv(a, sub, lane, eye)
    rhs = _each(lambda vt, bt, kbt, egt: jnp.concatenate([vt * bt, kbt * egt], axis=1),
                v2, beta, kb, eg)
    sol = _each(_mm, tm, rhs)
    value = [_bf(t[:, :DN_DV]) for t in sol]
    kcum = [_bf(t[:, DN_DV:]) for t in sol]
    g_end = [[gt[CHUNK - 1:CHUNK, :], gt[PAIR - 1:PAIR, :]] for gt in g]
    kd = _each(lambda kt, gt, ge: _bf(kt * jnp.exp(jnp.where(first, ge[0], ge[1]) - gt)),
               k2, g, g_end)
    qe = _each(lambda qt, egt, att, kct: _bf(qt * egt - _mm(att, kct)), q2, eg, at, kcum)
    ol = _each(_mm, at, value)
    for t, (hh, p) in enumerate(tiles):
        qe_s[hh, rows[p], :] = qe[t]
        ol_s[hh, rows[p], :] = ol[t]
    for t, (hh, p) in enumerate(tiles):
        for c, sl in enumerate(halves):
            pm_s[hh * n_chunks + 2 * p + c] = _bf(-_mm_tn(kd[t][sl], kcum[t][sl]))
    for t, (hh, p) in enumerate(tiles):
        for c, sl in enumerate(halves):
            ci = hh * n_chunks + 2 * p + c
            qc_s[ci] = _mm_tn(kd[t][sl], value[t][sl])
            gl_s[ci:ci + 1, :] = jnp.broadcast_to(jnp.exp(g_end[t][c]), (1, LANES))

    def step(c, states):
        crow = pl.ds(pl.multiple_of(c * CHUNK, CHUNK), CHUNK)
        sb = [_bf(s) for s in states]
        for hh in range(hs):
            o_s[crow, hcols[hh]] = (jnp.dot(qe_s[hh, crow, :], sb[hh], preferred_element_type=F32)
                                    + ol_s[hh, crow, :])
        upd = [jnp.dot(pm_s[hh * n_chunks + c], sb[hh], preferred_element_type=F32)
               for hh in range(hs)]
        return tuple(gl_s[pl.ds(hh * n_chunks + c, 1), :] * states[hh] + upd[hh]
                     + qc_s[hh * n_chunks + c] for hh in range(hs))

    final = lax.fori_loop(0, n_chunks, step, tuple(s_ref[head] for head in heads), unroll=True)
    for head, s in zip(heads, final):
        s_ref[head] = s
    gain = on_ref[...]
    for c in hcols:
        o_ref[:, c] = _bf(_rms_rows(o_s[:, c], gain) * _silu(z_ref[:, c]))


def _deltanet(u, conv_w, alog_row, dtb_row, out_norm, *, tb_rows=256, hs=8):
    t = u.shape[0]
    width = hs * LANES
    nb = DN_QK // width
    prev_blk = tb_rows // SUBLANES
    assert DN_DK == LANES and DN_DV == LANES and DN_HEADS % hs == 0

    def col(seg):
        return lambda i, h: (i, U_DN // width + seg * nb + h)

    def prev(seg):
        return lambda i, h: (jnp.maximum(i * prev_blk - 1, 0), U_DN // width + seg * nb + h)

    def cw(seg):
        return lambda i, h: (0, seg * nb + h)

    row = pl.BlockSpec((1, LANES), lambda i, h: (0, 0))
    blk = lambda seg: pl.BlockSpec((tb_rows, width), col(seg))
    pblk = lambda seg: pl.BlockSpec((SUBLANES, width), prev(seg))
    cblk = lambda seg: pl.BlockSpec((DN_CONV, width), cw(seg))
    n_chunks = tb_rows // CHUNK
    rows_f32 = pltpu.VMEM((tb_rows, LANES), F32)
    return pl.pallas_call(
        functools.partial(_dn_kernel, tb_rows=tb_rows, hs=hs),
        grid=(t // tb_rows, DN_HEADS // hs),
        in_specs=[blk(0), blk(1), blk(2), blk(3), pblk(0), pblk(1), pblk(2),
                  pl.BlockSpec((tb_rows, LANES), lambda i, h: (i, U_SMALL // LANES)),
                  cblk(0), cblk(1), cblk(2), row, row, row],
        out_specs=pl.BlockSpec((tb_rows, width), lambda i, h: (i, h)),
        out_shape=jax.ShapeDtypeStruct((t, DN_WIDTH), BF16),
        scratch_shapes=[
            pltpu.VMEM((tb_rows + SUBLANES, width), F32),
            rows_f32, rows_f32, rows_f32,
            pltpu.VMEM((hs, tb_rows, DN_DK), BF16),
            pltpu.VMEM((hs, tb_rows, DN_DV), F32),
            pltpu.VMEM((hs * n_chunks, DN_DK, DN_DK), BF16),
            pltpu.VMEM((hs * n_chunks, DN_DK, DN_DV), F32),
            pltpu.VMEM((hs * n_chunks, LANES), F32),
            pltpu.VMEM((tb_rows, width), F32),
            pltpu.VMEM((DN_HEADS, DN_DK, DN_DV), F32),
        ],
        compiler_params=_cparams(2),
        name="deltanet",
    )(u, u, u, u, u, u, u, u, conv_w, conv_w, conv_w, alog_row, dtb_row, out_norm)


def _gla_kernel(q_ref, k_ref, v_ref, r_ref, sm_ref, gu_ref, gb_ref, on_ref, o_ref,
                qg_s, oi_s, kv_s, gl_s, st_ref, *, tb_rows, hs):
    hstep = pl.program_id(1)
    n_pairs = tb_rows // PAIR
    n_chunks = 2 * n_pairs
    heads = [hstep * hs + hh for hh in range(hs)]
    kcols = [slice(hh * GLA_DK, (hh + 1) * GLA_DK) for hh in range(hs)]
    vcols = [slice(hh * GLA_DV, (hh + 1) * GLA_DV) for hh in range(hs)]

    @pl.when(pl.program_id(0) == 0)
    def _():
        for head in heads:
            st_ref[head] = jnp.zeros((GLA_DV, GLA_DK), F32)

    logits = jnp.dot(_bf(sm_ref[...]), gu_ref[...], preferred_element_type=F32) + gb_ref[...]
    la = jax.nn.log_sigmoid(logits) * (1.0 / GLA_TAU)

    sub, lane, same = _pair_masks()
    causal = same & (sub >= lane)
    tri = jnp.where(causal, 1.0, 0.0).astype(BF16)
    first = sub < CHUNK
    gain = on_ref[...]

    rows = [slice(p * PAIR, (p + 1) * PAIR) for p in range(n_pairs)]
    halves = [slice(c * CHUNK, (c + 1) * CHUNK) for c in range(2)]
    tiles = [(hh, p) for hh in range(hs) for p in range(n_pairs)]
    g = _chunk_cumsum(tri, [la[rows[p], kcols[hh]] for hh, p in tiles])
    k2 = [k_ref[rows[p], kcols[hh]] for hh, p in tiles]
    v2 = [_bf(v_ref[rows[p], vcols[hh]]) for hh, p in tiles]
    qg = [_bf(q_ref[rows[p], kcols[hh]] * (GLA_DK ** -0.5) * jnp.exp(gt))
          for (hh, p), gt in zip(tiles, g)]
    kg = _each(lambda kt, gt: _bf(kt * jnp.exp(-gt)), k2, g)
    a = _each(lambda qt, kt: _bf(jnp.where(causal, _mm_nt(qt, kt), 0.0)), qg, kg)
    oi = _each(_mm, a, v2)
    g_end = [[gt[CHUNK - 1:CHUNK, :], gt[PAIR - 1:PAIR, :]] for gt in g]
    kd = _each(lambda kt, gt, ge: _bf(kt * jnp.exp(jnp.where(first, ge[0], ge[1]) - gt)),
               k2, g, g_end)
    for t, (hh, p) in enumerate(tiles):
        qg_s[hh, rows[p], :] = qg[t]
        oi_s[hh, rows[p], :] = oi[t]
        for c, sl in enumerate(halves):
            ci = hh * n_chunks + 2 * p + c
            kv_s[ci] = _mm_tn(v2[t][sl], kd[t][sl])
            gl_s[ci:ci + 1, :] = jnp.exp(g_end[t][c])

    def step(c, states):
        rws = pl.ds(pl.multiple_of(c * CHUNK, CHUNK), CHUNK)
        for hh in range(hs):
            o_c = _mm_nt(qg_s[hh, rws, :], states[hh]) + oi_s[hh, rws, :]
            o_ref[rws, vcols[hh]] = _bf(_rms_rows(o_c, gain) * _silu(r_ref[rws, vcols[hh]]))
        return tuple(states[hh] * gl_s[pl.ds(hh * n_chunks + c, 1), :] + kv_s[hh * n_chunks + c]
                     for hh in range(hs))

    final = lax.fori_loop(0, n_chunks, step, tuple(st_ref[head] for head in heads), unroll=True)
    for head, st in zip(heads, final):
        st_ref[head] = st


def _gla(u, gate_up_pad, gate_bias, out_norm, *, tb_rows=512, hs=4):
    t = u.shape[0]
    kw, vw = hs * GLA_DK, hs * GLA_DV
    assert GLA_HEADS % hs == 0
    n_chunks = tb_rows // CHUNK
    return pl.pallas_call(
        functools.partial(_gla_kernel, tb_rows=tb_rows, hs=hs),
        grid=(t // tb_rows, GLA_HEADS // hs),
        in_specs=[
            pl.BlockSpec((tb_rows, kw), lambda i, h: (i, U_GLQ // kw + h)),
            pl.BlockSpec((tb_rows, kw), lambda i, h: (i, U_GLK // kw + h)),
            pl.BlockSpec((tb_rows, vw), lambda i, h: (i, U_GLV // vw + h)),
            pl.BlockSpec((tb_rows, vw), lambda i, h: (i, U_GLR // vw + h)),
            pl.BlockSpec((tb_rows, LANES), lambda i, h: (i, U_SMALL // LANES)),
            pl.BlockSpec((LANES, kw), lambda i, h: (0, h)),
            pl.BlockSpec((1, kw), lambda i, h: (0, h)),
            pl.BlockSpec((1, GLA_DV), lambda i, h: (0, 0)),
        ],
        out_specs=pl.BlockSpec((tb_rows, vw), lambda i, h: (i, h)),
        out_shape=jax.ShapeDtypeStruct((t, GLA_WIDTH), BF16),
        scratch_shapes=[
            pltpu.VMEM((hs, tb_rows, GLA_DK), BF16),
            pltpu.VMEM((hs, tb_rows, GLA_DV), F32),
            pltpu.VMEM((hs * n_chunks, GLA_DV, GLA_DK), F32),
            pltpu.VMEM((hs * n_chunks, GLA_DK), F32),
            pltpu.VMEM((GLA_HEADS, GLA_DV, GLA_DK), F32),
        ],
        compiler_params=_cparams(2),
        name="gla",
    )(u, u, u, u, u, gate_up_pad, gate_bias, out_norm)


def _swa_kernel(q_ref, kc_ref, kp_ref, vc_ref, vp_ref, qg_ref, kg_ref, sink_ref, o_ref):
    n = pl.program_id(0)
    w = SWA_WINDOW
    lane = lax.broadcasted_iota(jnp.int32, (w, LANES), 1)
    lo = lane < SWA_HD
    lane2 = lax.broadcasted_iota(jnp.int32, (2 * w, LANES), 1)
    lo2 = lane2 < SWA_HD
    qi = lax.broadcasted_iota(jnp.int32, (w, 2 * w), 0)
    ki = lax.broadcasted_iota(jnp.int32, (w, 2 * w), 1)
    mask = (ki <= qi + w) & (ki > qi + w - SWA_WINDOW) & ((n > 0) | (ki >= w))
    kk = jnp.concatenate([kp_ref[...], kc_ref[...]], axis=0)
    vv = jnp.concatenate([vp_ref[...], vc_ref[...]], axis=0)
    qgain = qg_ref[...]
    kgain = kg_ref[...]
    heads_per_grp = SWA_HQ // SWA_HKV
    pcols = [slice(p * LANES, (p + 1) * LANES) for p in range(SWA_HQ // 2)]

    def head_halves(t, g):
        tile = t[:, (g // 2) * LANES:(g // 2 + 1) * LANES]
        swapped = pltpu.roll(tile, SWA_HD, axis=1)
        return (tile, swapped) if g % 2 == 0 else (swapped, tile)

    def norm_k(kg):
        ms = jnp.sum(kg * kg, axis=-1, keepdims=True) * (1.0 / LANES)
        return _bf(kg * lax.rsqrt(ms + EPS) * kgain)

    def norm_q(qp):
        sq = qp * qp
        ms_lo = jnp.sum(jnp.where(lo, sq, 0.0), axis=-1, keepdims=True) * (1.0 / SWA_HD)
        ms_hi = jnp.sum(jnp.where(lo, 0.0, sq), axis=-1, keepdims=True) * (1.0 / SWA_HD)
        qn = qp * jnp.where(lo, lax.rsqrt(ms_lo + EPS), lax.rsqrt(ms_hi + EPS)) * qgain
        return qn * SWA_SCALE

    def probs(s, sink):
        s = jnp.where(mask, s, -jnp.inf)
        m = jnp.maximum(jnp.max(s, axis=-1, keepdims=True), sink)
        p = jnp.exp(s - m)
        inv = 1.0 / (jnp.sum(p, axis=-1, keepdims=True) + jnp.exp(sink - m))
        return _bf(p * inv)

    groups = range(SWA_HKV)
    k_lo_hi = [head_halves(kk, g) for g in groups]
    kn = [norm_k(jnp.where(lo2, k_lo, k_hi)) for k_lo, k_hi in k_lo_hi]
    v_lo_hi = [head_halves(vv, g) for g in groups]
    v_half = [(_bf(jnp.where(lo2, v_lo, 0.0)), _bf(jnp.where(lo2, 0.0, v_hi)))
              for v_lo, v_hi in v_lo_hi]
    qn = [norm_q(q_ref[:, c]) for c in pcols]
    heads = range(SWA_HQ)
    qm = [_bf(jnp.where(lo, qn[h // 2], 0.0) if h % 2 == 0 else jnp.where(lo, 0.0, qn[h // 2]))
          for h in heads]
    s = [_mm_nt(qm[h], kn[h // heads_per_grp]) for h in heads]
    p = [probs(s[h], sink_ref[h]) for h in heads]
    part = [jnp.dot(p[h], v_half[h // heads_per_grp][h % 2], preferred_element_type=F32)
            for h in heads]
    for pr, c in enumerate(pcols):
        o_ref[:, c] = _bf(part[2 * pr] + part[2 * pr + 1])


def _swa(u, qgain, kgain, sinks):
    t = u.shape[0]
    w = SWA_WINDOW
    kvw = SWA_KV
    assert U_SWK % kvw == 0 and U_SWV % kvw == 0 and U_SWQ % SWA_WIDTH == 0
    cur = lambda c: (lambda n: (n, c))
    prv = lambda c: (lambda n: (jnp.maximum(n - 1, 0), c))
    row = pl.BlockSpec((1, LANES), lambda n: (0, 0))
    return pl.pallas_call(
        _swa_kernel,
        grid=(t // w,),
        in_specs=[
            pl.BlockSpec((w, SWA_WIDTH), cur(U_SWQ // SWA_WIDTH)),
            pl.BlockSpec((w, kvw), cur(U_SWK // kvw)),
            pl.BlockSpec((w, kvw), prv(U_SWK // kvw)),
            pl.BlockSpec((w, kvw), cur(U_SWV // kvw)),
            pl.BlockSpec((w, kvw), prv(U_SWV // kvw)),
            row, row,
            pl.BlockSpec(memory_space=pltpu.SMEM),
        ],
        out_specs=pl.BlockSpec((w, SWA_WIDTH), lambda n: (n, 0)),
        out_shape=jax.ShapeDtypeStruct((t, SWA_WIDTH), BF16),
        compiler_params=_cparams(1),
        name="swa",
    )(u, u, u, u, u, qgain, kgain, sinks)


def _gatemix_kernel(h_ref, od_ref, og_ref, os_ref, g0_ref, g1_ref, g2_ref,
                    wd_ref, wg_ref, ws_ref, y_ref):
    dot = functools.partial(jnp.dot, preferred_element_type=F32)
    h = h_ref[...]
    y = (jax.nn.sigmoid(_mm_nt(h, g0_ref[...])) * dot(od_ref[...], wd_ref[...])
         + jax.nn.sigmoid(_mm_nt(h, g1_ref[...])) * dot(og_ref[...], wg_ref[...])
         + jax.nn.sigmoid(_mm_nt(h, g2_ref[...])) * dot(os_ref[...], ws_ref[...]))
    y_ref[...] = _bf(y)


def _gatemix(h, o_dn, o_gla, o_swa, wp, layer, w_dn, w_gla, w_swa, *, tm=1024, tj=512):
    t, d = h.shape
    nj = d // tj
    gate = lambda b: (lambda j, i: (layer, b * nj + j, 0))
    act = lambda width: pl.BlockSpec((tm, width), lambda j, i: (i, 0))
    wcol = lambda width: pl.BlockSpec((width, tj), lambda j, i: (0, j))
    return pl.pallas_call(
        _gatemix_kernel,
        grid=(nj, t // tm),
        in_specs=[
            act(d), act(DN_WIDTH), act(GLA_WIDTH), act(SWA_WIDTH),
            pl.BlockSpec((None, tj, d), gate(0)), pl.BlockSpec((None, tj, d), gate(1)),
            pl.BlockSpec((None, tj, d), gate(2)),
            wcol(DN_WIDTH), wcol(GLA_WIDTH), wcol(SWA_WIDTH),
        ],
        out_specs=pl.BlockSpec((tm, tj), lambda j, i: (i, j)),
        out_shape=jax.ShapeDtypeStruct((t, d), BF16),
        compiler_params=_cparams(2),
        name="gatemix",
    )(h, o_dn, o_gla, o_swa, wp, wp, wp, w_dn, w_gla, w_swa)


def _outproj_kernel(x_ref, y_ref, w_ref, o_ref):
    o_ref[...] = x_ref[...] + jnp.dot(y_ref[...], w_ref[...], preferred_element_type=F32)


def _outproj(x, y, w_o, *, tm=1024):
    t, d = x.shape
    return pl.pallas_call(
        _outproj_kernel,
        grid=(t // tm,),
        in_specs=[
            pl.BlockSpec((tm, d), lambda i: (i, 0)),
            pl.BlockSpec((tm, d), lambda i: (i, 0)),
            pl.BlockSpec((d, d), lambda i: (0, 0)),
        ],
        out_specs=pl.BlockSpec((tm, d), lambda i: (i, 0)),
        out_shape=jax.ShapeDtypeStruct((t, d), F32),
        compiler_params=_cparams(1),
        name="outproj",
    )(x, y, w_o)


def _lane_row(vec):
    return jnp.pad(vec.astype(F32), (0, LANES - vec.shape[0]))[None, :]


def kernel(x, ffn1_norm, ffn1_w1, ffn1_w3, ffn1_w2, mix_norm, w_in, dn_conv, dn_a_log, dn_dt_bias, dn_out_norm, gla_gate_up, gla_gate_bias, gla_out_norm, swa_q_norm, swa_k_norm, swa_sinks, w_branch_dn, w_branch_gla, w_branch_swa, w_out, ffn2_norm, ffn2_w1, ffn2_w3, ffn2_w2):
    assert x.shape[0] == 1 and x.shape[2] == D_MODEL
    xs = x[0]
    wp = _relayout_w_in(jnp.swapaxes(w_in, 1, 2))
    for l in range(DEPTH):
        xs = _ffn(xs, ffn1_norm[l][None], ffn1_w1, ffn1_w3, ffn1_w2, l)
        u, h = _inproj(xs, mix_norm[l][None], wp, l)
        o_dn = _deltanet(u, dn_conv[l], _lane_row(dn_a_log[l]), _lane_row(dn_dt_bias[l]),
                         dn_out_norm[l][None])
        gate_up_pad = jnp.zeros((LANES, GLA_QK), BF16).at[SM_LR:SM_LR + GLA_RANK].set(
            _bf(gla_gate_up[l]))
        o_gla = _gla(u, gate_up_pad, gla_gate_bias[l][None], gla_out_norm[l][None])
        o_swa = _swa(u, jnp.tile(swa_q_norm[l], 2)[None], jnp.tile(swa_k_norm[l], 2)[None],
                     swa_sinks[l])
        y = _gatemix(h, o_dn, o_gla, o_swa, wp, l, _bf(w_branch_dn[l]), _bf(w_branch_gla[l]),
                     _bf(w_branch_swa[l]))
        xs = _outproj(xs, y, _bf(w_out[l]))
        xs = _ffn(xs, ffn2_norm[l][None], ffn2_w1, ffn2_w3, ffn2_w2, l)
    return xs[None]
```

```python
import functools

import numpy as np
import jax
import jax.numpy as jnp
from jax import lax
from jax.experimental import pallas as pl
from jax.experimental.pallas import tpu as pltpu

F32 = jnp.float32
BF16 = jnp.bfloat16

D_MODEL = 2048
DEPTH = 2
EPS = 1e-6
D_FF = 5632
FFN_RES_SCALE = 0.5
DN_HEADS, DN_DK, DN_DV, DN_CONV = 8, 128, 128, 4
GLA_HEADS, GLA_DK, GLA_DV, GLA_RANK, GLA_TAU = 4, 128, 256, 16, 16.0
SWA_HQ, SWA_HKV, SWA_HD, SWA_WINDOW = 16, 4, 64, 128
N_BRANCH = 3
CHUNK = 64
DN_QK = DN_HEADS * DN_DK
DN_WIDTH = DN_HEADS * DN_DV
GLA_QK = GLA_HEADS * GLA_DK
GLA_WIDTH = GLA_HEADS * GLA_DV
SWA_WIDTH = SWA_HQ * SWA_HD
SWA_KV = SWA_HKV * SWA_HD
SWA_SCALE = SWA_HD ** -0.5
assert SWA_SCALE == 2.0 ** -3

_SPLITS = (DN_QK, DN_QK, DN_WIDTH, DN_WIDTH, DN_HEADS, DN_HEADS,
           GLA_QK, GLA_QK, GLA_WIDTH, GLA_WIDTH, GLA_RANK,
           SWA_WIDTH, SWA_KV, SWA_KV, N_BRANCH * D_MODEL)
_OFFS = tuple(sum(_SPLITS[:i]) for i in range(len(_SPLITS) + 1))
(W_DNQ, W_DNK, W_DNV, W_DNZ, W_DNA, W_DNB, W_GLQ, W_GLK, W_GLV, W_GLR, W_GLLR,
 W_SWQ, W_SWK, W_SWV, W_GATE, W_END) = _OFFS

LANES = 128
SUBLANES = 8
PAIR = 2 * CHUNK

U_DN = 0
U_GLQ = U_DN + 4 * DN_QK
U_GLK = U_GLQ + GLA_QK
U_GLV = U_GLK + GLA_QK
U_GLR = U_GLV + GLA_WIDTH
U_SWQ = U_GLR + GLA_WIDTH
U_SWK = U_SWQ + SWA_WIDTH
U_SWV = U_SWK + SWA_KV
U_SMALL = U_SWV + SWA_KV
SM_A, SM_B, SM_LR = 0, DN_HEADS, 2 * DN_HEADS

RELAY_BW = 512
_RELAY_REGIONS = ((W_GATE, N_BRANCH * D_MODEL // RELAY_BW), (W_DNQ, 4 * DN_QK // RELAY_BW),
                  (W_GLQ, (2 * GLA_QK + 2 * GLA_WIDTH) // RELAY_BW),
                  (W_SWQ, (SWA_WIDTH + 2 * SWA_KV) // RELAY_BW))
RELAY_NBLK = sum(n for _, n in _RELAY_REGIONS) + 1
WP_GATE_COLS = N_BRANCH * D_MODEL
U_COLS = (RELAY_NBLK * RELAY_BW) - WP_GATE_COLS

VMEM_LIMIT = 60 * 1024 * 1024


def _cparams(n_axes):
    return pltpu.CompilerParams(dimension_semantics=("arbitrary",) * n_axes,
                                vmem_limit_bytes=VMEM_LIMIT)


def _bf(t):
    return t.astype(BF16)


def _mm(a, b):
    return jnp.dot(_bf(a), _bf(b), preferred_element_type=F32)


def _mm_nt(a, b):
    return lax.dot_general(_bf(a), _bf(b), (((1,), (1,)), ((), ())), preferred_element_type=F32)


def _mm_tn(a, b):
    return lax.dot_general(_bf(a), _bf(b), (((0,), (0,)), ((), ())), preferred_element_type=F32)


def _silu(t):
    return t * jax.nn.sigmoid(t)


def _rms_rows(x, gain):
    ms = jnp.mean(x * x, axis=-1, keepdims=True)
    return x * lax.rsqrt(ms + EPS) * gain


def _each(fn, *lists):
    return [fn(*args) for args in zip(*lists)]


def _chunk_cumsum(tri, tiles):
    hi = _each(_bf, tiles)
    r1 = _each(lambda t, h: t - h.astype(F32), tiles, hi)
    mid = _each(_bf, r1)
    lo = _each(lambda r, m: _bf(r - m.astype(F32)), r1, mid)
    dot = functools.partial(jnp.dot, tri, preferred_element_type=F32)
    return _each(lambda h, m, l: dot(h) + dot(m) + dot(l), hi, mid, lo)


def _pair_masks():
    sub = lax.broadcasted_iota(jnp.int32, (PAIR, PAIR), 0)
    lane = lax.broadcasted_iota(jnp.int32, (PAIR, PAIR), 1)
    same = (sub >= CHUNK) == (lane >= CHUNK)
    return sub, lane, same


def _ffn_kernel(x_ref, g_ref, w1_ref, w3_ref, w2_ref, o_ref, h_ref):
    @pl.when(pl.program_id(1) == 0)
    def _():
        x = x_ref[...]
        h_ref[...] = _bf(_rms_rows(x, g_ref[...]))
        o_ref[...] = x

    h = h_ref[...]
    dot = functools.partial(jnp.dot, preferred_element_type=F32)
    half = w1_ref.shape[1] // 2
    cols = [slice(0, half), slice(half, 2 * half)]
    ab = [(dot(h, _bf(w1_ref[:, c])), dot(h, _bf(w3_ref[:, c]))) for c in cols]
    act = [_bf(_silu(a) * (b * FFN_RES_SCALE)) for a, b in ab]
    o_ref[...] += dot(act[0], _bf(w2_ref[cols[0], :])) + dot(act[1], _bf(w2_ref[cols[1], :]))


def _ffn(x, gain, w1, w3, w2, layer, *, tm=1024, tf=512):
    t, d = x.shape
    f = w1.shape[2]
    return pl.pallas_call(
        _ffn_kernel,
        grid=(t // tm, f // tf),
        in_specs=[
            pl.BlockSpec((tm, d), lambda i, j: (i, 0), pipeline_mode=pl.Buffered(1)),
            pl.BlockSpec((1, d), lambda i, j: (0, 0)),
            pl.BlockSpec((None, d, tf), lambda i, j: (layer, 0, j)),
            pl.BlockSpec((None, d, tf), lambda i, j: (layer, 0, j)),
            pl.BlockSpec((None, tf, d), lambda i, j: (layer, j, 0)),
        ],
        out_specs=pl.BlockSpec((tm, d), lambda i, j: (i, 0)),
        out_shape=jax.ShapeDtypeStruct((t, d), F32),
        scratch_shapes=[pltpu.VMEM((tm, d), BF16)],
        compiler_params=_cparams(2),
        name="ffn",
    )(x, gain, w1, w3, w2)


def _inproj_kernel(x_ref, g_ref, w_ref, u_ref, h_ref):
    @pl.when(pl.program_id(1) == 0)
    def _():
        h_ref[...] = _bf(_rms_rows(x_ref[...], g_ref[...]))

    u_ref[...] = _mm_nt(h_ref[...], w_ref[...])


def _inproj(x, gain, wp, layer, *, tm=1024, tn=1536):
    t, d = x.shape
    assert WP_GATE_COLS % tn == 0 and U_COLS % tn == 0
    return pl.pallas_call(
        _inproj_kernel,
        grid=(t // tm, U_COLS // tn),
        in_specs=[
            pl.BlockSpec((tm, d), lambda i, j: (i, 0)),
            pl.BlockSpec((1, d), lambda i, j: (0, 0)),
            pl.BlockSpec((None, tn, d), lambda i, j: (layer, WP_GATE_COLS // tn + j, 0)),
        ],
        out_specs=[pl.BlockSpec((tm, tn), lambda i, j: (i, j)),
                   pl.BlockSpec((tm, d), lambda i, j: (i, 0))],
        out_shape=[jax.ShapeDtypeStruct((t, U_COLS), F32), jax.ShapeDtypeStruct((t, d), BF16)],
        compiler_params=_cparams(2),
        name="inproj",
    )(x, gain, wp)


def _relay_rows():
    rows = [src + k * RELAY_BW for src, nblk in _RELAY_REGIONS for k in range(nblk)]
    assert all(r % SUBLANES == 0 for r in rows) and len(rows) == RELAY_NBLK - 1
    return np.asarray(rows + [0], np.int32) // SUBLANES


def _relayout_kernel(tab_ref, src_ref, ab_ref, lr_ref, o_ref):
    last = pl.program_id(1) == RELAY_NBLK - 1

    @pl.when(jnp.logical_not(last))
    def _():
        o_ref[...] = _bf(src_ref[0])

    @pl.when(last)
    def _():
        n_ab, n_lr = ab_ref.shape[0], lr_ref.shape[0]
        o_ref[...] = jnp.zeros(o_ref.shape, o_ref.dtype)
        o_ref[SM_A:SM_A + n_ab, :] = _bf(ab_ref[...])
        o_ref[SM_LR:SM_LR + n_lr, :] = _bf(lr_ref[...])


def _relayout_w_in(w_t):
    n_layers, _, d = w_t.shape
    n_ab, n_lr = 2 * DN_HEADS, GLA_RANK
    assert W_DNA % n_ab == 0 and W_GLLR % n_lr == 0 and SM_LR == n_ab
    return pl.pallas_call(
        _relayout_kernel,
        grid_spec=pltpu.PrefetchScalarGridSpec(
            num_scalar_prefetch=1,
            grid=(n_layers, RELAY_NBLK),
            in_specs=[
                pl.BlockSpec((pl.Element(1), pl.Element(RELAY_BW), pl.Element(d)),
                             lambda l, j, tab: (l, tab[j] * SUBLANES, 0)),
                pl.BlockSpec((None, n_ab, d), lambda l, j, tab: (l, W_DNA // n_ab, 0)),
                pl.BlockSpec((None, n_lr, d), lambda l, j, tab: (l, W_GLLR // n_lr, 0)),
            ],
            out_specs=pl.BlockSpec((None, RELAY_BW, d), lambda l, j, tab: (l, j, 0)),
        ),
        out_shape=jax.ShapeDtypeStruct((n_layers, RELAY_NBLK * RELAY_BW, d), BF16),
        compiler_params=_cparams(2),
        name="relayout_w_in",
    )(jnp.asarray(_relay_rows()), w_t, w_t, w_t)


def _tri_inv(a, sub, lane, eye):
    bd16 = (sub >> 4) == (lane >> 4)
    bd32 = (sub >> 5) == (lane >> 5)
    off32 = bd32 & jnp.logical_not(bd16)
    a16 = _each(lambda t: _bf(jnp.where(bd16, t, 0.0)), a)
    x = _each(lambda t: eye - t.astype(F32), a16)
    p = a16
    for _ in range(3):
        p = _each(lambda t: _bf(_mm(t, t)), p)
        x = _each(lambda xt, pt: xt + _mm(xt, pt), x, p)
    for blk in (lambda t: jnp.where(off32, t, 0.0), lambda t: jnp.where(bd32, 0.0, t)):
        xb = _each(_bf, x)
        y = _each(lambda at, xt: _mm(blk(at), xt), a, xb)
        x = _each(lambda xt, xbt, yt: xt - _mm(xbt, yt), x, xb, y)
    return x


def _dn_kernel(q_ref, k_ref, v_ref, z_ref, pq_ref, pk_ref, pv_ref, sm_ref,
               cq_ref, ck_ref, cv_ref, alog_ref, dtb_ref, on_ref, o_ref,
               xs_ref, gc_s, gt_s, b_s, qe_s, ol_s, pm_s, qc_s, gl_s, o_s, s_ref,
               *, tb_rows, hs):
    tb = pl.program_id(0)
    hstep = pl.program_id(1)
    n_pairs = tb_rows // PAIR
    n_chunks = 2 * n_pairs
    heads = [hstep * hs + hh for hh in range(hs)]
    hcols = [slice(hh * LANES, (hh + 1) * LANES) for hh in range(hs)]

    @pl.when(tb == 0)
    def _():
        for head in heads:
            s_ref[head] = jnp.zeros((DN_DK, DN_DV), F32)

    sub, lane, same = _pair_masks()
    causal = same & (sub >= lane)
    strict = same & (sub > lane)
    eye = jnp.where(sub == lane, 1.0, 0.0).astype(F32)
    tri = jnp.where(causal, 1.0, 0.0).astype(BF16)
    first = sub[:, :1] < CHUNK
    rows = [slice(p * PAIR, (p + 1) * PAIR) for p in range(n_pairs)]
    halves = [slice(c * CHUNK, (c + 1) * CHUNK) for c in range(2)]

    @pl.when(hstep == 0)
    def _():
        sm = sm_ref[...]
        b_s[...] = jax.nn.sigmoid(sm)
        gs = -jnp.exp(alog_ref[...]) * jax.nn.softplus(sm + dtb_ref[...])
        gall = _chunk_cumsum(tri, [gs[r] for r in rows])
        for r, gt in zip(rows, gall):
            gc_s[r, :] = gt
            gt_s[r, :] = gt.T

    keep = (tb > 0).astype(F32)

    def conv_silu(x_ref, p_ref, w_ref):
        xs_ref[0:SUBLANES, :] = p_ref[...] * keep
        xs_ref[SUBLANES:, :] = x_ref[...]
        w = w_ref[...]
        y = w[DN_CONV - 1:DN_CONV, :] * xs_ref[pl.ds(SUBLANES, tb_rows), :]
        for kk in range(DN_CONV - 1):
            off = SUBLANES - (DN_CONV - 1) + kk
            y = y + w[kk:kk + 1, :] * xs_ref[pl.ds(off, tb_rows), :]
        return _silu(y)

    def l2norm(t):
        return t * lax.rsqrt(jnp.sum(t * t, axis=-1, keepdims=True) + EPS)

    qc = conv_silu(q_ref, pq_ref, cq_ref)
    kc = conv_silu(k_ref, pk_ref, ck_ref)
    vc = conv_silu(v_ref, pv_ref, cv_ref)
    qn = [l2norm(qc[:, c]) * (DN_DK ** -0.5) for c in hcols]
    kn = [l2norm(kc[:, c]) for c in hcols]

    tiles = [(hh, p) for hh in range(hs) for p in range(n_pairs)]
    q2 = [qn[hh][rows[p]] for hh, p in tiles]
    k2 = [kn[hh][rows[p]] for hh, p in tiles]
    v2 = [vc[rows[p], hcols[hh]] for hh, p in tiles]
    g = [jnp.sum(jnp.where(lane == heads[hh] + SM_A, gc_s[rows[p], :], 0.0), axis=1, keepdims=True)
         for hh, p in tiles]
    beta = [jnp.sum(jnp.where(lane == heads[hh] + SM_B, b_s[rows[p], :], 0.0), axis=1, keepdims=True)
            for hh, p in tiles]
    grow = [jnp.sum(jnp.where(sub == heads[hh] + SM_A, gt_s[rows[p], :], 0.0), axis=0, keepdims=True)
            for hh, p in tiles]
    decay = _each(lambda gc, gr: jnp.exp(jnp.where(causal, gc - gr, 0.0)), g, grow)
    eg = _each(jnp.exp, g)
    kb = _each(lambda kt, bt: kt * bt, k2, beta)
    k2b = _each(_bf, k2)
    a = _each(lambda kbt, kt, dt: jnp.where(strict, _mm_nt(kbt, kt) * dt, 0.0), kb, k2b, decay)
    at = _each(lambda qt, kt, dt: _bf(jnp.where(causal, _mm_nt(qt, kt) * dt, 0.0)), q2, k2b, decay)
    tm = _tri_inv(a, sub, lane, eye)
    rhs = _each(lambda vt, bt, kbt, egt: jnp.concatenate([vt * bt, kbt * egt], axis=1),
                v2, beta, kb, eg)
    sol = _each(_mm, tm, rhs)
    value = [_bf(t[:, :DN_DV]) for t in sol]
    kcum = [_bf(t[:, DN_DV:]) for t in sol]
    g_end = [[gt[CHUNK - 1:CHUNK, :], gt[PAIR - 1:PAIR, :]] for gt in g]
    kd = _each(lambda kt, gt, ge: _bf(kt * jnp.exp(jnp.where(first, ge[0], ge[1]) - gt)),
               k2, g, g_end)
    qe = _each(lambda qt, egt, att, kct: _bf(qt * egt - _mm(att, kct)), q2, eg, at, kcum)
    ol = _each(_mm, at, value)
    for t, (hh, p) in enumerate(tiles):
        qe_s[hh, rows[p], :] = qe[t]
        ol_s[hh, rows[p], :] = ol[t]
    for t, (hh, p) in enumerate(tiles):
        for c, sl in enumerate(halves):
            pm_s[hh * n_chunks + 2 * p + c] = _bf(-_mm_tn(kd[t][sl], kcum[t][sl]))
    for t, (hh, p) in enumerate(tiles):
        for c, sl in enumerate(halves):
            ci = hh * n_chunks + 2 * p + c
            qc_s[ci] = _mm_tn(kd[t][sl], value[t][sl])
            gl_s[ci:ci + 1, :] = jnp.broadcast_to(jnp.exp(g_end[t][c]), (1, LANES))

    def step(c, states):
        crow = pl.ds(pl.multiple_of(c * CHUNK, CHUNK), CHUNK)
        sb = [_bf(s) for s in states]
        for hh in range(hs):
            o_s[crow, hcols[hh]] = (jnp.dot(qe_s[hh, crow, :], sb[hh], preferred_element_type=F32)
                                    + ol_s[hh, crow, :])
        upd = [jnp.dot(pm_s[hh * n_chunks + c], sb[hh], preferred_element_type=F32)
               for hh in range(hs)]
        return tuple(gl_s[pl.ds(hh * n_chunks + c, 1), :] * states[hh] + upd[hh]
                     + qc_s[hh * n_chunks + c] for hh in range(hs))

    final = lax.fori_loop(0, n_chunks, step, tuple(s_ref[head] for head in heads), unroll=True)
    for head, s in zip(heads, final):
        s_ref[head] = s
    gain = on_ref[...]
    for c in hcols:
        o_ref[:, c] = _bf(_rms_rows(o_s[:, c], gain) * _silu(z_ref[:, c]))


def _deltanet(u, conv_w, alog_row, dtb_row, out_norm, *, tb_rows=256, hs=8):
    t = u.shape[0]
    width = hs * LANES
    nb = DN_QK // width
    prev_blk = tb_rows // SUBLANES
    assert DN_DK == LANES and DN_DV == LANES and DN_HEADS % hs == 0

    def col(seg):
        return lambda i, h: (i, U_DN // width + seg * nb + h)

    def prev(seg):
        return lambda i, h: (jnp.maximum(i * prev_blk - 1, 0), U_DN // width + seg * nb + h)

    def cw(seg):
        return lambda i, h: (0, seg * nb + h)

    row = pl.BlockSpec((1, LANES), lambda i, h: (0, 0))
    blk = lambda seg: pl.BlockSpec((tb_rows, width), col(seg))
    pblk = lambda seg: pl.BlockSpec((SUBLANES, width), prev(seg))
    cblk = lambda seg: pl.BlockSpec((DN_CONV, width), cw(seg))
    n_chunks = tb_rows // CHUNK
    rows_f32 = pltpu.VMEM((tb_rows, LANES), F32)
    return pl.pallas_call(
        functools.partial(_dn_kernel, tb_rows=tb_rows, hs=hs),
        grid=(t // tb_rows, DN_HEADS // hs),
        in_specs=[blk(0), blk(1), blk(2), blk(3), pblk(0), pblk(1), pblk(2),
                  pl.BlockSpec((tb_rows, LANES), lambda i, h: (i, U_SMALL // LANES)),
                  cblk(0), cblk(1), cblk(2), row, row, row],
        out_specs=pl.BlockSpec((tb_rows, width), lambda i, h: (i, h)),
        out_shape=jax.ShapeDtypeStruct((t, DN_WIDTH), BF16),
        scratch_shapes=[
            pltpu.VMEM((tb_rows + SUBLANES, width), F32),
            rows_f32, rows_f32, rows_f32,
            pltpu.VMEM((hs, tb_rows, DN_DK), BF16),
            pltpu.VMEM((hs, tb_rows, DN_DV), F32),
            pltpu.VMEM((hs * n_chunks, DN_DK, DN_DK), BF16),
            pltpu.VMEM((hs * n_chunks, DN_DK, DN_DV), F32),
            pltpu.VMEM((hs * n_chunks, LANES), F32),
            pltpu.VMEM((tb_rows, width), F32),
            pltpu.VMEM((DN_HEADS, DN_DK, DN_DV), F32),
        ],
        compiler_params=_cparams(2),
        name="deltanet",
    )(u, u, u, u, u, u, u, u, conv_w, conv_w, conv_w, alog_row, dtb_row, out_norm)


def _gla_kernel(q_ref, k_ref, v_ref, r_ref, sm_ref, gu_ref, gb_ref, on_ref, o_ref,
                qg_s, oi_s, kv_s, gl_s, st_ref, *, tb_rows, hs):
    hstep = pl.program_id(1)
    n_pairs = tb_rows // PAIR
    n_chunks = 2 * n_pairs
    heads = [hstep * hs + hh for hh in range(hs)]
    kcols = [slice(hh * GLA_DK, (hh + 1) * GLA_DK) for hh in range(hs)]
    vcols = [slice(hh * GLA_DV, (hh + 1) * GLA_DV) for hh in range(hs)]

    @pl.when(pl.program_id(0) == 0)
    def _():
        for head in heads:
            st_ref[head] = jnp.zeros((GLA_DV, GLA_DK), F32)

    logits = jnp.dot(_bf(sm_ref[...]), gu_ref[...], preferred_element_type=F32) + gb_ref[...]
    la = jax.nn.log_sigmoid(logits) * (1.0 / GLA_TAU)

    sub, lane, same = _pair_masks()
    causal = same & (sub >= lane)
    tri = jnp.where(causal, 1.0, 0.0).astype(BF16)
    first = sub < CHUNK
    gain = on_ref[...]

    rows = [slice(p * PAIR, (p + 1) * PAIR) for p in range(n_pairs)]
    halves = [slice(c * CHUNK, (c + 1) * CHUNK) for c in range(2)]
    tiles = [(hh, p) for hh in range(hs) for p in range(n_pairs)]
    g = _chunk_cumsum(tri, [la[rows[p], kcols[hh]] for hh, p in tiles])
    k2 = [k_ref[rows[p], kcols[hh]] for hh, p in tiles]
    v2 = [_bf(v_ref[rows[p], vcols[hh]]) for hh, p in tiles]
    qg = [_bf(q_ref[rows[p], kcols[hh]] * (GLA_DK ** -0.5) * jnp.exp(gt))
          for (hh, p), gt in zip(tiles, g)]
    kg = _each(lambda kt, gt: _bf(kt * jnp.exp(-gt)), k2, g)
    a = _each(lambda qt, kt: _bf(jnp.where(causal, _mm_nt(qt, kt), 0.0)), qg, kg)
    oi = _each(_mm, a, v2)
    g_end = [[gt[CHUNK - 1:CHUNK, :], gt[PAIR - 1:PAIR, :]] for gt in g]
    kd = _each(lambda kt, gt, ge: _bf(kt * jnp.exp(jnp.where(first, ge[0], ge[1]) - gt)),
               k2, g, g_end)
    for t, (hh, p) in enumerate(tiles):
        qg_s[hh, rows[p], :] = qg[t]
        oi_s[hh, rows[p], :] = oi[t]
        for c, sl in enumerate(halves):
            ci = hh * n_chunks + 2 * p + c
            kv_s[ci] = _mm_tn(v2[t][sl], kd[t][sl])
            gl_s[ci:ci + 1, :] = jnp.exp(g_end[t][c])

    def step(c, states):
        rws = pl.ds(pl.multiple_of(c * CHUNK, CHUNK), CHUNK)
        for hh in range(hs):
            o_c = _mm_nt(qg_s[hh, rws, :], states[hh]) + oi_s[hh, rws, :]
            o_ref[rws, vcols[hh]] = _bf(_rms_rows(o_c, gain) * _silu(r_ref[rws, vcols[hh]]))
        return tuple(states[hh] * gl_s[pl.ds(hh * n_chunks + c, 1), :] + kv_s[hh * n_chunks + c]
                     for hh in range(hs))

    final = lax.fori_loop(0, n_chunks, step, tuple(st_ref[head] for head in heads), unroll=True)
    for head, st in zip(heads, final):
        st_ref[head] = st


def _gla(u, gate_up_pad, gate_bias, out_norm, *, tb_rows=512, hs=4):
    t = u.shape[0]
    kw, vw = hs * GLA_DK, hs * GLA_DV
    assert GLA_HEADS % hs == 0
    n_chunks = tb_rows // CHUNK
    return pl.pallas_call(
        functools.partial(_gla_kernel, tb_rows=tb_rows, hs=hs),
        grid=(t // tb_rows, GLA_HEADS // hs),
        in_specs=[
            pl.BlockSpec((tb_rows, kw), lambda i, h: (i, U_GLQ // kw + h)),
            pl.BlockSpec((tb_rows, kw), lambda i, h: (i, U_GLK // kw + h)),
            pl.BlockSpec((tb_rows, vw), lambda i, h: (i, U_GLV // vw + h)),
            pl.BlockSpec((tb_rows, vw), lambda i, h: (i, U_GLR // vw + h)),
            pl.BlockSpec((tb_rows, LANES), lambda i, h: (i, U_SMALL // LANES)),
            pl.BlockSpec((LANES, kw), lambda i, h: (0, h)),
            pl.BlockSpec((1, kw), lambda i, h: (0, h)),
            pl.BlockSpec((1, GLA_DV), lambda i, h: (0, 0)),
        ],
        out_specs=pl.BlockSpec((tb_rows, vw), lambda i, h: (i, h)),
        out_shape=jax.ShapeDtypeStruct((t, GLA_WIDTH), BF16),
        scratch_shapes=[
            pltpu.VMEM((hs, tb_rows, GLA_DK), BF16),
            pltpu.VMEM((hs, tb_rows, GLA_DV), F32),
            pltpu.VMEM((hs * n_chunks, GLA_DV, GLA_DK), F32),
            pltpu.VMEM((hs * n_chunks, GLA_DK), F32),
            pltpu.VMEM((GLA_HEADS, GLA_DV, GLA_DK), F32),
        ],
        compiler_params=_cparams(2),
        name="gla",
    )(u, u, u, u, u, gate_up_pad, gate_bias, out_norm)


def _swa_kernel(q_ref, kc_ref, kp_ref, vc_ref, vp_ref, qg_ref, kg_ref, sink_ref, o_ref):
    n = pl.program_id(0)
    w = SWA_WINDOW
    lane = lax.broadcasted_iota(jnp.int32, (w, LANES), 1)
    lo = lane < SWA_HD
    lane2 = lax.broadcasted_iota(jnp.int32, (2 * w, LANES), 1)
    lo2 = lane2 < SWA_HD
    qi = lax.broadcasted_iota(jnp.int32, (w, 2 * w), 0)
    ki = lax.broadcasted_iota(jnp.int32, (w, 2 * w), 1)
    mask = (ki <= qi + w) & (ki > qi + w - SWA_WINDOW) & ((n > 0) | (ki >= w))
    kk = jnp.concatenate([kp_ref[...], kc_ref[...]], axis=0)
    vv = jnp.concatenate([vp_ref[...], vc_ref[...]], axis=0)
    qgain = qg_ref[...]
    kgain = kg_ref[...]
    heads_per_grp = SWA_HQ // SWA_HKV
    pcols = [slice(p * LANES, (p + 1) * LANES) for p in range(SWA_HQ // 2)]

    def head_halves(t, g):
        tile = t[:, (g // 2) * LANES:(g // 2 + 1) * LANES]
        swapped = pltpu.roll(tile, SWA_HD, axis=1)
        return (tile, swapped) if g % 2 == 0 else (swapped, tile)

    def norm_k(kg):
        ms = jnp.sum(kg * kg, axis=-1, keepdims=True) * (1.0 / LANES)
        return _bf(kg * lax.rsqrt(ms + EPS) * kgain)

    def norm_q(qp):
        sq = qp * qp
        ms_lo = jnp.sum(jnp.where(lo, sq, 0.0), axis=-1, keepdims=True) * (1.0 / SWA_HD)
        ms_hi = jnp.sum(jnp.where(lo, 0.0, sq), axis=-1, keepdims=True) * (1.0 / SWA_HD)
        qn = qp * jnp.where(lo, lax.rsqrt(ms_lo + EPS), lax.rsqrt(ms_hi + EPS)) * qgain
        return qn * SWA_SCALE

    def probs(s, sink):
        s = jnp.where(mask, s, -jnp.inf)
        m = jnp.maximum(jnp.max(s, axis=-1, keepdims=True), sink)
        p = jnp.exp(s - m)
        inv = 1.0 / (jnp.sum(p, axis=-1, keepdims=True) + jnp.exp(sink - m))
        return _bf(p * inv)

    groups = range(SWA_HKV)
    k_lo_hi = [head_halves(kk, g) for g in groups]
    kn = [norm_k(jnp.where(lo2, k_lo, k_hi)) for k_lo, k_hi in k_lo_hi]
    v_lo_hi = [head_halves(vv, g) for g in groups]
    v_half = [(_bf(jnp.where(lo2, v_lo, 0.0)), _bf(jnp.where(lo2, 0.0, v_hi)))
              for v_lo, v_hi in v_lo_hi]
    qn = [norm_q(q_ref[:, c]) for c in pcols]
    heads = range(SWA_HQ)
    qm = [_bf(jnp.where(lo, qn[h // 2], 0.0) if h % 2 == 0 else jnp.where(lo, 0.0, qn[h // 2]))
          for h in heads]
    s = [_mm_nt(qm[h], kn[h // heads_per_grp]) for h in heads]
    p = [probs(s[h], sink_ref[h]) for h in heads]
    part = [jnp.dot(p[h], v_half[h // heads_per_grp][h % 2], preferred_element_type=F32)
            for h in heads]
    for pr, c in enumerate(pcols):
        o_ref[:, c] = _bf(part[2 * pr] + part[2 * pr + 1])


def _swa(u, qgain, kgain, sinks):
    t = u.shape[0]
    w = SWA_WINDOW
    kvw = SWA_KV
    assert U_SWK % kvw == 0 and U_SWV % kvw == 0 and U_SWQ % SWA_WIDTH == 0
    cur = lambda c: (lambda n: (n, c))
    prv = lambda c: (lambda n: (jnp.maximum(n - 1, 0), c))
    row = pl.BlockSpec((1, LANES), lambda n: (0, 0))
    return pl.pallas_call(
        _swa_kernel,
        grid=(t // w,),
        in_specs=[
            pl.BlockSpec((w, SWA_WIDTH), cur(U_SWQ // SWA_WIDTH)),
            pl.BlockSpec((w, kvw), cur(U_SWK // kvw)),
            pl.BlockSpec((w, kvw), prv(U_SWK // kvw)),
            pl.BlockSpec((w, kvw), cur(U_SWV // kvw)),
            pl.BlockSpec((w, kvw), prv(U_SWV // kvw)),
            row, row,
            pl.BlockSpec(memory_space=pltpu.SMEM),
        ],
        out_specs=pl.BlockSpec((w, SWA_WIDTH), lambda n: (n, 0)),
        out_shape=jax.ShapeDtypeStruct((t, SWA_WIDTH), BF16),
        compiler_params=_cparams(1),
        name="swa",
    )(u, u, u, u, u, qgain, kgain, sinks)


def _gatemix_kernel(h_ref, od_ref, og_ref, os_ref, g0_ref, g1_ref, g2_ref,
                    wd_ref, wg_ref, ws_ref, y_ref):
    dot = functools.partial(jnp.dot, preferred_element_type=F32)
    h = h_ref[...]
    y = (jax.nn.sigmoid(_mm_nt(h, g0_ref[...])) * dot(od_ref[...], wd_ref[...])
         + jax.nn.sigmoid(_mm_nt(h, g1_ref[...])) * dot(og_ref[...], wg_ref[...])
         + jax.nn.sigmoid(_mm_nt(h, g2_ref[...])) * dot(os_ref[...], ws_ref[...]))
    y_ref[...] = _bf(y)


def _gatemix(h, o_dn, o_gla, o_swa, wp, layer, w_dn, w_gla, w_swa, *, tm=1024, tj=512):
    t, d = h.shape
    nj = d // tj
    gate = lambda b: (lambda j, i: (layer, b * nj + j, 0))
    act = lambda width: pl.BlockSpec((tm, width), lambda j, i: (i, 0))
    wcol = lambda width: pl.BlockSpec((width, tj), lambda j, i: (0, j))
    return pl.pallas_call(
        _gatemix_kernel,
        grid=(nj, t // tm),
        in_specs=[
            act(d), act(DN_WIDTH), act(GLA_WIDTH), act(SWA_WIDTH),
            pl.BlockSpec((None, tj, d), gate(0)), pl.BlockSpec((None, tj, d), gate(1)),
            pl.BlockSpec((None, tj, d), gate(2)),
            wcol(DN_WIDTH), wcol(GLA_WIDTH), wcol(SWA_WIDTH),
        ],
        out_specs=pl.BlockSpec((tm, tj), lambda j, i: (i, j)),
        out_shape=jax.ShapeDtypeStruct((t, d), BF16),
        compiler_params=_cparams(2),
        name="gatemix",
    )(h, o_dn, o_gla, o_swa, wp, wp, wp, w_dn, w_gla, w_swa)


def _outproj_kernel(x_ref, y_ref, w_ref, o_ref):
    o_ref[...] = x_ref[...] + jnp.dot(y_ref[...], w_ref[...], preferred_element_type=F32)


def _outproj(x, y, w_o, *, tm=512):
    t, d = x.shape
    return pl.pallas_call(
        _outproj_kernel,
        grid=(t // tm,),
        in_specs=[
            pl.BlockSpec((tm, d), lambda i: (i, 0)),
            pl.BlockSpec((tm, d), lambda i: (i, 0)),
            pl.BlockSpec((d, d), lambda i: (0, 0)),
        ],
        out_specs=pl.BlockSpec((tm, d), lambda i: (i, 0)),
        out_shape=jax.ShapeDtypeStruct((t, d), F32),
        compiler_params=_cparams(1),
        name="outproj",
    )(x, y, w_o)


def _lane_row(vec):
    return jnp.pad(vec.astype(F32), (0, LANES - vec.shape[0]))[None, :]


def kernel(x, ffn1_norm, ffn1_w1, ffn1_w3, ffn1_w2, mix_norm, w_in, dn_conv, dn_a_log, dn_dt_bias, dn_out_norm, gla_gate_up, gla_gate_bias, gla_out_norm, swa_q_norm, swa_k_norm, swa_sinks, w_branch_dn, w_branch_gla, w_branch_swa, w_out, ffn2_norm, ffn2_w1, ffn2_w3, ffn2_w2):
    assert x.shape[0] == 1 and x.shape[2] == D_MODEL
    xs = x[0]
    wp = _relayout_w_in(jnp.swapaxes(w_in, 1, 2))
    for l in range(DEPTH):
        xs = _ffn(xs, ffn1_norm[l][None], ffn1_w1, ffn1_w3, ffn1_w2, l)
        u, h = _inproj(xs, mix_norm[l][None], wp, l)
        o_dn = _deltanet(u, dn_conv[l], _lane_row(dn_a_log[l]), _lane_row(dn_dt_bias[l]),
                         dn_out_norm[l][None])
        gate_up_pad = jnp.zeros((LANES, GLA_QK), BF16).at[SM_LR:SM_LR + GLA_RANK].set(
            _bf(gla_gate_up[l]))
        o_gla = _gla(u, gate_up_pad, gla_gate_bias[l][None], gla_out_norm[l][None])
        o_swa = _swa(u, jnp.tile(swa_q_norm[l], 2)[None], jnp.tile(swa_k_norm[l], 2)[None],
                     swa_sinks[l])
        y = _gatemix(h, o_dn, o_gla, o_swa, wp, l, _bf(w_branch_dn[l]), _bf(w_branch_gla[l]),
                     _bf(w_branch_swa[l]))
        xs = _outproj(xs, y, _bf(w_out[l]))
        xs = _ffn(xs, ffn2_norm[l][None], ffn2_w1, ffn2_w3, ffn2_w2, l)
    return xs[None]
```
